```python
import jax, jax.numpy as jnp
from jax import lax
import numpy as np

D_MODEL = 1024
BATCH = 8
SEQ = 2048
DEPTH = 4
DEC_BATCH = 128
DEC_SEQ = 8
PAST_LEN = 16384
PAGE_SIZE = 128

N_MIXERS = 2
EXPAND = 2
D_INNER = EXPAND * D_MODEL
HEAD_DIM = 64
N_HEADS = D_INNER // HEAD_DIM
N_GROUPS = 8
HEADS_PER_GROUP = N_HEADS // N_GROUPS
D_STATE = 128
CONV_WIDTH = 4
CONV_DIM = D_INNER + 2 * N_GROUPS * D_STATE
D_IN_PROJ = 2 * D_INNER + 2 * N_GROUPS * D_STATE + N_HEADS
CHUNK = 128
POOL_WIDTH = EXPAND * D_MODEL
POOL_WINDOWS = (2, 4, 8, 16)
N_POOL_GROUPS = len(POOL_WINDOWS)
POOL_GROUP_DIM = POOL_WIDTH // N_POOL_GROUPS
POOL_STATE = max(POOL_WINDOWS) - 1
N_SSD_LAYERS = (DEPTH + 1) // 2
N_POOL_LAYERS = DEPTH // 2
EPS = 1e-6

kernel_name = "hybrid_ssd_pool_adaln_step"


def rmsnorm(x, g):
    xf = x.astype(jnp.float32)
    return xf * lax.rsqrt(jnp.mean(xf * xf, axis=-1, keepdims=True) + EPS) * g.astype(jnp.float32)


def ssd_scan(x, dt, A, B, C, h0):
    b, L, H, P = x.shape
    T = min(CHUNK, L)
    nc = L // T
    G, R, N = N_GROUPS, HEADS_PER_GROUP, D_STATE
    x = x.reshape(b, nc, T, G, R, P)
    dt = dt.reshape(b, nc, T, G, R)
    B = B.reshape(b, nc, T, G, N)
    C = C.reshape(b, nc, T, G, N)
    acum = jnp.cumsum(dt * A.reshape(G, R), axis=2)
    seg = acum[:, :, :, None] - acum[:, :, None, :]
    mask = jnp.tril(jnp.ones((T, T), dtype=bool))[:, :, None, None]
    decay = jnp.where(mask, jnp.exp(jnp.where(mask, seg, 0.0)), 0.0)
    CB = jnp.einsum('bctgn,bcsgn->bctsg', C, B)
    xdt = x * dt[..., None]
    y_intra = jnp.einsum('bctsgr,bcsgrp->bctgrp', CB[..., None] * decay, xdt)
    decay_end = jnp.exp(acum[:, :, -1:] - acum)
    chunk_states = jnp.einsum('bctgn,bctgr,bctgrp->bcgrpn', B, decay_end, xdt)
    chunk_decay = jnp.exp(acum[:, :, -1])

    def step(h, inp):
        s, d = inp
        return d[..., None, None] * h + s, h

    h_final, h_prev = lax.scan(step, h0.reshape(b, G, R, P, N),
                               (jnp.moveaxis(chunk_states, 1, 0), jnp.moveaxis(chunk_decay, 1, 0)))
    h_prev = jnp.moveaxis(h_prev, 0, 1)
    y_inter = jnp.einsum('bctgn,bctgr,bcgrpn->bctgrp', C, jnp.exp(acum), h_prev)
    y = (y_intra + y_inter).reshape(b, L, H, P)
    return y, h_final.reshape(b, H, P, N)


def ssd_branch(h, conv_prev, ssm_prev, w_in, conv_w, conv_b, dt_bias, a_log, d_skip, norm_g, w_out):
    b, L, _ = h.shape
    zxbcdt = h @ w_in
    z = zxbcdt[..., :D_INNER]
    xbc = zxbcdt[..., D_INNER:D_INNER + CONV_DIM].astype(jnp.float32)
    dt_raw = zxbcdt[..., D_INNER + CONV_DIM:].astype(jnp.float32)
    xp = jnp.concatenate([conv_prev.astype(jnp.float32), xbc], axis=1)
    conv = conv_b.astype(jnp.float32) + sum(xp[:, k:k + L] * conv_w[k].astype(jnp.float32)
                                            for k in range(CONV_WIDTH))
    new_conv = xp[:, -(CONV_WIDTH - 1):]
    xbc_act = jax.nn.silu(conv)
    xs = xbc_act[..., :D_INNER].reshape(b, L, N_HEADS, HEAD_DIM)
    Bm = xbc_act[..., D_INNER:D_INNER + N_GROUPS * D_STATE].reshape(b, L, N_GROUPS, D_STATE)
    Cm = xbc_act[..., D_INNER + N_GROUPS * D_STATE:].reshape(b, L, N_GROUPS, D_STATE)
    dt = jax.nn.softplus(dt_raw + dt_bias.astype(jnp.float32))
    A = -jnp.exp(a_log.astype(jnp.float32))
    y, new_ssm = ssd_scan(xs, dt, A, Bm, Cm, ssm_prev.astype(jnp.float32))
    y = (y + d_skip.astype(jnp.float32)[:, None] * xs).reshape(b, L, D_INNER)
    gated = (y * jax.nn.silu(z.astype(jnp.float32))).reshape(b, L, N_GROUPS, D_INNER // N_GROUPS)
    gated = gated * lax.rsqrt(jnp.mean(gated * gated, axis=-1, keepdims=True) + EPS)
    gated = gated.reshape(b, L, D_INNER) * norm_g.astype(jnp.float32)
    return gated.astype(h.dtype) @ w_out, new_conv, new_ssm


def pool_branch(h, pool_prev, start_pos, w_in, w_group, ch_scale, w_out):
    b, L, _ = h.shape
    uz = h @ w_in
    u = uz[..., :POOL_WIDTH].astype(jnp.float32)
    z = uz[..., POOL_WIDTH:].astype(jnp.float32)
    up = jnp.concatenate([pool_prev.astype(jnp.float32), u], axis=1)
    cs = jnp.concatenate([jnp.zeros((b, 1, POOL_WIDTH), jnp.float32), jnp.cumsum(up, axis=1)], axis=1)
    off = POOL_STATE + 1
    pos = start_pos + jnp.arange(L)
    groups = []
    for g, w in enumerate(POOL_WINDOWS):
        sl = slice(g * POOL_GROUP_DIM, (g + 1) * POOL_GROUP_DIM)
        win_sum = cs[:, off:off + L, sl] - cs[:, off - w:off - w + L, sl]
        cnt = jnp.minimum(w, pos + 1).astype(jnp.float32)[None, :, None]
        groups.append(win_sum / cnt - u[..., sl])
    pooled = jnp.stack(groups, axis=2)
    mixed = jnp.einsum('blgc,gcd->blgd', pooled, w_group.astype(jnp.float32)).reshape(b, L, POOL_WIDTH)
    mixed = mixed * ch_scale.astype(jnp.float32) * jax.nn.silu(z)
    return mixed.astype(h.dtype) @ w_out, up[:, -POOL_STATE:]


def trunk(x, c, start_pos, ssm_in, conv_in, pool_in, ada_w, ada_b, norm_g,
          ssd_w_in, ssd_conv_w, ssd_conv_b, ssd_dt_bias, ssd_a_log, ssd_d, ssd_norm_g, ssd_w_out,
          pool_w_in, pool_w_group, pool_scale, pool_w_out, final_norm_g):
    new_ssm, new_conv, new_pool = [], [], []
    sc = jax.nn.silu(c.astype(jnp.float32))
    for i in range(DEPTH):
        mod = sc @ ada_w[i].astype(jnp.float32) + ada_b[i].astype(jnp.float32)
        shift, scale, gate = jnp.split(mod, 3, axis=-1)
        hn = (rmsnorm(x, norm_g[i]) * (1.0 + scale[:, None]) + shift[:, None]).astype(x.dtype)
        j = i // N_MIXERS
        if i % N_MIXERS == 0:
            out, cv, st = ssd_branch(hn, conv_in[j], ssm_in[j], ssd_w_in[j], ssd_conv_w[j], ssd_conv_b[j],
                                     ssd_dt_bias[j], ssd_a_log[j], ssd_d[j], ssd_norm_g[j], ssd_w_out[j])
            new_conv.append(cv)
            new_ssm.append(st)
        else:
            out, ps = pool_branch(hn, pool_in[j], start_pos, pool_w_in[j], pool_w_group[j],
                                  pool_scale[j], pool_w_out[j])
            new_pool.append(ps)
        x = (x.astype(jnp.float32) + (1.0 + gate[:, None]) * out.astype(jnp.float32)).astype(x.dtype)
    y = rmsnorm(x, final_norm_g).astype(x.dtype)
    return y, jnp.stack(new_ssm), jnp.stack(new_conv), jnp.stack(new_pool)


def setup_inputs(seed: int = 0) -> dict:
    key = jax.random.key(seed)
    ks = jax.random.split(key, 24)
    nrm = lambda k, shape, s: jax.random.normal(k, shape, jnp.float32) * s
    dt0 = jnp.exp(jax.random.uniform(ks[10], (N_SSD_LAYERS, N_HEADS), jnp.float32,
                                     np.log(1e-3), np.log(1e-1)))
    return {
        "x_prompt": nrm(ks[0], (BATCH, SEQ, D_MODEL), 1.0),
        "x_sample": nrm(ks[1], (DEC_BATCH, DEC_SEQ, D_MODEL), 1.0),
        "state_ssm": nrm(ks[2], (N_SSD_LAYERS, DEC_BATCH, N_HEADS, HEAD_DIM, D_STATE), 0.1),
        "state_conv": nrm(ks[3], (N_SSD_LAYERS, DEC_BATCH, CONV_WIDTH - 1, CONV_DIM), 1.0),
        "state_pool": nrm(ks[4], (N_POOL_LAYERS, DEC_BATCH, POOL_STATE, POOL_WIDTH), 1.0),
        "c_prompt": nrm(ks[5], (BATCH, D_MODEL), 1.0),
        "c_sample": nrm(ks[6], (DEC_BATCH, D_MODEL), 1.0),
        "ada_w": nrm(ks[7], (DEPTH, D_MODEL, 3 * D_MODEL), 0.1 * D_MODEL ** -0.5),
        "ada_b": nrm(ks[8], (DEPTH, 3 * D_MODEL), 0.02),
        "norm_g": 1.0 + nrm(ks[9], (DEPTH, D_MODEL), 0.02),
        "ssd_w_in": nrm(ks[11], (N_SSD_LAYERS, D_MODEL, D_IN_PROJ), D_MODEL ** -0.5),
        "ssd_conv_w": nrm(ks[12], (N_SSD_LAYERS, CONV_WIDTH, CONV_DIM), CONV_WIDTH ** -0.5),
        "ssd_conv_b": nrm(ks[13], (N_SSD_LAYERS, CONV_DIM), 0.02),
        "ssd_dt_bias": dt0 + jnp.log(-jnp.expm1(-dt0)),
        "ssd_a_log": jnp.log(jax.random.uniform(ks[14], (N_SSD_LAYERS, N_HEADS), jnp.float32, 1.0, 16.0)),
        "ssd_d": 1.0 + nrm(ks[15], (N_SSD_LAYERS, N_HEADS), 0.02),
        "ssd_norm_g": 1.0 + nrm(ks[16], (N_SSD_LAYERS, D_INNER), 0.02),
        "ssd_w_out": nrm(ks[17], (N_SSD_LAYERS, D_INNER, D_MODEL), D_INNER ** -0.5),
        "pool_w_in": nrm(ks[18], (N_POOL_LAYERS, D_MODEL, 2 * POOL_WIDTH), D_MODEL ** -0.5),
        "pool_w_group": nrm(ks[19], (N_POOL_LAYERS, N_POOL_GROUPS, POOL_GROUP_DIM, POOL_GROUP_DIM), POOL_GROUP_DIM ** -0.5),
        "pool_scale": 1.0 + nrm(ks[20], (N_POOL_LAYERS, POOL_WIDTH), 0.1),
        "pool_w_out": nrm(ks[21], (N_POOL_LAYERS, POOL_WIDTH, D_MODEL), POOL_WIDTH ** -0.5),
        "final_norm_g": 1.0 + nrm(ks[22], (D_MODEL,), 0.02),
    }


def reference(x_prompt, x_sample, state_ssm, state_conv, state_pool, c_prompt, c_sample,
              ada_w, ada_b, norm_g, ssd_w_in, ssd_conv_w, ssd_conv_b, ssd_dt_bias, ssd_a_log, ssd_d,
              ssd_norm_g, ssd_w_out, pool_w_in, pool_w_group, pool_scale, pool_w_out, final_norm_g):
    weights = (ada_w, ada_b, norm_g, ssd_w_in, ssd_conv_w, ssd_conv_b, ssd_dt_bias, ssd_a_log, ssd_d,
               ssd_norm_g, ssd_w_out, pool_w_in, pool_w_group, pool_scale, pool_w_out, final_norm_g)
    b = x_prompt.shape[0]
    ssm0 = jnp.zeros((N_SSD_LAYERS, b, N_HEADS, HEAD_DIM, D_STATE), jnp.float32)
    conv0 = jnp.zeros((N_SSD_LAYERS, b, CONV_WIDTH - 1, CONV_DIM), jnp.float32)
    pool0 = jnp.zeros((N_POOL_LAYERS, b, POOL_STATE, POOL_WIDTH), jnp.float32)
    y_prompt, ssm_p, conv_p, pool_p = trunk(x_prompt, c_prompt, 0, ssm0, conv0, pool0, *weights)
    y_sample, ssm_s, conv_s, pool_s = trunk(x_sample, c_sample, PAST_LEN, state_ssm, state_conv,
                                            state_pool, *weights)
    return (y_prompt, y_sample, ssm_p, conv_p, pool_p, ssm_s, conv_s, pool_s)
```

```python
import functools

import numpy as np
import jax
import jax.numpy as jnp
from jax import lax
from jax.experimental import pallas as pl
from jax.experimental.pallas import tpu as pltpu

D_MODEL = 1024
DEPTH = 4
PAST_LEN = 16384
N_MIXERS = 2
D_INNER = 2048
HEAD_DIM = 64
N_HEADS = 32
N_GROUPS = 8
HEADS_PER_GROUP = 4
D_STATE = 128
CONV_WIDTH = 4
CONV_DIM = D_INNER + 2 * N_GROUPS * D_STATE
GROUP_WIDTH = HEADS_PER_GROUP * HEAD_DIM
POOL_WIDTH = 2048
POOL_WINDOWS = (2, 4, 8, 16)
POOL_GROUP_DIM = 512
POOL_STATE = 15
EPS = 1e-6

LANES = 128
SUBLANES = 8
CHUNK = 128
PROMPT_TILE = 256
SSD_SEQ_BLOCK = 4
POOL_SEQ_BLOCK = 16
POOL_HIST = 2 * SUBLANES
VMEM_LIMIT = 56 * 1024 * 1024
NEG_BIG = -1e30

_F32 = jnp.float32
_BF16 = jnp.bfloat16
_HI = lax.Precision.HIGHEST


def _silu(v):
    return v * (1.0 / (1.0 + jnp.exp(-v)))


def _softplus(v):
    return jnp.maximum(v, 0.0) + jnp.log(1.0 + jnp.exp(-jnp.abs(v)))


def _dot(a, b):
    return jnp.dot(a, b, preferred_element_type=_F32)


def _dot_exact(a, b):
    return jnp.dot(a, b, precision=_HI, preferred_element_type=_F32)


def _dot_nt(a, b):
    return lax.dot_general(a, b, (((1,), (1,)), ((), ())), preferred_element_type=_F32)


def _dot_tn(a, b):
    return lax.dot_general(a, b, (((0,), (0,)), ((), ())), preferred_element_type=_F32)


def _prenorm(x, g, shift, scale):
    ms = jnp.mean(x * x, axis=-1, keepdims=True)
    return x * lax.rsqrt(ms + EPS) * g * (1.0 + scale) + shift


def _residual(x, out, gate, fin_ref, apply_final):
    y = x + (1.0 + gate) * out
    if apply_final:
        ms = jnp.mean(y * y, axis=-1, keepdims=True)
        y = y * lax.rsqrt(ms + EPS) * fin_ref[...]
    return y


def _group_rmsnorm_gate(y_ref, z_ref, ng_ref):
    parts = []
    for g in range(N_GROUPS):
        sl = slice(g * GROUP_WIDTH, (g + 1) * GROUP_WIDTH)
        gated = y_ref[:, sl] * _silu(z_ref[:, sl])
        ms = jnp.mean(gated * gated, axis=-1, keepdims=True)
        parts.append((gated * lax.rsqrt(ms + EPS) * ng_ref[:, sl]).astype(_BF16))
    return jnp.concatenate(parts, axis=1)


def _const_spec(shape):
    nd = len(shape)
    return pl.BlockSpec(shape, lambda *_: (0,) * nd, pipeline_mode=pl.Buffered(1))


def _params(n_axes):
    return pltpu.CompilerParams(dimension_semantics=("arbitrary",) * n_axes, vmem_limit_bytes=VMEM_LIMIT)


def _mod_kernel(c_ref, w_ref, b_ref, o_ref):
    sc = _silu(c_ref[...]).astype(_BF16)
    o_ref[...] = _dot(sc, w_ref[...].astype(_BF16)) + b_ref[...]


def _modulation(c_all, ada_w, ada_b):
    n = c_all.shape[0]
    tn = 1024
    return pl.pallas_call(
        _mod_kernel,
        grid=(DEPTH, 3 * D_MODEL // tn),
        in_specs=[
            pl.BlockSpec((n, D_MODEL), lambda i, j: (0, 0)),
            pl.BlockSpec((None, D_MODEL, tn), lambda i, j: (i, 0, j)),
            pl.BlockSpec((None, 1, tn), lambda i, j: (i, 0, j)),
        ],
        out_specs=pl.BlockSpec((None, n, tn), lambda i, j: (i, 0, j)),
        out_shape=jax.ShapeDtypeStruct((DEPTH, n, 3 * D_MODEL), _F32),
        compiler_params=_params(2),
        name="adaln_mod",
    )(c_all, ada_w, ada_b.reshape(DEPTH, 1, 3 * D_MODEL))


def _ssd_prompt_kernel(x_ref, mod_ref, g_ref, wz_ref, wxbc_ref, wdt_ref, cw_ref, cb_ref, dtb_ref,
                       alog_ref, dsk_ref, ng_ref, wout_ref, fin_ref,
                       xo_ref, nconv_ref, nssm_ref,
                       ht_ref, xpad_ref, act_ref, z_ref, y_ref, *, tile, n_tiles, apply_final):
    l = pl.program_id(1)

    @pl.when(l == 0)
    def _():
        ht_ref[...] = jnp.zeros_like(ht_ref)
        xpad_ref[0:SUBLANES, :] = jnp.zeros((SUBLANES, CONV_DIM), _F32)

    x = x_ref[...]
    hn = _prenorm(x, g_ref[...], mod_ref[0:1, :], mod_ref[1:2, :]).astype(_BF16)
    z_ref[...] = _dot(hn, wz_ref[...])
    xpad_ref[SUBLANES:SUBLANES + tile, :] = _dot(hn, wxbc_ref[...])
    dt = _softplus(_dot(hn, wdt_ref[...]) + dtb_ref[...])
    a = dt * (-jnp.exp(alog_ref[...]))

    base = SUBLANES - (CONV_WIDTH - 1)
    conv = cb_ref[...]
    for k in range(CONV_WIDTH):
        conv = conv + xpad_ref[base + k:base + k + tile, :] * cw_ref[k:k + 1, :]
    act_ref[...] = _silu(conv)
    tail = xpad_ref[SUBLANES + tile - (CONV_WIDTH - 1):SUBLANES + tile, :]
    xpad_ref[base:SUBLANES, :] = tail
    nconv_ref[...] = tail

    row = lax.broadcasted_iota(jnp.int32, (CHUNK, CHUNK), 0)
    col = lax.broadcasted_iota(jnp.int32, (CHUNK, CHUNK), 1)
    causal = col <= row
    tril = causal.astype(_F32)
    head_of_lane = lax.broadcasted_iota(jnp.int32, (CHUNK, GROUP_WIDTH), 1) // HEAD_DIM
    low_half = lax.broadcasted_iota(jnp.int32, (CHUNK, LANES), 1) < HEAD_DIM

    for c in range(tile // CHUNK):
        rows = slice(c * CHUNK, (c + 1) * CHUNK)
        dt_c = dt[rows, :]
        acum = _dot_exact(tril, a[rows, :])
        w_c = dt_c * jnp.exp(acum[CHUNK - 1:CHUNK, :] - acum)
        acum_t = acum.T
        dt_t = dt_c.T
        w_t = w_c.T
        for g in range(N_GROUPS):
            b_g = act_ref[rows, D_INNER + g * D_STATE:D_INNER + (g + 1) * D_STATE]
            c_lo = D_INNER + (N_GROUPS + g) * D_STATE
            c_bf = act_ref[rows, c_lo:c_lo + D_STATE].astype(_BF16)
            sl = slice(g * GROUP_WIDTH, (g + 1) * GROUP_WIDTH)
            xs_g = act_ref[rows, sl]
            bt_g = b_g.T
            cb = _dot(c_bf, bt_g.astype(_BF16))
            rhs = jnp.concatenate(
                [jnp.where(head_of_lane == hh, xs_g, 0.0).astype(_BF16) for hh in range(HEADS_PER_GROUP)],
                axis=0)
            m_parts, w_parts, e_cols = [], [], []
            for hh in range(HEADS_PER_GROUP):
                h = g * HEADS_PER_GROUP + hh
                acum_bc = jnp.broadcast_to(acum[:, h:h + 1], (CHUNK, CHUNK))
                dec = jnp.exp(jnp.where(causal, acum_bc - acum_t[h:h + 1, :], NEG_BIG))
                m_parts.append((cb * dec * dt_t[h:h + 1, :]).astype(_BF16))
                w_parts.append((bt_g * w_t[h:h + 1, :]).astype(_BF16))
                e_cols.append(jnp.exp(acum_bc))
            lhs = jnp.concatenate(
                [jnp.concatenate(m_parts, axis=1), jnp.concatenate(w_parts, axis=1)], axis=0)
            out = _dot(lhs, rhs)
            e_g = jnp.concatenate(
                [jnp.where(low_half, e_cols[0], e_cols[1]), jnp.where(low_half, e_cols[2], e_cols[3])],
                axis=1)
            ht_g = ht_ref[g]
            y_inter = _dot(c_bf, ht_g.astype(_BF16))
            y_ref[rows, sl] = out[0:CHUNK, :] + y_inter * e_g + dsk_ref[:, sl] * xs_g
            ht_ref[g] = ht_g * e_g[CHUNK - 1:CHUNK, :] + out[CHUNK:2 * CHUNK, :]

    gn = _group_rmsnorm_gate(y_ref, z_ref, ng_ref)
    xo_ref[...] = _residual(x, _dot(gn, wout_ref[...]), mod_ref[2:3, :], fin_ref, apply_final)

    @pl.when(l == n_tiles - 1)
    def _():
        for g in range(N_GROUPS):
            nssm_ref[g * GROUP_WIDTH:(g + 1) * GROUP_WIDTH, :] = ht_ref[g].T


_SSD_CONSTS = ("norm_g", "w_z", "w_xbc", "w_dt", "conv_w", "conv_b", "dt_bias", "a_log", "d_skip",
               "ssd_norm_g", "w_out", "final_g")
_POOL_CONSTS = ("norm_g", "w_u", "w_zp", "w_group", "pool_scale", "w_out", "final_g")


def _ssd_prompt_layer(x, mod, p, apply_final):
    b, L, _ = x.shape
    tile = PROMPT_TILE
    n_tiles = L // tile
    kern = functools.partial(_ssd_prompt_kernel, tile=tile, n_tiles=n_tiles, apply_final=apply_final)
    consts = [p[k] for k in _SSD_CONSTS]
    return pl.pallas_call(
        kern,
        grid=(b, n_tiles),
        in_specs=[pl.BlockSpec((None, tile, D_MODEL), lambda i, l: (i, l, 0)),
                  pl.BlockSpec((None, 3, D_MODEL), lambda i, l: (i, 0, 0))]
                 + [_const_spec(c.shape) for c in consts],
        out_specs=[pl.BlockSpec((None, tile, D_MODEL), lambda i, l: (i, l, 0)),
                   pl.BlockSpec((None, CONV_WIDTH - 1, CONV_DIM), lambda i, l: (i, 0, 0)),
                   pl.BlockSpec((None, D_INNER, D_STATE), lambda i, l: (i, 0, 0))],
        out_shape=[jax.ShapeDtypeStruct((b, L, D_MODEL), _F32),
                   jax.ShapeDtypeStruct((b, CONV_WIDTH - 1, CONV_DIM), _F32),
                   jax.ShapeDtypeStruct((b, D_INNER, D_STATE), _F32)],
        scratch_shapes=[pltpu.VMEM((N_GROUPS, D_STATE, GROUP_WIDTH), _F32),
                        pltpu.VMEM((SUBLANES + tile, CONV_DIM), _F32),
                        pltpu.VMEM((tile, CONV_DIM), _F32),
                        pltpu.VMEM((tile, D_INNER), _F32),
                        pltpu.VMEM((tile, D_INNER), _F32)],
        compiler_params=_params(2),
        name="ssd_prompt",
    )(x, mod, *consts)


def _ssd_sample_kernel(x_ref, mod_ref, g_ref, wz_ref, wxbc_ref, wdt_ref, cw_ref, cb_ref, dtb_ref,
                       alog_ref, dsk_ref, ng_ref, wout_ref, fin_ref,
                       cprev_ref, hprev_ref, exps_ref, exp64_ref, _alias_ref,
                       xo_ref, nconv_ref, nssm_ref,
                       xp_ref, act_ref, z_ref, mfac_ref, e_ref, xw_ref, y_ref, *, nseq, seq, apply_final):
    rows = nseq * seq
    x3 = x_ref[...]
    hn = _prenorm(x3, g_ref[...], mod_ref[:, 0:1, :], mod_ref[:, 1:2, :]).reshape(rows, D_MODEL).astype(_BF16)
    z_ref[...] = _dot(hn, wz_ref[...])
    base = SUBLANES - (CONV_WIDTH - 1)
    xp_ref[:, base:SUBLANES, :] = cprev_ref[...]
    xp_ref[:, SUBLANES:SUBLANES + seq, :] = _dot(hn, wxbc_ref[...]).reshape(nseq, seq, CONV_DIM)
    conv = cb_ref[...]
    for k in range(CONV_WIDTH):
        conv = conv + xp_ref[:, base + k:base + k + seq, :] * cw_ref[k:k + 1, :]
    act_ref[...] = _silu(conv)
    nconv_ref[...] = xp_ref[:, SUBLANES + seq - (CONV_WIDTH - 1):SUBLANES + seq, :]

    dt = _softplus(_dot(hn, wdt_ref[...]) + dtb_ref[...])
    a = dt * (-jnp.exp(alog_ref[...]))
    r_i = lax.broadcasted_iota(jnp.int32, (rows, rows), 0)
    c_i = lax.broadcasted_iota(jnp.int32, (rows, rows), 1)
    same_seq = (r_i // seq) == (c_i // seq)
    acum = _dot_exact((same_seq & (c_i <= r_i)).astype(_F32), a)
    atot = _dot_exact(same_seq.astype(_F32), a)
    w = dt * jnp.exp(atot - acum)

    z3 = _dot_exact(acum, exps_ref[...]).reshape(nseq, seq, N_GROUPS * LANES)
    dtz3 = _dot_exact(dt, exps_ref[...]).reshape(nseq, seq, N_GROUPS * LANES)
    t3 = lax.broadcasted_iota(jnp.int32, z3.shape, 1)
    s3 = lax.broadcasted_iota(jnp.int32, z3.shape, 2) % seq
    diag = t3 == s3
    acum_s = jnp.sum(jnp.where(diag, z3, 0.0), axis=1, keepdims=True)
    dt_s = jnp.sum(jnp.where(diag, dtz3, 0.0), axis=1, keepdims=True)
    mfac_ref[...] = jnp.exp(jnp.where(s3 <= t3, z3 - acum_s, NEG_BIG)) * dt_s
    e_ref[...] = _dot_exact(jnp.exp(acum), exp64_ref[...]).reshape(nseq, seq, D_INNER)
    xw_ref[...] = act_ref[:, :, 0:D_INNER] * _dot_exact(w, exp64_ref[...]).reshape(nseq, seq, D_INNER)

    rep = LANES // seq
    bd_r = lax.broadcasted_iota(jnp.int32, (LANES, GROUP_WIDTH), 0)
    bd_c = lax.broadcasted_iota(jnp.int32, (LANES, GROUP_WIDTH), 1)
    blockdiag = (bd_r // seq) == (bd_c // HEAD_DIM)
    t_row = lax.broadcasted_iota(jnp.int32, (seq, GROUP_WIDTH), 0)
    zeros_b = jnp.zeros((seq, D_STATE), _F32)
    ones_b = jnp.ones((seq, D_STATE), _F32)

    def per_seq(s, carry):
        for g in range(N_GROUPS):
            b_g = act_ref[s, :, D_INNER + g * D_STATE:D_INNER + (g + 1) * D_STATE]
            c_lo = D_INNER + (N_GROUPS + g) * D_STATE
            c_bf = act_ref[s, :, c_lo:c_lo + D_STATE].astype(_BF16)
            sl = slice(g * GROUP_WIDTH, (g + 1) * GROUP_WIDTH)
            xs_g = act_ref[s, :, sl]
            cbx = _dot_nt(c_bf, jnp.concatenate([b_g] * rep, axis=0).astype(_BF16))
            mp = (cbx * mfac_ref[s, :, g * LANES:(g + 1) * LANES]).astype(_BF16)
            rhs = jnp.where(blockdiag, jnp.concatenate([xs_g] * rep, axis=0), 0.0).astype(_BF16)
            h0 = hprev_ref[s, sl, :]
            e_g = e_ref[s, :, sl]
            y_ref[pl.ds(pl.multiple_of(s * seq, seq), seq), sl] = _dot(mp, rhs) + _dot_nt(c_bf, h0.astype(_BF16)) * e_g + dsk_ref[:, sl] * xs_g
            e_hi = e_g.astype(_BF16).astype(_F32)
            ez = jnp.where(t_row == seq - 1, e_hi, jnp.where(t_row == seq - 2, e_g - e_hi, 0.0))
            lhs_t = jnp.concatenate([xw_ref[s, :, sl], ez], axis=0).astype(_BF16)
            rhs_s = jnp.concatenate([jnp.concatenate([b_g, zeros_b], axis=1),
                                     jnp.concatenate([zeros_b, ones_b], axis=1)], axis=0).astype(_BF16)
            out = _dot_tn(lhs_t, rhs_s)
            nssm_ref[s, sl, :] = h0 * out[:, D_STATE:2 * D_STATE] + out[:, 0:D_STATE]
        return carry

    lax.fori_loop(0, nseq, per_seq, 0)

    gn = _group_rmsnorm_gate(y_ref, z_ref, ng_ref)
    out = _dot(gn, wout_ref[...]).reshape(nseq, seq, D_MODEL)
    xo_ref[...] = _residual(x3, out, mod_ref[:, 2:3, :], fin_ref, apply_final)


def _expansion_constants(seq):
    exps = np.zeros((LANES, N_GROUPS * LANES), np.float32)
    for h in range(N_HEADS):
        g, hh = divmod(h, HEADS_PER_GROUP)
        exps[h, g * LANES + hh * seq:g * LANES + (hh + 1) * seq] = 1.0
    exp64 = np.zeros((LANES, D_INNER), np.float32)
    for h in range(N_HEADS):
        exp64[h, h * HEAD_DIM:(h + 1) * HEAD_DIM] = 1.0
    return jnp.asarray(exps), jnp.asarray(exp64)


def _ssd_sample_layer(x, mod, p, j, state_conv, state_ssm, ssm_acc, apply_final):
    b, seq, _ = x.shape
    nseq = SSD_SEQ_BLOCK
    rows = nseq * seq
    kern = functools.partial(_ssd_sample_kernel, nseq=nseq, seq=seq, apply_final=apply_final)
    consts = [p[k] for k in _SSD_CONSTS]
    exps, exp64 = _expansion_constants(seq)
    n_layers = state_ssm.shape[0]
    ssm_shape = jax.ShapeDtypeStruct((n_layers, b, D_INNER, D_STATE), _F32)
    in_specs = ([pl.BlockSpec((nseq, seq, D_MODEL), lambda i: (i, 0, 0)),
                 pl.BlockSpec((nseq, 3, D_MODEL), lambda i: (i, 0, 0))]
                + [_const_spec(c.shape) for c in consts]
                + [pl.BlockSpec((None, nseq, CONV_WIDTH - 1, CONV_DIM), lambda i: (j, i, 0, 0)),
                   pl.BlockSpec((None, nseq, D_INNER, D_STATE), lambda i: (j, i, 0, 0)),
                   _const_spec(exps.shape), _const_spec(exp64.shape)])
    args = [x, mod, *consts, state_conv, state_ssm, exps, exp64]
    aliases = {}
    if ssm_acc is not None:
        in_specs.append(pl.BlockSpec(memory_space=pl.ANY))
        args.append(ssm_acc)
        aliases = {len(args) - 1: 2}
        kern_fn = kern
    else:
        kern_fn = lambda *refs: kern(*refs[:len(args)], None, *refs[len(args):])
    return pl.pallas_call(
        kern_fn,
        grid=(b // nseq,),
        in_specs=in_specs,
        out_specs=[pl.BlockSpec((nseq, seq, D_MODEL), lambda i: (i, 0, 0)),
                   pl.BlockSpec((nseq, CONV_WIDTH - 1, CONV_DIM), lambda i: (i, 0, 0)),
                   pl.BlockSpec((None, nseq, D_INNER, D_STATE), lambda i: (j, i, 0, 0))],
        out_shape=[jax.ShapeDtypeStruct((b, seq, D_MODEL), _F32),
                   jax.ShapeDtypeStruct((b, CONV_WIDTH - 1, CONV_DIM), _F32),
                   ssm_shape],
        scratch_shapes=[pltpu.VMEM((nseq, 2 * SUBLANES, CONV_DIM), _F32),
                        pltpu.VMEM((nseq, seq, CONV_DIM), _F32),
                        pltpu.VMEM((rows, D_INNER), _F32),
                        pltpu.VMEM((nseq, seq, N_GROUPS * LANES), _F32),
                        pltpu.VMEM((nseq, seq, D_INNER), _F32),
                        pltpu.VMEM((nseq, seq, D_INNER), _F32),
                        pltpu.VMEM((rows, D_INNER), _F32)],
        input_output_aliases=aliases,
        compiler_params=_params(1),
        name="ssd_sample",
    )(*args)


def _pool_mix(load_rows, z, wg_ref, ps_ref, pos):
    parts = []
    for g, w in enumerate(POOL_WINDOWS):
        sl = slice(g * POOL_GROUP_DIM, (g + 1) * POOL_GROUP_DIM)
        u_g = load_rows(0, sl)
        win = u_g
        for k in range(1, w):
            win = win + load_rows(k, sl)
        cnt = jnp.minimum(float(w), pos + 1.0)
        pooled = (win / cnt - u_g).astype(_BF16)
        mixed = _dot(pooled, wg_ref[g])
        parts.append((mixed * ps_ref[:, sl] * _silu(z[:, sl])).astype(_BF16))
    return jnp.concatenate(parts, axis=1)


def _pool_prompt_kernel(x_ref, mod_ref, g_ref, wu_ref, wz_ref, wg_ref, ps_ref, wout_ref, fin_ref,
                        xo_ref, npool_ref, up_ref, *, tile, apply_final):
    l = pl.program_id(1)
    hist = POOL_HIST

    @pl.when(l == 0)
    def _():
        up_ref[0:hist, :] = jnp.zeros((hist, POOL_WIDTH), _F32)

    x = x_ref[...]
    hn = _prenorm(x, g_ref[...], mod_ref[0:1, :], mod_ref[1:2, :]).astype(_BF16)
    up_ref[hist:hist + tile, :] = _dot(hn, wu_ref[...])
    z = _dot(hn, wz_ref[...])
    pos = (l * tile + lax.broadcasted_iota(jnp.int32, (tile, POOL_GROUP_DIM), 0)).astype(_F32)
    mixed = _pool_mix(lambda k, sl: up_ref[hist - k:hist - k + tile, sl], z, wg_ref, ps_ref, pos)
    xo_ref[...] = _residual(x, _dot(mixed, wout_ref[...]), mod_ref[2:3, :], fin_ref, apply_final)
    tail = up_ref[hist + tile - POOL_STATE:hist + tile, :]
    up_ref[hist - POOL_STATE:hist, :] = tail
    npool_ref[...] = tail


def _pool_prompt_layer(x, mod, p, apply_final):
    b, L, _ = x.shape
    tile = PROMPT_TILE
    kern = functools.partial(_pool_prompt_kernel, tile=tile, apply_final=apply_final)
    consts = [p[k] for k in _POOL_CONSTS]
    return pl.pallas_call(
        kern,
        grid=(b, L // tile),
        in_specs=[pl.BlockSpec((None, tile, D_MODEL), lambda i, l: (i, l, 0)),
                  pl.BlockSpec((None, 3, D_MODEL), lambda i, l: (i, 0, 0))]
                 + [_const_spec(c.shape) for c in consts],
        out_specs=[pl.BlockSpec((None, tile, D_MODEL), lambda i, l: (i, l, 0)),
                   pl.BlockSpec((None, POOL_STATE, POOL_WIDTH), lambda i, l: (i, 0, 0))],
        out_shape=[jax.ShapeDtypeStruct((b, L, D_MODEL), _F32),
                   jax.ShapeDtypeStruct((b, POOL_STATE, POOL_WIDTH), _F32)],
        scratch_shapes=[pltpu.VMEM((POOL_HIST + tile, POOL_WIDTH), _F32)],
        compiler_params=_params(2),
        name="pool_prompt",
    )(x, mod, *consts)


def _pool_sample_kernel(x_ref, mod_ref, g_ref, wu_ref, wz_ref, wg_ref, ps_ref, wout_ref, fin_ref, prev_ref,
                        xo_ref, npool_ref, up_ref, *, nseq, seq, start_pos, apply_final):
    rows = nseq * seq
    hist = POOL_HIST
    x3 = x_ref[...]
    hn = _prenorm(x3, g_ref[...], mod_ref[:, 0:1, :], mod_ref[:, 1:2, :]).reshape(rows, D_MODEL).astype(_BF16)
    up_ref[:, hist - POOL_STATE:hist, :] = prev_ref[...]
    up_ref[:, hist:hist + seq, :] = _dot(hn, wu_ref[...]).reshape(nseq, seq, POOL_WIDTH)
    z = _dot(hn, wz_ref[...])
    pos3 = start_pos + lax.broadcasted_iota(jnp.int32, (nseq, seq, POOL_GROUP_DIM), 1)
    pos = pos3.astype(_F32).reshape(rows, POOL_GROUP_DIM)
    mixed = _pool_mix(lambda k, sl: up_ref[:, hist - k:hist - k + seq, sl].reshape(rows, POOL_GROUP_DIM),
                      z, wg_ref, ps_ref, pos)
    out = _dot(mixed, wout_ref[...]).reshape(nseq, seq, D_MODEL)
    xo_ref[...] = _residual(x3, out, mod_ref[:, 2:3, :], fin_ref, apply_final)
    npool_ref[...] = up_ref[:, hist + seq - POOL_STATE:hist + seq, :]


def _pool_sample_layer(x, mod, p, j, state_pool, start_pos, apply_final):
    b, seq, _ = x.shape
    nseq = POOL_SEQ_BLOCK
    kern = functools.partial(_pool_sample_kernel, nseq=nseq, seq=seq, start_pos=start_pos,
                             apply_final=apply_final)
    consts = [p[k] for k in _POOL_CONSTS]
    return pl.pallas_call(
        kern,
        grid=(b // nseq,),
        in_specs=[pl.BlockSpec((nseq, seq, D_MODEL), lambda i: (i, 0, 0)),
                  pl.BlockSpec((nseq, 3, D_MODEL), lambda i: (i, 0, 0))]
                 + [_const_spec(c.shape) for c in consts]
                 + [pl.BlockSpec((None, nseq, POOL_STATE, POOL_WIDTH), lambda i: (j, i, 0, 0))],
        out_specs=[pl.BlockSpec((nseq, seq, D_MODEL), lambda i: (i, 0, 0)),
                   pl.BlockSpec((nseq, POOL_STATE, POOL_WIDTH), lambda i: (i, 0, 0))],
        out_shape=[jax.ShapeDtypeStruct((b, seq, D_MODEL), _F32),
                   jax.ShapeDtypeStruct((b, POOL_STATE, POOL_WIDTH), _F32)],
        scratch_shapes=[pltpu.VMEM((nseq, POOL_HIST + seq, POOL_WIDTH), _F32)],
        compiler_params=_params(1),
        name="pool_sample",
    )(x, mod, *consts, state_pool)


def _layer_params(i, norm_g, ssd_w_in, ssd_conv_w, ssd_conv_b, ssd_dt_bias, ssd_a_log, ssd_d, ssd_norm_g,
                  ssd_w_out, pool_w_in, pool_w_group, pool_scale, pool_w_out, final_norm_g):
    j = i // N_MIXERS
    p = {"norm_g": norm_g[i].reshape(1, D_MODEL), "final_g": final_norm_g.reshape(1, D_MODEL)}
    if i % N_MIXERS == 0:
        w_in = ssd_w_in[j]
        pad = LANES - N_HEADS
        p.update(
            w_z=w_in[:, :D_INNER].astype(_BF16),
            w_xbc=w_in[:, D_INNER:D_INNER + CONV_DIM].astype(_BF16),
            w_dt=jnp.pad(w_in[:, D_INNER + CONV_DIM:], ((0, 0), (0, pad))).astype(_BF16),
            conv_w=ssd_conv_w[j],
            conv_b=ssd_conv_b[j].reshape(1, CONV_DIM),
            dt_bias=jnp.pad(ssd_dt_bias[j], (0, pad)).reshape(1, LANES),
            a_log=jnp.pad(ssd_a_log[j], (0, pad)).reshape(1, LANES),
            d_skip=jnp.repeat(ssd_d[j], HEAD_DIM).reshape(1, D_INNER),
            ssd_norm_g=ssd_norm_g[j].reshape(1, D_INNER),
            w_out=ssd_w_out[j].astype(_BF16),
        )
    else:
        w_in = pool_w_in[j]
        p.update(
            w_u=w_in[:, :POOL_WIDTH].astype(_BF16),
            w_zp=w_in[:, POOL_WIDTH:].astype(_BF16),
            w_group=pool_w_group[j].astype(_BF16),
            pool_scale=pool_scale[j].reshape(1, POOL_WIDTH),
            w_out=pool_w_out[j].astype(_BF16),
        )
    return p


def kernel(x_prompt, x_sample, state_ssm, state_conv, state_pool, c_prompt, c_sample, ada_w, ada_b, norm_g,
           ssd_w_in, ssd_conv_w, ssd_conv_b, ssd_dt_bias, ssd_a_log, ssd_d, ssd_norm_g, ssd_w_out, pool_w_in,
           pool_w_group, pool_scale, pool_w_out, final_norm_g):
    b_p = x_prompt.shape[0]
    b_s = x_sample.shape[0]
    n_ssd = state_ssm.shape[0]
    mod = _modulation(jnp.concatenate([c_prompt, c_sample], axis=0), ada_w, ada_b)
    mod_p = mod[:, :b_p].reshape(DEPTH, b_p, 3, D_MODEL)
    mod_s = mod[:, b_p:].reshape(DEPTH, b_s, 3, D_MODEL)
    ssm_in = state_ssm.reshape(n_ssd, b_s, D_INNER, D_STATE)

    xp, xs = x_prompt, x_sample
    ssm_p, conv_p, pool_p, conv_s, pool_s = [], [], [], [], []
    ssm_s = None
    for i in range(DEPTH):
        p = _layer_params(i, norm_g, ssd_w_in, ssd_conv_w, ssd_conv_b, ssd_dt_bias, ssd_a_log, ssd_d,
                          ssd_norm_g, ssd_w_out, pool_w_in, pool_w_group, pool_scale, pool_w_out,
                          final_norm_g)
        j = i // N_MIXERS
        last = i == DEPTH - 1
        if i % N_MIXERS == 0:
            xp, cv, st = _ssd_prompt_layer(xp, mod_p[i], p, last)
            conv_p.append(cv)
            ssm_p.append(st)
            xs, cv, ssm_s = _ssd_sample_layer(xs, mod_s[i], p, j, state_conv, ssm_in, ssm_s, last)
            conv_s.append(cv)
        else:
            xp, ps = _pool_prompt_layer(xp, mod_p[i], p, last)
            pool_p.append(ps)
            xs, ps = _pool_sample_layer(xs, mod_s[i], p, j, state_pool, PAST_LEN, last)
            pool_s.append(ps)

    head_shape = (N_HEADS, HEAD_DIM, D_STATE)
    return (xp, xs,
            jnp.stack(ssm_p).reshape(n_ssd, b_p, *head_shape),
            jnp.stack(conv_p), jnp.stack(pool_p),
            ssm_s.reshape(n_ssd, b_s, *head_shape),
            jnp.stack(conv_s), jnp.stack(pool_s))
```

```python
import functools

import numpy as np
import jax
import jax.numpy as jnp
from jax import lax
from jax.experimental import pallas as pl
from jax.experimental.pallas import tpu as pltpu

D_MODEL = 1024
DEPTH = 4
PAST_LEN = 16384
N_MIXERS = 2
D_INNER = 2048
HEAD_DIM = 64
N_HEADS = 32
N_GROUPS = 8
HEADS_PER_GROUP = 4
D_STATE = 128
CONV_WIDTH = 4
CONV_DIM = D_INNER + 2 * N_GROUPS * D_STATE
GROUP_WIDTH = HEADS_PER_GROUP * HEAD_DIM
POOL_WIDTH = 2048
POOL_WINDOWS = (2, 4, 8, 16)
POOL_GROUP_DIM = 512
POOL_STATE = 15
EPS = 1e-6

LANES = 128
SUBLANES = 8
CHUNK = 128
N_PHASES = CHUNK // SUBLANES
PROMPT_TILE = 256
SSD_SEQ_BLOCK = 8
POOL_SEQ_BLOCK = 16
POOL_HIST = 2 * SUBLANES
WEIGHT_COLS = 2048
PIECE_COLS = 256
PROJ_COLS = 1024
VMEM_LIMIT = 56 * 1024 * 1024
NEG_BIG = -1e30
LOG2E = 1.4426950408889634

_F32 = jnp.float32
_BF16 = jnp.bfloat16
_HI = lax.Precision.HIGHEST


def _silu(v):
    h = 0.5 * v
    return h + h * jnp.tanh(h)


def _softplus(v):
    return jnp.maximum(v, 0.0) + jnp.log(1.0 + jnp.exp(-jnp.abs(v)))


def _dot(a, b):
    return jnp.dot(a, b, preferred_element_type=_F32)


def _dot_exact(a, b):
    return jnp.dot(a, b, precision=_HI, preferred_element_type=_F32)


def _dot_nt(a, b):
    return lax.dot_general(a, b, (((1,), (1,)), ((), ())), preferred_element_type=_F32)


def _dot_tn(a, b):
    return lax.dot_general(a, b, (((0,), (0,)), ((), ())), preferred_element_type=_F32)


def _select_lanes(v, onehot_bf, terms):
    acc = None
    rest = v
    for _ in range(terms):
        piece = rest.astype(_BF16)
        part = _dot(piece, onehot_bf)
        acc = part if acc is None else acc + part
        rest = rest - piece.astype(_F32)
    return acc


def _prenorm(x, g, shift, scale):
    ms = jnp.mean(x * x, axis=-1, keepdims=True)
    return x * lax.rsqrt(ms + EPS) * g * (1.0 + scale) + shift


def _residual(x, out, gate, fin_ref, apply_final):
    y = x + (1.0 + gate) * out
    if apply_final:
        ms = jnp.mean(y * y, axis=-1, keepdims=True)
        y = y * lax.rsqrt(ms + EPS) * fin_ref[...]
    return y


def _group_rmsnorm_gate(y_ref, z_ref, ng_ref):
    parts = []
    for g in range(N_GROUPS):
        sl = slice(g * GROUP_WIDTH, (g + 1) * GROUP_WIDTH)
        gated = y_ref[:, sl] * _silu(z_ref[:, sl])
        ms = jnp.mean(gated * gated, axis=-1, keepdims=True)
        parts.append((gated * lax.rsqrt(ms + EPS) * ng_ref[:, sl]).astype(_BF16))
    return jnp.concatenate(parts, axis=1)


def _wspec(block_shape, index):
    return pl.BlockSpec(block_shape, lambda *_: index, pipeline_mode=pl.Buffered(1))


def _params(n_axes):
    return pltpu.CompilerParams(dimension_semantics=("arbitrary",) * n_axes, vmem_limit_bytes=VMEM_LIMIT)


def _operands(w, i):
    j = i // N_MIXERS
    wcols = (None, D_MODEL, WEIGHT_COLS)
    ops = {"norm_g": (w["norm_g"], _wspec((None, 1, D_MODEL), (i, 0, 0))),
           "final_g": (w["final_g"], _wspec((1, D_MODEL), (0, 0)))}
    if i % N_MIXERS == 0:
        ops.update(
            w_z=(w["ssd_w_in"], _wspec(wcols, (j, 0, 0))),
            w_x=(w["ssd_w_in"], _wspec(wcols, (j, 0, 1))),
            w_bc=(w["ssd_w_in"], _wspec(wcols, (j, 0, 2))),
            w_dt=(w["ssd_w_dt"], _wspec((None, D_MODEL, LANES), (j, 0, 0))),
            conv_w=(w["ssd_conv_w"], _wspec((None, CONV_WIDTH, CONV_DIM), (j, 0, 0))),
            conv_b=(w["ssd_conv_b"], _wspec((None, 1, CONV_DIM), (j, 0, 0))),
            dt_bias=(w["ssd_dt_bias"], _wspec((None, 1, LANES), (j, 0, 0))),
            a_log=(w["ssd_a_log"], _wspec((None, 1, LANES), (j, 0, 0))),
            d_skip=(w["ssd_d"], _wspec((None, 1, D_INNER), (j, 0, 0))),
            ssd_norm_g=(w["ssd_norm_g"], _wspec((None, 1, D_INNER), (j, 0, 0))),
            w_out=(w["ssd_w_out"], _wspec((None, D_INNER, D_MODEL), (j, 0, 0))),
        )
    else:
        ops.update(
            w_u=(w["pool_w_in"], _wspec(wcols, (j, 0, 0))),
            w_zp=(w["pool_w_in"], _wspec(wcols, (j, 0, 1))),
            w_group=(w["pool_w_group"],
                     _wspec((None, len(POOL_WINDOWS), POOL_GROUP_DIM, POOL_GROUP_DIM), (j, 0, 0, 0))),
            pool_scale=(w["pool_scale"], _wspec((None, 1, POOL_WIDTH), (j, 0, 0))),
            w_out=(w["pool_w_out"], _wspec((None, POOL_WIDTH, D_MODEL), (j, 0, 0))),
        )
    return ops


def _pick(ops, names):
    return [ops[n][0] for n in names], [ops[n][1] for n in names]


def _mod_kernel(c_ref, w_ref, b_ref, o_ref):
    sc = _silu(c_ref[...]).astype(_BF16)
    o_ref[...] = _dot(sc, w_ref[...].astype(_BF16)) + b_ref[...]


def _modulation(c_all, ada_w, ada_b):
    n = c_all.shape[0]
    tn = 1024
    return pl.pallas_call(
        _mod_kernel,
        grid=(DEPTH, 3 * D_MODEL // tn),
        in_specs=[
            pl.BlockSpec((n, D_MODEL), lambda i, j: (0, 0)),
            pl.BlockSpec((None, D_MODEL, tn), lambda i, j: (i, 0, j)),
            pl.BlockSpec((None, 1, tn), lambda i, j: (i, 0, j)),
        ],
        out_specs=pl.BlockSpec((None, n, tn), lambda i, j: (i, 0, j)),
        out_shape=jax.ShapeDtypeStruct((DEPTH, n, 3 * D_MODEL), _F32),
        compiler_params=_params(2),
        name="adaln_mod",
    )(c_all, ada_w, ada_b.reshape(DEPTH, 1, 3 * D_MODEL))


_SSD_PROMPT_OPS = ("norm_g", "w_z", "w_x", "w_bc", "w_dt", "conv_w", "conv_b", "dt_bias", "a_log", "d_skip",
                   "ssd_norm_g", "w_out", "final_g")


def _ssd_prompt_kernel(*refs, tile, n_tiles, apply_final):
    n_xcols = D_MODEL // LANES
    x_refs = refs[:n_xcols]
    (mod_ref, g_ref, wz_ref, wx_ref, wbc_ref, wdt_ref, cw_ref, cb_ref, dtb_ref, alog_ref, dsk_ref, ng_ref,
     wout_ref, fin_ref, xo_ref, nconv_ref, nssm_ref,
     ht_ref, carry_ref, act_ref, z_ref, y_ref, stage_ref) = refs[n_xcols:]
    l = pl.program_id(1)
    n_chunks = tile // CHUNK
    n_carry = CONV_WIDTH - 1

    @pl.when(l == 0)
    def _():
        ht_ref[...] = jnp.zeros_like(ht_ref)
        carry_ref[...] = jnp.zeros_like(carry_ref)

    x = jnp.concatenate(
        [jnp.concatenate([xr[pl.ds(c * CHUNK + r, SUBLANES, stride=N_PHASES), :] for xr in x_refs], axis=1)
         for c in range(n_chunks) for r in range(N_PHASES)], axis=0)
    hn = _prenorm(x, g_ref[...], mod_ref[0:1, :], mod_ref[1:2, :]).astype(_BF16)
    dt = _softplus(_dot(hn, wdt_ref[...]) + dtb_ref[...])

    last = (N_PHASES - n_carry) * SUBLANES
    for k in range(CONV_DIM // PIECE_COLS):
        cols = slice(k * PIECE_COLS, (k + 1) * PIECE_COLS)
        half = D_INNER // PIECE_COLS
        w_ref, kw = (wx_ref, k) if k < half else (wbc_ref, k - half)
        xb = _dot(hn, w_ref[:, kw * PIECE_COLS:(kw + 1) * PIECE_COLS])
        prev = carry_ref[:, cols]
        for c in range(n_chunks):
            cur = xb[c * CHUNK:(c + 1) * CHUNK, :]
            tail = cur[last:CHUNK, :]
            wrapped = [jnp.concatenate([prev[j * SUBLANES + SUBLANES - 1:(j + 1) * SUBLANES, :],
                                        tail[j * SUBLANES:(j + 1) * SUBLANES - 1, :]], axis=0)
                       for j in range(n_carry)]
            ext = jnp.concatenate(wrapped + [cur], axis=0)
            conv = cb_ref[:, cols]
            for kk in range(CONV_WIDTH):
                conv = conv + ext[kk * SUBLANES:kk * SUBLANES + CHUNK, :] * cw_ref[kk:kk + 1, cols]
            act_ref[c * CHUNK:(c + 1) * CHUNK, cols] = _silu(conv)
            prev = tail
        carry_ref[:, cols] = prev
        for j in range(n_carry):
            nconv_ref[j:j + 1, cols] = prev[(j + 1) * SUBLANES - 1:(j + 1) * SUBLANES, :]

    row = lax.broadcasted_iota(jnp.int32, (CHUNK, CHUNK), 0)
    col = lax.broadcasted_iota(jnp.int32, (CHUNK, CHUNK), 1)
    token = lambda p: (p % SUBLANES) * N_PHASES + p // SUBLANES
    causal = token(col) <= token(row)
    tril = causal.astype(_F32)
    head_of_lane = lax.broadcasted_iota(jnp.int32, (CHUNK, GROUP_WIDTH), 1) // HEAD_DIM
    low_half = lax.broadcasted_iota(jnp.int32, (CHUNK, LANES), 1) < HEAD_DIM
    neg_a = -jnp.exp(alog_ref[...])

    for c in range(n_chunks):
        rows = slice(c * CHUNK, (c + 1) * CHUNK)
        dt_c = dt[rows, :]
        acum = _dot_exact(tril, dt_c * neg_a)
        acum2 = acum * LOG2E
        w_c = dt_c * jnp.exp(acum[CHUNK - 1:CHUNK, :] - acum)
        srow_t = (jnp.log(dt_c) * LOG2E - acum2).T
        w_t = w_c.T
        for g in range(N_GROUPS):
            b_g = act_ref[rows, D_INNER + g * D_STATE:D_INNER + (g + 1) * D_STATE]
            c_lo = D_INNER + (N_GROUPS + g) * D_STATE
            c_bf = act_ref[rows, c_lo:c_lo + D_STATE].astype(_BF16)
            sl = slice(g * GROUP_WIDTH, (g + 1) * GROUP_WIDTH)
            xs_g = act_ref[rows, sl]
            bt_g = b_g.T
            cb = _dot(c_bf, bt_g.astype(_BF16))
            rhs = jnp.concatenate(
                [jnp.where(head_of_lane == hh, xs_g, 0.0).astype(_BF16) for hh in range(HEADS_PER_GROUP)],
                axis=0)
            m_parts, w_parts, e_cols = [], [], []
            for hh in range(HEADS_PER_GROUP):
                h = g * HEADS_PER_GROUP + hh
                acum_bc = jnp.broadcast_to(acum2[:, h:h + 1], (CHUNK, CHUNK))
                seg = acum_bc + srow_t[h:h + 1, :]
                m_parts.append((cb * jnp.exp2(jnp.where(causal, seg, NEG_BIG))).astype(_BF16))
                w_parts.append((bt_g * w_t[h:h + 1, :]).astype(_BF16))
                e_cols.append(jnp.exp2(acum_bc))
            lhs = jnp.concatenate(
                [jnp.concatenate(m_parts, axis=1), jnp.concatenate(w_parts, axis=1)], axis=0)
            out = _dot(lhs, rhs)
            e_g = jnp.concatenate(
                [jnp.where(low_half, e_cols[0], e_cols[1]), jnp.where(low_half, e_cols[2], e_cols[3])],
                axis=1)
            ht_g = ht_ref[g]
            y_inter = _dot(c_bf, ht_g.astype(_BF16))
            y_ref[rows, sl] = out[0:CHUNK, :] + y_inter * e_g + dsk_ref[:, sl] * xs_g
            ht_ref[g] = ht_g * e_g[CHUNK - 1:CHUNK, :] + out[CHUNK:2 * CHUNK, :]
            piece = c * N_GROUPS + g
            if piece < D_INNER // PIECE_COLS:
                zc = slice(piece * PIECE_COLS, (piece + 1) * PIECE_COLS)
                z_ref[:, zc] = _dot(hn, wz_ref[:, zc])

    gn = _group_rmsnorm_gate(y_ref, z_ref, ng_ref)
    xo = _residual(x, _dot(gn, wout_ref[...]), mod_ref[2:3, :], fin_ref, apply_final)
    for c in range(n_chunks):
        for r in range(N_PHASES):
            p0 = c * CHUNK + r * SUBLANES
            for j in range(n_xcols):
                stage_ref[j, pl.ds(c * CHUNK + r, SUBLANES, stride=N_PHASES), :] = (
                    xo[p0:p0 + SUBLANES, j * LANES:(j + 1) * LANES])
    for j in range(n_xcols):
        xo_ref[:, j * LANES:(j + 1) * LANES] = stage_ref[j]

    @pl.when(l == n_tiles - 1)
    def _():
        for g in range(N_GROUPS):
            nssm_ref[g * GROUP_WIDTH:(g + 1) * GROUP_WIDTH, :] = ht_ref[g].T


def _ssd_prompt_layer(x, mod, w, i, apply_final):
    b, L, _ = x.shape
    tile = PROMPT_TILE
    n_tiles = L // tile
    assert tile // CHUNK * N_GROUPS >= D_INNER // PIECE_COLS
    kern = functools.partial(_ssd_prompt_kernel, tile=tile, n_tiles=n_tiles, apply_final=apply_final)
    arrays, specs = _pick(_operands(w, i), _SSD_PROMPT_OPS)
    n_xcols = D_MODEL // LANES
    x_specs = [pl.BlockSpec((None, tile, LANES), lambda s, l, j=j: (s, l, j)) for j in range(n_xcols)]
    return pl.pallas_call(
        kern,
        grid=(b, n_tiles),
        in_specs=x_specs + [pl.BlockSpec((None, 3, D_MODEL), lambda s, l: (s, 0, 0))] + specs,
        out_specs=[pl.BlockSpec((None, tile, D_MODEL), lambda s, l: (s, l, 0)),
                   pl.BlockSpec((None, CONV_WIDTH - 1, CONV_DIM), lambda s, l: (s, 0, 0)),
                   pl.BlockSpec((None, D_INNER, D_STATE), lambda s, l: (s, 0, 0))],
        out_shape=[jax.ShapeDtypeStruct((b, L, D_MODEL), _F32),
                   jax.ShapeDtypeStruct((b, CONV_WIDTH - 1, CONV_DIM), _F32),
                   jax.ShapeDtypeStruct((b, D_INNER, D_STATE), _F32)],
        scratch_shapes=[pltpu.VMEM((N_GROUPS, D_STATE, GROUP_WIDTH), _F32),
                        pltpu.VMEM(((CONV_WIDTH - 1) * SUBLANES, CONV_DIM), _F32),
                        pltpu.VMEM((tile, CONV_DIM), _F32),
                        pltpu.VMEM((tile, D_INNER), _F32),
                        pltpu.VMEM((tile, D_INNER), _F32),
                        pltpu.VMEM((n_xcols, tile, LANES), _F32)],
        compiler_params=_params(2),
        name="ssd_prompt",
    )(*([x] * n_xcols), mod, *arrays)


def _sample_proj_kernel(*refs, nseq, seq, has_dt):
    if has_dt:
        x_ref, mod_ref, g_ref, w_ref, wdt_ref, dtb_ref, o_ref, dt_ref, hn_ref = refs
    else:
        x_ref, mod_ref, g_ref, w_ref, o_ref, hn_ref = refs

    @pl.when(pl.program_id(0) == 0)
    def _():
        hn = _prenorm(x_ref[...], g_ref[...], mod_ref[:, 0:1, :], mod_ref[:, 1:2, :])
        hn_ref[...] = hn.reshape(nseq * seq, D_MODEL).astype(_BF16)
        if has_dt:
            dt_ref[...] = _softplus(_dot(hn_ref[...], wdt_ref[...]) + dtb_ref[...])

    o_ref[...] = _dot(hn_ref[...], w_ref[...])


def _sample_proj(x, mod, w, i, weight, n_cols, has_dt):
    nseq, seq, _ = x.shape
    rows = nseq * seq
    j = i // N_MIXERS
    ops = _operands(w, i)
    arrays = [x, mod, ops["norm_g"][0], weight]
    specs = [pl.BlockSpec((nseq, seq, D_MODEL), lambda n: (0, 0, 0)),
             pl.BlockSpec((nseq, 3, D_MODEL), lambda n: (0, 0, 0)),
             ops["norm_g"][1],
             pl.BlockSpec((None, D_MODEL, PROJ_COLS), lambda n: (j, 0, n))]
    out_shape = [jax.ShapeDtypeStruct((rows, n_cols), _F32)]
    out_specs = [pl.BlockSpec((rows, PROJ_COLS), lambda n: (0, n))]
    if has_dt:
        extra, extra_specs = _pick(ops, ("w_dt", "dt_bias"))
        arrays += extra
        specs += extra_specs
        out_shape.append(jax.ShapeDtypeStruct((rows, LANES), _F32))
        out_specs.append(pl.BlockSpec((rows, LANES), lambda n: (0, 0)))
    return pl.pallas_call(
        functools.partial(_sample_proj_kernel, nseq=nseq, seq=seq, has_dt=has_dt),
        grid=(n_cols // PROJ_COLS,),
        in_specs=specs,
        out_specs=out_specs,
        out_shape=out_shape,
        scratch_shapes=[pltpu.VMEM((rows, D_MODEL), _BF16)],
        compiler_params=_params(1),
        name="sample_proj",
    )(*arrays)


_SSD_SAMPLE_OPS = ("conv_w", "conv_b", "a_log", "d_skip", "ssd_norm_g", "w_out", "final_g")


def _ssd_sample_kernel(x_ref, mod_ref, z_ref, xin_ref, bcin_ref, dt_ref,
                       cw_ref, cb_ref, alog_ref, dsk_ref, ng_ref, wout_ref, fin_ref,
                       cprev_ref, hprev_ref, exps_ref, exp64_ref, _alias_ref,
                       xo_ref, nconv_ref, nssm_ref,
                       xp_ref, act_ref, mfac_ref, e_ref, xw_ref, y_ref, *, nseq, seq, apply_final):
    rows = nseq * seq
    base = SUBLANES - (CONV_WIDTH - 1)
    xp_ref[:, base:SUBLANES, :] = cprev_ref[...]
    xp_ref[:, SUBLANES:SUBLANES + seq, 0:D_INNER] = xin_ref[...].reshape(nseq, seq, D_INNER)
    xp_ref[:, SUBLANES:SUBLANES + seq, D_INNER:CONV_DIM] = bcin_ref[...].reshape(nseq, seq, D_INNER)
    conv = cb_ref[...]
    for k in range(CONV_WIDTH):
        conv = conv + xp_ref[:, base + k:base + k + seq, :] * cw_ref[k:k + 1, :]
    act_ref[...] = _silu(conv)
    nconv_ref[...] = xp_ref[:, SUBLANES + seq - (CONV_WIDTH - 1):SUBLANES + seq, :]

    dt = dt_ref[...]
    a = dt * (-jnp.exp(alog_ref[...]))
    r_i = lax.broadcasted_iota(jnp.int32, (rows, rows), 0)
    c_i = lax.broadcasted_iota(jnp.int32, (rows, rows), 1)
    same_seq = (r_i // seq) == (c_i // seq)
    acum = _dot_exact((same_seq & (c_i <= r_i)).astype(_F32), a)
    atot = _dot_exact(same_seq.astype(_F32), a)
    w = dt * jnp.exp(atot - acum)

    shape3 = (nseq, seq, N_GROUPS * LANES)
    z3 = _select_lanes(acum, exps_ref[...], 3).reshape(shape3)
    dtz3 = _select_lanes(dt, exps_ref[...], 2).reshape(shape3)
    t3 = lax.broadcasted_iota(jnp.int32, shape3, 1)
    s3 = lax.broadcasted_iota(jnp.int32, shape3, 2) % seq
    diag = t3 == s3
    acum_s = jnp.sum(jnp.where(diag, z3, 0.0), axis=1, keepdims=True)
    dt_s = jnp.sum(jnp.where(diag, dtz3, 0.0), axis=1, keepdims=True)
    mfac_ref[...] = jnp.exp(jnp.where(s3 <= t3, z3 - acum_s, NEG_BIG)) * dt_s
    e_ref[...] = _select_lanes(jnp.exp(acum), exp64_ref[...], 2).reshape(nseq, seq, D_INNER)
    xw_ref[...] = act_ref[:, :, 0:D_INNER] * _select_lanes(w, exp64_ref[...], 2).reshape(nseq, seq, D_INNER)

    width = HEADS_PER_GROUP * seq
    bd_r = lax.broadcasted_iota(jnp.int32, (width, GROUP_WIDTH), 0)
    bd_c = lax.broadcasted_iota(jnp.int32, (width, GROUP_WIDTH), 1)
    blockdiag = (bd_r // seq) == (bd_c // HEAD_DIM)
    t_row = lax.broadcasted_iota(jnp.int32, (seq, GROUP_WIDTH), 0)
    zeros_b = jnp.zeros((seq, D_STATE), _F32)
    ones_b = jnp.ones((seq, D_STATE), _F32)

    def per_seq(s, carry):
        for g in range(N_GROUPS):
            b_g = act_ref[s, :, D_INNER + g * D_STATE:D_INNER + (g + 1) * D_STATE]
            c_lo = D_INNER + (N_GROUPS + g) * D_STATE
            c_bf = act_ref[s, :, c_lo:c_lo + D_STATE].astype(_BF16)
            sl = slice(g * GROUP_WIDTH, (g + 1) * GROUP_WIDTH)
            xs_g = act_ref[s, :, sl]
            cbx = _dot_nt(c_bf, jnp.concatenate([b_g] * HEADS_PER_GROUP, axis=0).astype(_BF16))
            mp = (cbx * mfac_ref[s, :, g * LANES:g * LANES + width]).astype(_BF16)
            rhs = jnp.where(blockdiag, jnp.concatenate([xs_g] * HEADS_PER_GROUP, axis=0), 0.0).astype(_BF16)
            h0 = hprev_ref[s, sl, :]
            e_g = e_ref[s, :, sl]
            y_ref[pl.ds(pl.multiple_of(s * seq, seq), seq), sl] = (
                _dot(mp, rhs) + _dot_nt(c_bf, h0.astype(_BF16)) * e_g + dsk_ref[:, sl] * xs_g)
            e_hi = e_g.astype(_BF16).astype(_F32)
            ez = jnp.where(t_row == seq - 1, e_hi, jnp.where(t_row == seq - 2, e_g - e_hi, 0.0))
            lhs_t = jnp.concatenate([xw_ref[s, :, sl], ez], axis=0).astype(_BF16)
            rhs_s = jnp.concatenate([jnp.concatenate([b_g, zeros_b], axis=1),
                                     jnp.concatenate([zeros_b, ones_b], axis=1)], axis=0).astype(_BF16)
            out = _dot_tn(lhs_t, rhs_s)
            nssm_ref[s, sl, :] = h0 * out[:, D_STATE:2 * D_STATE] + out[:, 0:D_STATE]
        return carry

    lax.fori_loop(0, nseq, per_seq, 0)

    gn = _group_rmsnorm_gate(y_ref, z_ref, ng_ref)
    out = _dot(gn, wout_ref[...]).reshape(nseq, seq, D_MODEL)
    xo_ref[...] = _residual(x_ref[...], out, mod_ref[:, 2:3, :], fin_ref, apply_final)


def _expansion_constants(seq):
    exps = np.zeros((LANES, N_GROUPS * LANES), np.float32)
    for h in range(N_HEADS):
        g, hh = divmod(h, HEADS_PER_GROUP)
        exps[h, g * LANES + hh * seq:g * LANES + (hh + 1) * seq] = 1.0
    exp64 = np.zeros((LANES, D_INNER), np.float32)
    for h in range(N_HEADS):
        exp64[h, h * HEAD_DIM:(h + 1) * HEAD_DIM] = 1.0
    return jnp.asarray(exps, _BF16), jnp.asarray(exp64, _BF16)


def _ssd_sample_layer(x, mod, w, i, state_conv, state_ssm, ssm_acc, apply_final):
    b, seq, _ = x.shape
    j = i // N_MIXERS
    nseq = SSD_SEQ_BLOCK
    rows = nseq * seq
    proj, dt = _sample_proj(x, mod, w, i, w["ssd_w_in"], D_INNER + CONV_DIM, True)
    kern = functools.partial(_ssd_sample_kernel, nseq=nseq, seq=seq, apply_final=apply_final)
    arrays, specs = _pick(_operands(w, i), _SSD_SAMPLE_OPS)
    exps, exp64 = _expansion_constants(seq)
    n_layers = state_ssm.shape[0]
    col_block = lambda n: pl.BlockSpec((rows, WEIGHT_COLS), lambda s: (s, n))
    in_specs = ([pl.BlockSpec((nseq, seq, D_MODEL), lambda s: (s, 0, 0)),
                 pl.BlockSpec((nseq, 3, D_MODEL), lambda s: (s, 0, 0)),
                 col_block(0), col_block(1), col_block(2),
                 pl.BlockSpec((rows, LANES), lambda s: (s, 0))]
                + specs
                + [pl.BlockSpec((None, nseq, CONV_WIDTH - 1, CONV_DIM), lambda s: (j, s, 0, 0)),
                   pl.BlockSpec((None, nseq, D_INNER, D_STATE), lambda s: (j, s, 0, 0)),
                   _wspec(exps.shape, (0, 0)), _wspec(exp64.shape, (0, 0))])
    args = [x, mod, proj, proj, proj, dt, *arrays, state_conv, state_ssm, exps, exp64]
    aliases = {}
    if ssm_acc is not None:
        in_specs.append(pl.BlockSpec(memory_space=pl.ANY))
        args.append(ssm_acc)
        aliases = {len(args) - 1: 2}
        kern_fn = kern
    else:
        n_in = len(args)
        kern_fn = lambda *refs: kern(*refs[:n_in], None, *refs[n_in:])
    return pl.pallas_call(
        kern_fn,
        grid=(b // nseq,),
        in_specs=in_specs,
        out_specs=[pl.BlockSpec((nseq, seq, D_MODEL), lambda s: (s, 0, 0)),
                   pl.BlockSpec((nseq, CONV_WIDTH - 1, CONV_DIM), lambda s: (s, 0, 0)),
                   pl.BlockSpec((None, nseq, D_INNER, D_STATE), lambda s: (j, s, 0, 0))],
        out_shape=[jax.ShapeDtypeStruct((b, seq, D_MODEL), _F32),
                   jax.ShapeDtypeStruct((b, CONV_WIDTH - 1, CONV_DIM), _F32),
                   jax.ShapeDtypeStruct((n_layers, b, D_INNER, D_STATE), _F32)],
        scratch_shapes=[pltpu.VMEM((nseq, 2 * SUBLANES, CONV_DIM), _F32),
                        pltpu.VMEM((nseq, seq, CONV_DIM), _F32),
                        pltpu.VMEM((nseq, seq, N_GROUPS * LANES), _F32),
                        pltpu.VMEM((nseq, seq, D_INNER), _F32),
                        pltpu.VMEM((nseq, seq, D_INNER), _F32),
                        pltpu.VMEM((rows, D_INNER), _F32)],
        input_output_aliases=aliases,
        compiler_params=_params(1),
        name="ssd_sample",
    )(*args)


_POOL_OPS = ("norm_g", "w_u", "w_zp", "w_group", "pool_scale", "w_out", "final_g")


def _pool_mix(load_rows, z, wg_ref, ps_ref, pos):
    parts = []
    for g, w in enumerate(POOL_WINDOWS):
        sl = slice(g * POOL_GROUP_DIM, (g + 1) * POOL_GROUP_DIM)
        u_g = load_rows(0, sl)
        win = u_g
        for k in range(1, w):
            win = win + load_rows(k, sl)
        cnt = jnp.minimum(float(w), pos + 1.0)
        pooled = (win / cnt - u_g).astype(_BF16)
        mixed = _dot(pooled, wg_ref[g])
        parts.append((mixed * ps_ref[:, sl] * _silu(z[:, sl])).astype(_BF16))
    return jnp.concatenate(parts, axis=1)


def _pool_prompt_kernel(x_ref, mod_ref, g_ref, wu_ref, wz_ref, wg_ref, ps_ref, wout_ref, fin_ref,
                        xo_ref, npool_ref, up_ref, *, tile, apply_final):
    l = pl.program_id(1)
    hist = POOL_HIST

    @pl.when(l == 0)
    def _():
        up_ref[0:hist, :] = jnp.zeros((hist, POOL_WIDTH), _F32)

    x = x_ref[...]
    hn = _prenorm(x, g_ref[...], mod_ref[0:1, :], mod_ref[1:2, :]).astype(_BF16)
    up_ref[hist:hist + tile, :] = _dot(hn, wu_ref[...])
    z = _dot(hn, wz_ref[...])
    pos = (l * tile + lax.broadcasted_iota(jnp.int32, (tile, POOL_GROUP_DIM), 0)).astype(_F32)
    mixed = _pool_mix(lambda k, sl: up_ref[hist - k:hist - k + tile, sl], z, wg_ref, ps_ref, pos)
    xo_ref[...] = _residual(x, _dot(mixed, wout_ref[...]), mod_ref[2:3, :], fin_ref, apply_final)
    tail = up_ref[hist + tile - POOL_STATE:hist + tile, :]
    up_ref[hist - POOL_STATE:hist, :] = tail
    npool_ref[...] = tail


def _pool_prompt_layer(x, mod, w, i, apply_final):
    b, L, _ = x.shape
    tile = PROMPT_TILE
    kern = functools.partial(_pool_prompt_kernel, tile=tile, apply_final=apply_final)
    arrays, specs = _pick(_operands(w, i), _POOL_OPS)
    return pl.pallas_call(
        kern,
        grid=(b, L // tile),
        in_specs=[pl.BlockSpec((None, tile, D_MODEL), lambda s, l: (s, l, 0)),
                  pl.BlockSpec((None, 3, D_MODEL), lambda s, l: (s, 0, 0))] + specs,
        out_specs=[pl.BlockSpec((None, tile, D_MODEL), lambda s, l: (s, l, 0)),
                   pl.BlockSpec((None, POOL_STATE, POOL_WIDTH), lambda s, l: (s, 0, 0))],
        out_shape=[jax.ShapeDtypeStruct((b, L, D_MODEL), _F32),
                   jax.ShapeDtypeStruct((b, POOL_STATE, POOL_WIDTH), _F32)],
        scratch_shapes=[pltpu.VMEM((POOL_HIST + tile, POOL_WIDTH), _F32)],
        compiler_params=_params(2),
        name="pool_prompt",
    )(x, mod, *arrays)


def _pool_sample_kernel(x_ref, mod_ref, g_ref, wu_ref, wz_ref, wg_ref, ps_ref, wout_ref, fin_ref, prev_ref,
                        xo_ref, npool_ref, up_ref, *, nseq, seq, start_pos, apply_final):
    rows = nseq * seq
    hist = POOL_HIST
    x3 = x_ref[...]
    hn = _prenorm(x3, g_ref[...], mod_ref[:, 0:1, :], mod_ref[:, 1:2, :]).reshape(rows, D_MODEL).astype(_BF16)
    up_ref[:, hist - POOL_STATE:hist, :] = prev_ref[...]
    up_ref[:, hist:hist + seq, :] = _dot(hn, wu_ref[...]).reshape(nseq, seq, POOL_WIDTH)
    z = _dot(hn, wz_ref[...])
    pos3 = start_pos + lax.broadcasted_iota(jnp.int32, (nseq, seq, POOL_GROUP_DIM), 1)
    pos = pos3.astype(_F32).reshape(rows, POOL_GROUP_DIM)
    mixed = _pool_mix(lambda k, sl: up_ref[:, hist - k:hist - k + seq, sl].reshape(rows, POOL_GROUP_DIM),
                      z, wg_ref, ps_ref, pos)
    out = _dot(mixed, wout_ref[...]).reshape(nseq, seq, D_MODEL)
    xo_ref[...] = _residual(x3, out, mod_ref[:, 2:3, :], fin_ref, apply_final)
    npool_ref[...] = up_ref[:, hist + seq - POOL_STATE:hist + seq, :]


def _pool_sample_layer(x, mod, w, i, state_pool, start_pos, apply_final):
    b, seq, _ = x.shape
    j = i // N_MIXERS
    nseq = POOL_SEQ_BLOCK
    kern = functools.partial(_pool_sample_kernel, nseq=nseq, seq=seq, start_pos=start_pos,
                             apply_final=apply_final)
    arrays, specs = _pick(_operands(w, i), _POOL_OPS)
    return pl.pallas_call(
        kern,
        grid=(b // nseq,),
        in_specs=[pl.BlockSpec((nseq, seq, D_MODEL), lambda s: (s, 0, 0)),
                  pl.BlockSpec((nseq, 3, D_MODEL), lambda s: (s, 0, 0))] + specs
                 + [pl.BlockSpec((None, nseq, POOL_STATE, POOL_WIDTH), lambda s: (j, s, 0, 0))],
        out_specs=[pl.BlockSpec((nseq, seq, D_MODEL), lambda s: (s, 0, 0)),
                   pl.BlockSpec((nseq, POOL_STATE, POOL_WIDTH), lambda s: (s, 0, 0))],
        out_shape=[jax.ShapeDtypeStruct((b, seq, D_MODEL), _F32),
                   jax.ShapeDtypeStruct((b, POOL_STATE, POOL_WIDTH), _F32)],
        scratch_shapes=[pltpu.VMEM((nseq, POOL_HIST + seq, POOL_WIDTH), _F32)],
        compiler_params=_params(1),
        name="pool_sample",
    )(x, mod, *arrays, state_pool)


def _prepare_weights(norm_g, ssd_w_in, ssd_conv_w, ssd_conv_b, ssd_dt_bias, ssd_a_log, ssd_d, ssd_norm_g,
                     ssd_w_out, pool_w_in, pool_w_group, pool_scale, pool_w_out, final_norm_g):
    n_ssd = ssd_w_in.shape[0]
    n_pool = pool_w_in.shape[0]
    pad = LANES - N_HEADS
    return {
        "norm_g": norm_g.reshape(DEPTH, 1, D_MODEL),
        "final_g": final_norm_g.reshape(1, D_MODEL),
        "ssd_w_in": ssd_w_in.astype(_BF16),
        "ssd_w_dt": jnp.pad(ssd_w_in[:, :, D_INNER + CONV_DIM:], ((0, 0), (0, 0), (0, pad))).astype(_BF16),
        "ssd_conv_w": ssd_conv_w,
        "ssd_conv_b": ssd_conv_b.reshape(n_ssd, 1, CONV_DIM),
        "ssd_dt_bias": jnp.pad(ssd_dt_bias, ((0, 0), (0, pad))).reshape(n_ssd, 1, LANES),
        "ssd_a_log": jnp.pad(ssd_a_log, ((0, 0), (0, pad))).reshape(n_ssd, 1, LANES),
        "ssd_d": jnp.repeat(ssd_d, HEAD_DIM, axis=1).reshape(n_ssd, 1, D_INNER),
        "ssd_norm_g": ssd_norm_g.reshape(n_ssd, 1, D_INNER),
        "ssd_w_out": ssd_w_out.astype(_BF16),
        "pool_w_in": pool_w_in.astype(_BF16),
        "pool_w_group": pool_w_group.astype(_BF16),
        "pool_scale": pool_scale.reshape(n_pool, 1, POOL_WIDTH),
        "pool_w_out": pool_w_out.astype(_BF16),
    }


def kernel(x_prompt, x_sample, state_ssm, state_conv, state_pool, c_prompt, c_sample, ada_w, ada_b, norm_g,
           ssd_w_in, ssd_conv_w, ssd_conv_b, ssd_dt_bias, ssd_a_log, ssd_d, ssd_norm_g, ssd_w_out, pool_w_in,
           pool_w_group, pool_scale, pool_w_out, final_norm_g):
    b_p = x_prompt.shape[0]
    b_s = x_sample.shape[0]
    n_ssd = state_ssm.shape[0]
    w = _prepare_weights(norm_g, ssd_w_in, ssd_conv_w, ssd_conv_b, ssd_dt_bias, ssd_a_log, ssd_d, ssd_norm_g,
                         ssd_w_out, pool_w_in, pool_w_group, pool_scale, pool_w_out, final_norm_g)
    mod = _modulation(jnp.concatenate([c_prompt, c_sample], axis=0), ada_w, ada_b)
    mod_p = mod[:, :b_p].reshape(DEPTH, b_p, 3, D_MODEL)
    mod_s = mod[:, b_p:].reshape(DEPTH, b_s, 3, D_MODEL)
    ssm_in = state_ssm.reshape(n_ssd, b_s, D_INNER, D_STATE)

    xp, xs = x_prompt, x_sample
    ssm_p, conv_p, pool_p, conv_s, pool_s = [], [], [], [], []
    ssm_s = None
    for i in range(DEPTH):
        last = i == DEPTH - 1
        if i % N_MIXERS == 0:
            xp, cv, st = _ssd_prompt_layer(xp, mod_p[i], w, i, last)
            conv_p.append(cv)
            ssm_p.append(st)
            xs, cv, ssm_s = _ssd_sample_layer(xs, mod_s[i], w, i, state_conv, ssm_in, ssm_s, last)
            conv_s.append(cv)
        else:
            xp, ps = _pool_prompt_layer(xp, mod_p[i], w, i, last)
            pool_p.append(ps)
            xs, ps = _pool_sample_layer(xs, mod_s[i], w, i, state_pool, PAST_LEN, last)
            pool_s.append(ps)

    head_shape = (N_HEADS, HEAD_DIM, D_STATE)
    return (xp, xs,
            jnp.stack(ssm_p).reshape(n_ssd, b_p, *head_shape),
            jnp.stack(conv_p), jnp.stack(pool_p),
            ssm_s.reshape(n_ssd, b_s, *head_shape),
            jnp.stack(conv_s), jnp.stack(pool_s))
```

```python
import functools

import numpy as np
import jax
import jax.numpy as jnp
from jax import lax
from jax.experimental import pallas as pl
from jax.experimental.pallas import tpu as pltpu

D_MODEL = 1024
DEPTH = 4
PAST_LEN = 16384
N_MIXERS = 2
D_INNER = 2048
HEAD_DIM = 64
N_HEADS = 32
N_GROUPS = 8
HEADS_PER_GROUP = 4
D_STATE = 128
CONV_WIDTH = 4
CONV_DIM = D_INNER + 2 * N_GROUPS * D_STATE
GROUP_WIDTH = HEADS_PER_GROUP * HEAD_DIM
POOL_WIDTH = 2048
POOL_WINDOWS = (2, 4, 8, 16)
POOL_GROUP_DIM = 512
POOL_STATE = 15
EPS = 1e-6

LANES = 128
SUBLANES = 8
CHUNK = 128
N_PHASES = CHUNK // SUBLANES
PROMPT_TILE = 256
SSD_SEQ_BLOCK = 8
POOL_SEQ_BLOCK = 16
POOL_HIST = 2 * SUBLANES
WEIGHT_COLS = 2048
PIECE_COLS = 256
PROJ_COLS = 1024
VMEM_LIMIT = 56 * 1024 * 1024
NEG_BIG = -1e30
LOG2E = 1.4426950408889634

_F32 = jnp.float32
_BF16 = jnp.bfloat16
_HI = lax.Precision.HIGHEST


def _silu(v):
    h = 0.5 * v
    return h + h * jnp.tanh(h)


def _softplus(v):
    return jnp.maximum(v, 0.0) + jnp.log(1.0 + jnp.exp(-jnp.abs(v)))


def _dot(a, b):
    return jnp.dot(a, b, preferred_element_type=_F32)


def _dot_exact(a, b):
    return jnp.dot(a, b, precision=_HI, preferred_element_type=_F32)


def _dot_nt(a, b):
    return lax.dot_general(a, b, (((1,), (1,)), ((), ())), preferred_element_type=_F32)


def _dot_tn(a, b):
    return lax.dot_general(a, b, (((0,), (0,)), ((), ())), preferred_element_type=_F32)


def _select_lanes(v, onehot_bf, terms):
    acc = None
    rest = v
    for _ in range(terms):
        piece = rest.astype(_BF16)
        part = _dot(piece, onehot_bf)
        acc = part if acc is None else acc + part
        rest = rest - piece.astype(_F32)
    return acc


def _prenorm(x, g, shift, scale):
    ms = jnp.mean(x * x, axis=-1, keepdims=True)
    return x * lax.rsqrt(ms + EPS) * g * (1.0 + scale) + shift


def _residual(x, out, gate, fin_ref, apply_final):
    y = x + (1.0 + gate) * out
    if apply_final:
        ms = jnp.mean(y * y, axis=-1, keepdims=True)
        y = y * lax.rsqrt(ms + EPS) * fin_ref[...]
    return y


def _group_rmsnorm_gate(y_ref, z_ref, ng_ref):
    parts = []
    for g in range(N_GROUPS):
        sl = slice(g * GROUP_WIDTH, (g + 1) * GROUP_WIDTH)
        gated = y_ref[:, sl] * _silu(z_ref[:, sl])
        ms = jnp.mean(gated * gated, axis=-1, keepdims=True)
        parts.append((gated * lax.rsqrt(ms + EPS) * ng_ref[:, sl]).astype(_BF16))
    return jnp.concatenate(parts, axis=1)


def _wspec(block_shape, index):
    return pl.BlockSpec(block_shape, lambda *_: index, pipeline_mode=pl.Buffered(1))


def _params(n_axes):
    return pltpu.CompilerParams(dimension_semantics=("arbitrary",) * n_axes, vmem_limit_bytes=VMEM_LIMIT)


def _operands(w, i):
    j = i // N_MIXERS
    wcols = (None, D_MODEL, WEIGHT_COLS)
    ops = {"norm_g": (w["norm_g"], _wspec((None, 1, D_MODEL), (i, 0, 0))),
           "final_g": (w["final_g"], _wspec((1, D_MODEL), (0, 0)))}
    if i % N_MIXERS == 0:
        ops.update(
            w_z=(w["ssd_w_in"], _wspec(wcols, (j, 0, 0))),
            w_x=(w["ssd_w_in"], _wspec(wcols, (j, 0, 1))),
            w_bc=(w["ssd_w_in"], _wspec(wcols, (j, 0, 2))),
            w_dt=(w["ssd_w_dt"], _wspec((None, D_MODEL, LANES), (j, 0, 0))),
            conv_w=(w["ssd_conv_w"], _wspec((None, CONV_WIDTH, CONV_DIM), (j, 0, 0))),
            conv_b=(w["ssd_conv_b"], _wspec((None, 1, CONV_DIM), (j, 0, 0))),
            dt_bias=(w["ssd_dt_bias"], _wspec((None, 1, LANES), (j, 0, 0))),
            a_log=(w["ssd_a_log"], _wspec((None, 1, LANES), (j, 0, 0))),
            d_skip=(w["ssd_d"], _wspec((None, 1, D_INNER), (j, 0, 0))),
            ssd_norm_g=(w["ssd_norm_g"], _wspec((None, 1, D_INNER), (j, 0, 0))),
            w_out=(w["ssd_w_out"], _wspec((None, D_INNER, D_MODEL), (j, 0, 0))),
        )
    else:
        ops.update(
            w_u=(w["pool_w_in"], _wspec(wcols, (j, 0, 0))),
            w_zp=(w["pool_w_in"], _wspec(wcols, (j, 0, 1))),
            w_group=(w["pool_w_group"],
                     _wspec((None, len(POOL_WINDOWS), POOL_GROUP_DIM, POOL_GROUP_DIM), (j, 0, 0, 0))),
            pool_scale=(w["pool_scale"], _wspec((None, 1, POOL_WIDTH), (j, 0, 0))),
            w_out=(w["pool_w_out"], _wspec((None, POOL_WIDTH, D_MODEL), (j, 0, 0))),
        )
    return ops


def _pick(ops, names):
    return [ops[n][0] for n in names], [ops[n][1] for n in names]


def _mod_kernel(c_ref, w_ref, b_ref, o_ref):
    sc = _silu(c_ref[...]).astype(_BF16)
    o_ref[...] = _dot(sc, w_ref[...].astype(_BF16)) + b_ref[...]


def _modulation(c_all, ada_w, ada_b):
    n = c_all.shape[0]
    tn = 1024
    return pl.pallas_call(
        _mod_kernel,
        grid=(DEPTH, 3 * D_MODEL // tn),
        in_specs=[
            pl.BlockSpec((n, D_MODEL), lambda i, j: (0, 0)),
            pl.BlockSpec((None, D_MODEL, tn), lambda i, j: (i, 0, j)),
            pl.BlockSpec((None, 1, tn), lambda i, j: (i, 0, j)),
        ],
        out_specs=pl.BlockSpec((None, n, tn), lambda i, j: (i, 0, j)),
        out_shape=jax.ShapeDtypeStruct((DEPTH, n, 3 * D_MODEL), _F32),
        compiler_params=_params(2),
        name="adaln_mod",
    )(c_all, ada_w, ada_b.reshape(DEPTH, 1, 3 * D_MODEL))


_SSD_PROMPT_OPS = ("norm_g", "w_z", "w_x", "w_bc", "w_dt", "conv_w", "conv_b", "dt_bias", "a_log", "d_skip",
                   "ssd_norm_g", "w_out", "final_g")


def _ssd_prompt_kernel(*refs, tile, n_tiles, apply_final):
    n_xcols = D_MODEL // LANES
    x_refs = refs[:n_xcols]
    (mod_ref, g_ref, wz_ref, wx_ref, wbc_ref, wdt_ref, cw_ref, cb_ref, dtb_ref, alog_ref, dsk_ref, ng_ref,
     wout_ref, fin_ref, xo_ref, nconv_ref, nssm_ref,
     ht_ref, carry_ref, act_ref, z_ref, y_ref, stage_ref) = refs[n_xcols:]
    l = pl.program_id(1)
    n_chunks = tile // CHUNK
    n_carry = CONV_WIDTH - 1

    @pl.when(l == 0)
    def _():
        ht_ref[...] = jnp.zeros_like(ht_ref)
        carry_ref[...] = jnp.zeros_like(carry_ref)

    def load_x():
        return jnp.concatenate(
            [jnp.concatenate([xr[pl.ds(c * CHUNK + r, SUBLANES, stride=N_PHASES), :] for xr in x_refs], axis=1)
             for c in range(n_chunks) for r in range(N_PHASES)], axis=0)

    hn = _prenorm(load_x(), g_ref[...], mod_ref[0:1, :], mod_ref[1:2, :]).astype(_BF16)
    dt = _softplus(_dot(hn, wdt_ref[...]) + dtb_ref[...])

    last = (N_PHASES - n_carry) * SUBLANES
    for k in range(CONV_DIM // PIECE_COLS):
        cols = slice(k * PIECE_COLS, (k + 1) * PIECE_COLS)
        half = D_INNER // PIECE_COLS
        w_ref, kw = (wx_ref, k) if k < half else (wbc_ref, k - half)
        xb = _dot(hn, w_ref[:, kw * PIECE_COLS:(kw + 1) * PIECE_COLS])
        prev = carry_ref[:, cols]
        for c in range(n_chunks):
            cur = xb[c * CHUNK:(c + 1) * CHUNK, :]
            tail = cur[last:CHUNK, :]
            wrapped = [jnp.concatenate([prev[j * SUBLANES + SUBLANES - 1:(j + 1) * SUBLANES, :],
                                        tail[j * SUBLANES:(j + 1) * SUBLANES - 1, :]], axis=0)
                       for j in range(n_carry)]
            ext = jnp.concatenate(wrapped + [cur], axis=0)
            conv = cb_ref[:, cols]
            for kk in range(CONV_WIDTH):
                conv = conv + ext[kk * SUBLANES:kk * SUBLANES + CHUNK, :] * cw_ref[kk:kk + 1, cols]
            act_ref[c * CHUNK:(c + 1) * CHUNK, cols] = _silu(conv)
            prev = tail
        carry_ref[:, cols] = prev
        for j in range(n_carry):
            nconv_ref[j:j + 1, cols] = prev[(j + 1) * SUBLANES - 1:(j + 1) * SUBLANES, :]

    row = lax.broadcasted_iota(jnp.int32, (CHUNK, CHUNK), 0)
    col = lax.broadcasted_iota(jnp.int32, (CHUNK, CHUNK), 1)
    token = lambda p: (p % SUBLANES) * N_PHASES + p // SUBLANES
    causal = token(col) <= token(row)
    tril = causal.astype(_F32)
    head_of_lane = lax.broadcasted_iota(jnp.int32, (CHUNK, GROUP_WIDTH), 1) // HEAD_DIM
    low_half = lax.broadcasted_iota(jnp.int32, (CHUNK, LANES), 1) < HEAD_DIM
    neg_a = -jnp.exp(alog_ref[...])

    def decay_stage(c):
        dt_c = dt[c * CHUNK:(c + 1) * CHUNK, :]
        acum = _dot_exact(tril, dt_c * neg_a)
        acum2 = acum * LOG2E
        w_c = dt_c * jnp.exp(acum[CHUNK - 1:CHUNK, :] - acum)
        srow_t = (jnp.log(dt_c) * LOG2E - acum2).T
        return acum2, srow_t, w_c.T

    def cb_stage(c, g):
        rows = slice(c * CHUNK, (c + 1) * CHUNK)
        b_g = act_ref[rows, D_INNER + g * D_STATE:D_INNER + (g + 1) * D_STATE]
        c_lo = D_INNER + (N_GROUPS + g) * D_STATE
        c_bf = act_ref[rows, c_lo:c_lo + D_STATE].astype(_BF16)
        bt_g = b_g.T
        return c_bf, bt_g, _dot(c_bf, bt_g.astype(_BF16))

    def operand_stage(c, g, decay, cbs):
        acum2, srow_t, w_t = decay
        _, bt_g, cb = cbs
        xs_g = act_ref[c * CHUNK:(c + 1) * CHUNK, g * GROUP_WIDTH:(g + 1) * GROUP_WIDTH]
        rhs = jnp.concatenate(
            [jnp.where(head_of_lane == hh, xs_g, 0.0).astype(_BF16) for hh in range(HEADS_PER_GROUP)],
            axis=0)
        m_parts, w_parts, e_cols = [], [], []
        for hh in range(HEADS_PER_GROUP):
            h = g * HEADS_PER_GROUP + hh
            acum_bc = jnp.broadcast_to(acum2[:, h:h + 1], (CHUNK, CHUNK))
            seg = acum_bc + srow_t[h:h + 1, :]
            m_parts.append((cb * jnp.exp2(jnp.where(causal, seg, NEG_BIG))).astype(_BF16))
            w_parts.append((bt_g * w_t[h:h + 1, :]).astype(_BF16))
            e_cols.append(jnp.exp2(acum_bc))
        lhs = jnp.concatenate(
            [jnp.concatenate(m_parts, axis=1), jnp.concatenate(w_parts, axis=1)], axis=0)
        e_g = jnp.concatenate(
            [jnp.where(low_half, e_cols[0], e_cols[1]), jnp.where(low_half, e_cols[2], e_cols[3])],
            axis=1)
        return lhs, rhs, e_g

    def output_stage(c, g, cbs, e_g, out):
        rows = slice(c * CHUNK, (c + 1) * CHUNK)
        sl = slice(g * GROUP_WIDTH, (g + 1) * GROUP_WIDTH)
        ht_g = ht_ref[g]
        y_inter = _dot(cbs[0], ht_g.astype(_BF16))
        y_ref[rows, sl] = out[0:CHUNK, :] + y_inter * e_g + dsk_ref[:, sl] * act_ref[rows, sl]
        ht_ref[g] = ht_g * e_g[CHUNK - 1:CHUNK, :] + out[CHUNK:2 * CHUNK, :]

    def gate_proj_piece(p):
        if p < D_INNER // PIECE_COLS:
            zc = slice(p * PIECE_COLS, (p + 1) * PIECE_COLS)
            z_ref[:, zc] = _dot(hn, wz_ref[:, zc])

    groups = range(N_GROUPS)
    decays = [decay_stage(c) for c in range(n_chunks)]
    cbs = [[cb_stage(c, g) for g in groups] for c in range(n_chunks)]
    operands = [operand_stage(0, g, decays[0], cbs[0][g]) for g in groups]
    for c in range(n_chunks):
        outs, nxt = [], []
        for g in groups:
            lhs, rhs, _ = operands[g]
            outs.append(_dot(lhs, rhs))
            if c + 1 < n_chunks:
                nxt.append(operand_stage(c + 1, g, decays[c + 1], cbs[c + 1][g]))
        for g in groups:
            output_stage(c, g, cbs[c][g], operands[g][2], outs[g])
            gate_proj_piece(c * N_GROUPS + g)
        operands = nxt

    gn = _group_rmsnorm_gate(y_ref, z_ref, ng_ref)
    xo = _residual(load_x(), _dot(gn, wout_ref[...]), mod_ref[2:3, :], fin_ref, apply_final)
    for c in range(n_chunks):
        for r in range(N_PHASES):
            p0 = c * CHUNK + r * SUBLANES
            for j in range(n_xcols):
                stage_ref[j, pl.ds(c * CHUNK + r, SUBLANES, stride=N_PHASES), :] = (
                    xo[p0:p0 + SUBLANES, j * LANES:(j + 1) * LANES])
    for j in range(n_xcols):
        xo_ref[:, j * LANES:(j + 1) * LANES] = stage_ref[j]

    @pl.when(l == n_tiles - 1)
    def _():
        for g in range(N_GROUPS):
            nssm_ref[g * GROUP_WIDTH:(g + 1) * GROUP_WIDTH, :] = ht_ref[g].T


def _ssd_prompt_layer(x, mod, w, i, apply_final):
    b, L, _ = x.shape
    tile = PROMPT_TILE
    n_tiles = L // tile
    assert tile // CHUNK * N_GROUPS >= D_INNER // PIECE_COLS
    kern = functools.partial(_ssd_prompt_kernel, tile=tile, n_tiles=n_tiles, apply_final=apply_final)
    arrays, specs = _pick(_operands(w, i), _SSD_PROMPT_OPS)
    n_xcols = D_MODEL // LANES
    x_specs = [pl.BlockSpec((None, tile, LANES), lambda s, l, j=j: (s, l, j)) for j in range(n_xcols)]
    return pl.pallas_call(
        kern,
        grid=(b, n_tiles),
        in_specs=x_specs + [pl.BlockSpec((None, 3, D_MODEL), lambda s, l: (s, 0, 0))] + specs,
        out_specs=[pl.BlockSpec((None, tile, D_MODEL), lambda s, l: (s, l, 0)),
                   pl.BlockSpec((None, CONV_WIDTH - 1, CONV_DIM), lambda s, l: (s, 0, 0)),
                   pl.BlockSpec((None, D_INNER, D_STATE), lambda s, l: (s, 0, 0))],
        out_shape=[jax.ShapeDtypeStruct((b, L, D_MODEL), _F32),
                   jax.ShapeDtypeStruct((b, CONV_WIDTH - 1, CONV_DIM), _F32),
                   jax.ShapeDtypeStruct((b, D_INNER, D_STATE), _F32)],
        scratch_shapes=[pltpu.VMEM((N_GROUPS, D_STATE, GROUP_WIDTH), _F32),
                        pltpu.VMEM(((CONV_WIDTH - 1) * SUBLANES, CONV_DIM), _F32),
                        pltpu.VMEM((tile, CONV_DIM), _F32),
                        pltpu.VMEM((tile, D_INNER), _F32),
                        pltpu.VMEM((tile, D_INNER), _F32),
                        pltpu.VMEM((n_xcols, tile, LANES), _F32)],
        compiler_params=_params(2),
        name="ssd_prompt",
    )(*([x] * n_xcols), mod, *arrays)


def _sample_proj_kernel(*refs, nseq, seq, has_dt):
    if has_dt:
        x_ref, mod_ref, g_ref, w_ref, wdt_ref, dtb_ref, o_ref, dt_ref, hn_ref = refs
    else:
        x_ref, mod_ref, g_ref, w_ref, o_ref, hn_ref = refs

    @pl.when(pl.program_id(0) == 0)
    def _():
        hn = _prenorm(x_ref[...], g_ref[...], mod_ref[:, 0:1, :], mod_ref[:, 1:2, :])
        hn_ref[...] = hn.reshape(nseq * seq, D_MODEL).astype(_BF16)
        if has_dt:
            dt_ref[...] = _softplus(_dot(hn_ref[...], wdt_ref[...]) + dtb_ref[...])

    o_ref[...] = _dot(hn_ref[...], w_ref[...])


def _sample_proj(x, mod, w, i, weight, n_cols, has_dt):
    nseq, seq, _ = x.shape
    rows = nseq * seq
    j = i // N_MIXERS
    ops = _operands(w, i)
    arrays = [x, mod, ops["norm_g"][0], weight]
    specs = [pl.BlockSpec((nseq, seq, D_MODEL), lambda n: (0, 0, 0)),
             pl.BlockSpec((nseq, 3, D_MODEL), lambda n: (0, 0, 0)),
             ops["norm_g"][1],
             pl.BlockSpec((None, D_MODEL, PROJ_COLS), lambda n: (j, 0, n))]
    out_shape = [jax.ShapeDtypeStruct((rows, n_cols), _F32)]
    out_specs = [pl.BlockSpec((rows, PROJ_COLS), lambda n: (0, n))]
    if has_dt:
        extra, extra_specs = _pick(ops, ("w_dt", "dt_bias"))
        arrays += extra
        specs += extra_specs
        out_shape.append(jax.ShapeDtypeStruct((rows, LANES), _F32))
        out_specs.append(pl.BlockSpec((rows, LANES), lambda n: (0, 0)))
    return pl.pallas_call(
        functools.partial(_sample_proj_kernel, nseq=nseq, seq=seq, has_dt=has_dt),
        grid=(n_cols // PROJ_COLS,),
        in_specs=specs,
        out_specs=out_specs,
        out_shape=out_shape,
        scratch_shapes=[pltpu.VMEM((rows, D_MODEL), _BF16)],
        compiler_params=_params(1),
        name="sample_proj",
    )(*arrays)


_SSD_SAMPLE_OPS = ("conv_w", "conv_b", "a_log", "d_skip", "ssd_norm_g", "w_out", "final_g")


def _ssd_sample_kernel(x_ref, mod_ref, z_ref, xin_ref, bcin_ref, dt_ref,
                       cw_ref, cb_ref, alog_ref, dsk_ref, ng_ref, wout_ref, fin_ref,
                       cprev_ref, hprev_ref, exps_ref, exp64_ref, _alias_ref,
                       xo_ref, nconv_ref, nssm_ref,
                       xp_ref, act_ref, mfac_ref, e_ref, xw_ref, y_ref, *, nseq, seq, apply_final):
    rows = nseq * seq
    base = SUBLANES - (CONV_WIDTH - 1)
    xp_ref[:, base:SUBLANES, :] = cprev_ref[...]
    xp_ref[:, SUBLANES:SUBLANES + seq, 0:D_INNER] = xin_ref[...].reshape(nseq, seq, D_INNER)
    xp_ref[:, SUBLANES:SUBLANES + seq, D_INNER:CONV_DIM] = bcin_ref[...].reshape(nseq, seq, D_INNER)
    conv = cb_ref[...]
    for k in range(CONV_WIDTH):
        conv = conv + xp_ref[:, base + k:base + k + seq, :] * cw_ref[k:k + 1, :]
    act_ref[...] = _silu(conv)
    nconv_ref[...] = xp_ref[:, SUBLANES + seq - (CONV_WIDTH - 1):SUBLANES + seq, :]

    dt = dt_ref[...]
    a = dt * (-jnp.exp(alog_ref[...]))
    r_i = lax.broadcasted_iota(jnp.int32, (rows, rows), 0)
    c_i = lax.broadcasted_iota(jnp.int32, (rows, rows), 1)
    same_seq = (r_i // seq) == (c_i // seq)
    acum = _dot_exact((same_seq & (c_i <= r_i)).astype(_F32), a)
    atot = _dot_exact(same_seq.astype(_F32), a)
    w = dt * jnp.exp(atot - acum)

    shape3 = (nseq, seq, N_GROUPS * LANES)
    z3 = _select_lanes(acum, exps_ref[...], 3).reshape(shape3)
    dtz3 = _select_lanes(dt, exps_ref[...], 2).reshape(shape3)
    t3 = lax.broadcasted_iota(jnp.int32, shape3, 1)
    s3 = lax.broadcasted_iota(jnp.int32, shape3, 2) % seq
    diag = t3 == s3
    acum_s = jnp.sum(jnp.where(diag, z3, 0.0), axis=1, keepdims=True)
    dt_s = jnp.sum(jnp.where(diag, dtz3, 0.0), axis=1, keepdims=True)
    mfac_ref[...] = jnp.exp(jnp.where(s3 <= t3, z3 - acum_s, NEG_BIG)) * dt_s
    e_ref[...] = _select_lanes(jnp.exp(acum), exp64_ref[...], 2).reshape(nseq, seq, D_INNER)
    xw_ref[...] = act_ref[:, :, 0:D_INNER] * _select_lanes(w, exp64_ref[...], 2).reshape(nseq, seq, D_INNER)

    width = HEADS_PER_GROUP * seq
    bd_r = lax.broadcasted_iota(jnp.int32, (width, GROUP_WIDTH), 0)
    bd_c = lax.broadcasted_iota(jnp.int32, (width, GROUP_WIDTH), 1)
    blockdiag = (bd_r // seq) == (bd_c // HEAD_DIM)
    t_row = lax.broadcasted_iota(jnp.int32, (seq, GROUP_WIDTH), 0)
    zeros_b = jnp.zeros((seq, D_STATE), _F32)
    ones_b = jnp.ones((seq, D_STATE), _F32)

    def per_seq(s, carry):
        for g in range(N_GROUPS):
            b_g = act_ref[s, :, D_INNER + g * D_STATE:D_INNER + (g + 1) * D_STATE]
            c_lo = D_INNER + (N_GROUPS + g) * D_STATE
            c_bf = act_ref[s, :, c_lo:c_lo + D_STATE].astype(_BF16)
            sl = slice(g * GROUP_WIDTH, (g + 1) * GROUP_WIDTH)
            xs_g = act_ref[s, :, sl]
            cbx = _dot_nt(c_bf, jnp.concatenate([b_g] * HEADS_PER_GROUP, axis=0).astype(_BF16))
            mp = (cbx * mfac_ref[s, :, g * LANES:g * LANES + width]).astype(_BF16)
            rhs = jnp.where(blockdiag, jnp.concatenate([xs_g] * HEADS_PER_GROUP, axis=0), 0.0).astype(_BF16)
            h0 = hprev_ref[s, sl, :]
            e_g = e_ref[s, :, sl]
            y_ref[pl.ds(pl.multiple_of(s * seq, seq), seq), sl] = (
                _dot(mp, rhs) + _dot_nt(c_bf, h0.astype(_BF16)) * e_g + dsk_ref[:, sl] * xs_g)
            e_hi = e_g.astype(_BF16).astype(_F32)
            ez = jnp.where(t_row == seq - 1, e_hi, jnp.where(t_row == seq - 2, e_g - e_hi, 0.0))
            lhs_t = jnp.concatenate([xw_ref[s, :, sl], ez], axis=0).astype(_BF16)
            rhs_s = jnp.concatenate([jnp.concatenate([b_g, zeros_b], axis=1),
                                     jnp.concatenate([zeros_b, ones_b], axis=1)], axis=0).astype(_BF16)
            out = _dot_tn(lhs_t, rhs_s)
            nssm_ref[s, sl, :] = h0 * out[:, D_STATE:2 * D_STATE] + out[:, 0:D_STATE]
        return carry

    lax.fori_loop(0, nseq, per_seq, 0)

    gn = _group_rmsnorm_gate(y_ref, z_ref, ng_ref)
    out = _dot(gn, wout_ref[...]).reshape(nseq, seq, D_MODEL)
    xo_ref[...] = _residual(x_ref[...], out, mod_ref[:, 2:3, :], fin_ref, apply_final)


def _expansion_constants(seq):
    exps = np.zeros((LANES, N_GROUPS * LANES), np.float32)
    for h in range(N_HEADS):
        g, hh = divmod(h, HEADS_PER_GROUP)
        exps[h, g * LANES + hh * seq:g * LANES + (hh + 1) * seq] = 1.0
    exp64 = np.zeros((LANES, D_INNER), np.float32)
    for h in range(N_HEADS):
        exp64[h, h * HEAD_DIM:(h + 1) * HEAD_DIM] = 1.0
    return jnp.asarray(exps, _BF16), jnp.asarray(exp64, _BF16)


def _ssd_sample_layer(x, mod, w, i, state_conv, state_ssm, ssm_acc, apply_final):
    b, seq, _ = x.shape
    j = i // N_MIXERS
    nseq = SSD_SEQ_BLOCK
    rows = nseq * seq
    proj, dt = _sample_proj(x, mod, w, i, w["ssd_w_in"], D_INNER + CONV_DIM, True)
    kern = functools.partial(_ssd_sample_kernel, nseq=nseq, seq=seq, apply_final=apply_final)
    arrays, specs = _pick(_operands(w, i), _SSD_SAMPLE_OPS)
    exps, exp64 = _expansion_constants(seq)
    n_layers = state_ssm.shape[0]
    col_block = lambda n: pl.BlockSpec((rows, WEIGHT_COLS), lambda s: (s, n))
    in_specs = ([pl.BlockSpec((nseq, seq, D_MODEL), lambda s: (s, 0, 0)),
                 pl.BlockSpec((nseq, 3, D_MODEL), lambda s: (s, 0, 0)),
                 col_block(0), col_block(1), col_block(2),
                 pl.BlockSpec((rows, LANES), lambda s: (s, 0))]
                + specs
                + [pl.BlockSpec((None, nseq, CONV_WIDTH - 1, CONV_DIM), lambda s: (j, s, 0, 0)),
                   pl.BlockSpec((None, nseq, D_INNER, D_STATE), lambda s: (j, s, 0, 0)),
                   _wspec(exps.shape, (0, 0)), _wspec(exp64.shape, (0, 0))])
    args = [x, mod, proj, proj, proj, dt, *arrays, state_conv, state_ssm, exps, exp64]
    aliases = {}
    if ssm_acc is not None:
        in_specs.append(pl.BlockSpec(memory_space=pl.ANY))
        args.append(ssm_acc)
        aliases = {len(args) - 1: 2}
        kern_fn = kern
    else:
        n_in = len(args)
        kern_fn = lambda *refs: kern(*refs[:n_in], None, *refs[n_in:])
    return pl.pallas_call(
        kern_fn,
        grid=(b // nseq,),
        in_specs=in_specs,
        out_specs=[pl.BlockSpec((nseq, seq, D_MODEL), lambda s: (s, 0, 0)),
                   pl.BlockSpec((nseq, CONV_WIDTH - 1, CONV_DIM), lambda s: (s, 0, 0)),
                   pl.BlockSpec((None, nseq, D_INNER, D_STATE), lambda s: (j, s, 0, 0))],
        out_shape=[jax.ShapeDtypeStruct((b, seq, D_MODEL), _F32),
                   jax.ShapeDtypeStruct((b, CONV_WIDTH - 1, CONV_DIM), _F32),
                   jax.ShapeDtypeStruct((n_layers, b, D_INNER, D_STATE), _F32)],
        scratch_shapes=[pltpu.VMEM((nseq, 2 * SUBLANES, CONV_DIM), _F32),
                        pltpu.VMEM((nseq, seq, CONV_DIM), _F32),
                        pltpu.VMEM((nseq, seq, N_GROUPS * LANES), _F32),
                        pltpu.VMEM((nseq, seq, D_INNER), _F32),
                        pltpu.VMEM((nseq, seq, D_INNER), _F32),
                        pltpu.VMEM((rows, D_INNER), _F32)],
        input_output_aliases=aliases,
        compiler_params=_params(1),
        name="ssd_sample",
    )(*args)


_POOL_OPS = ("norm_g", "w_u", "w_zp", "w_group", "pool_scale", "w_out", "final_g")


def _pool_mix(load_rows, z, wg_ref, ps_ref, pos):
    parts = []
    for g, w in enumerate(POOL_WINDOWS):
        sl = slice(g * POOL_GROUP_DIM, (g + 1) * POOL_GROUP_DIM)
        u_g = load_rows(0, sl)
        win = u_g
        for k in range(1, w):
            win = win + load_rows(k, sl)
        cnt = jnp.minimum(float(w), pos + 1.0)
        pooled = (win / cnt - u_g).astype(_BF16)
        mixed = _dot(pooled, wg_ref[g])
        parts.append((mixed * ps_ref[:, sl] * _silu(z[:, sl])).astype(_BF16))
    return jnp.concatenate(parts, axis=1)


def _pool_prompt_kernel(x_ref, mod_ref, g_ref, wu_ref, wz_ref, wg_ref, ps_ref, wout_ref, fin_ref,
                        xo_ref, npool_ref, up_ref, *, tile, apply_final):
    l = pl.program_id(1)
    hist = POOL_HIST

    @pl.when(l == 0)
    def _():
        up_ref[0:hist, :] = jnp.zeros((hist, POOL_WIDTH), _F32)

    x = x_ref[...]
    hn = _prenorm(x, g_ref[...], mod_ref[0:1, :], mod_ref[1:2, :]).astype(_BF16)
    up_ref[hist:hist + tile, :] = _dot(hn, wu_ref[...])
    z = _dot(hn, wz_ref[...])
    pos = (l * tile + lax.broadcasted_iota(jnp.int32, (tile, POOL_GROUP_DIM), 0)).astype(_F32)
    mixed = _pool_mix(lambda k, sl: up_ref[hist - k:hist - k + tile, sl], z, wg_ref, ps_ref, pos)
    xo_ref[...] = _residual(x, _dot(mixed, wout_ref[...]), mod_ref[2:3, :], fin_ref, apply_final)
    tail = up_ref[hist + tile - POOL_STATE:hist + tile, :]
    up_ref[hist - POOL_STATE:hist, :] = tail
    npool_ref[...] = tail


def _pool_prompt_layer(x, mod, w, i, apply_final):
    b, L, _ = x.shape
    tile = PROMPT_TILE
    kern = functools.partial(_pool_prompt_kernel, tile=tile, apply_final=apply_final)
    arrays, specs = _pick(_operands(w, i), _POOL_OPS)
    return pl.pallas_call(
        kern,
        grid=(b, L // tile),
        in_specs=[pl.BlockSpec((None, tile, D_MODEL), lambda s, l: (s, l, 0)),
                  pl.BlockSpec((None, 3, D_MODEL), lambda s, l: (s, 0, 0))] + specs,
        out_specs=[pl.BlockSpec((None, tile, D_MODEL), lambda s, l: (s, l, 0)),
                   pl.BlockSpec((None, POOL_STATE, POOL_WIDTH), lambda s, l: (s, 0, 0))],
        out_shape=[jax.ShapeDtypeStruct((b, L, D_MODEL), _F32),
                   jax.ShapeDtypeStruct((b, POOL_STATE, POOL_WIDTH), _F32)],
        scratch_shapes=[pltpu.VMEM((POOL_HIST + tile, POOL_WIDTH), _F32)],
        compiler_params=_params(2),
        name="pool_prompt",
    )(x, mod, *arrays)


def _pool_sample_kernel(x_ref, mod_ref, g_ref, wu_ref, wz_ref, wg_ref, ps_ref, wout_ref, fin_ref, prev_ref,
                        xo_ref, npool_ref, up_ref, *, nseq, seq, start_pos, apply_final):
    rows = nseq * seq
    hist = POOL_HIST
    x3 = x_ref[...]
    hn = _prenorm(x3, g_ref[...], mod_ref[:, 0:1, :], mod_ref[:, 1:2, :]).reshape(rows, D_MODEL).astype(_BF16)
    up_ref[:, hist - POOL_STATE:hist, :] = prev_ref[...]
    up_ref[:, hist:hist + seq, :] = _dot(hn, wu_ref[...]).reshape(nseq, seq, POOL_WIDTH)
    z = _dot(hn, wz_ref[...])
    pos3 = start_pos + lax.broadcasted_iota(jnp.int32, (nseq, seq, POOL_GROUP_DIM), 1)
    pos = pos3.astype(_F32).reshape(rows, POOL_GROUP_DIM)
    mixed = _pool_mix(lambda k, sl: up_ref[:, hist - k:hist - k + seq, sl].reshape(rows, POOL_GROUP_DIM),
                      z, wg_ref, ps_ref, pos)
    out = _dot(mixed, wout_ref[...]).reshape(nseq, seq, D_MODEL)
    xo_ref[...] = _residual(x3, out, mod_ref[:, 2:3, :], fin_ref, apply_final)
    npool_ref[...] = up_ref[:, hist + seq - POOL_STATE:hist + seq, :]


def _pool_sample_layer(x, mod, w, i, state_pool, start_pos, apply_final):
    b, seq, _ = x.shape
    j = i // N_MIXERS
    nseq = POOL_SEQ_BLOCK
    kern = functools.partial(_pool_sample_kernel, nseq=nseq, seq=seq, start_pos=start_pos,
                             apply_final=apply_final)
    arrays, specs = _pick(_operands(w, i), _POOL_OPS)
    return pl.pallas_call(
        kern,
        grid=(b // nseq,),
        in_specs=[pl.BlockSpec((nseq, seq, D_MODEL), lambda s: (s, 0, 0)),
                  pl.BlockSpec((nseq, 3, D_MODEL), lambda s: (s, 0, 0))] + specs
                 + [pl.BlockSpec((None, nseq, POOL_STATE, POOL_WIDTH), lambda s: (j, s, 0, 0))],
        out_specs=[pl.BlockSpec((nseq, seq, D_MODEL), lambda s: (s, 0, 0)),
                   pl.BlockSpec((nseq, POOL_STATE, POOL_WIDTH), lambda s: (s, 0, 0))],
        out_shape=[jax.ShapeDtypeStruct((b, seq, D_MODEL), _F32),
                   jax.ShapeDtypeStruct((b, POOL_STATE, POOL_WIDTH), _F32)],
        scratch_shapes=[pltpu.VMEM((nseq, POOL_HIST + seq, POOL_WIDTH), _F32)],
        compiler_params=_params(1),
        name="pool_sample",
    )(x, mod, *arrays, state_pool)


def _tm_proj_kernel(*refs, has_dt):
    if has_dt:
        x_ref, mod_ref, g_ref, w_ref, wdt_ref, dtb_ref, o_ref, dt_ref, hn_ref = refs
    else:
        x_ref, mod_ref, g_ref, w_ref, o_ref, hn_ref = refs

    @pl.when(pl.program_id(0) == 0)
    def _():
        hn = _prenorm(x_ref[...], g_ref[...], mod_ref[0:1], mod_ref[1:2])
        hn_ref[...] = hn.reshape(hn_ref.shape).astype(_BF16)
        if has_dt:
            dt_ref[...] = _softplus(_dot(hn_ref[...], wdt_ref[...]) + dtb_ref[...])

    o_ref[...] = _dot(hn_ref[...], w_ref[...])


def _tm_proj(x, mod, w, i, weight, n_cols, has_dt):
    seq, b, _ = x.shape
    rows = seq * b
    j = i // N_MIXERS
    ops = _operands(w, i)
    arrays = [x, mod, ops["norm_g"][0], weight]
    specs = [pl.BlockSpec((seq, b, D_MODEL), lambda n: (0, 0, 0)),
             pl.BlockSpec((3, b, D_MODEL), lambda n: (0, 0, 0)),
             ops["norm_g"][1],
             pl.BlockSpec((None, D_MODEL, PROJ_COLS), lambda n: (j, 0, n))]
    out_shape = [jax.ShapeDtypeStruct((rows, n_cols), _F32)]
    out_specs = [pl.BlockSpec((rows, PROJ_COLS), lambda n: (0, n))]
    if has_dt:
        extra, extra_specs = _pick(ops, ("w_dt", "dt_bias"))
        arrays += extra
        specs += extra_specs
        out_shape.append(jax.ShapeDtypeStruct((rows, LANES), _F32))
        out_specs.append(pl.BlockSpec((rows, LANES), lambda n: (0, 0)))
    return pl.pallas_call(
        functools.partial(_tm_proj_kernel, has_dt=has_dt),
        grid=(n_cols // PROJ_COLS,),
        in_specs=specs,
        out_specs=out_specs,
        out_shape=out_shape,
        scratch_shapes=[pltpu.VMEM((rows, D_MODEL), _BF16)],
        compiler_params=_params(1),
        name="sample_proj",
    )(*arrays)


def _tm_ssd_kernel(x_ref, mod_ref, z_ref, xin_ref, bcin_ref, dt_ref,
                   cw_ref, cb_ref, alog_ref, dsk_ref, ng_ref, wout_ref, fin_ref,
                   cprev_ref, hprev_ref, exps_ref, exp64_ref, _conv_alias_ref, _ssm_alias_ref,
                   xo_ref, nconv_ref, nssm_ref,
                   xp_ref, act_ref, mfac_ref, e_ref, xw_ref, y_ref, *, nseq, seq, apply_final):
    rows = nseq * seq
    n_carry = CONV_WIDTH - 1
    xp_ref[0:n_carry] = cprev_ref[...]
    xp_ref[n_carry:n_carry + seq, :, 0:D_INNER] = xin_ref[...]
    xp_ref[n_carry:n_carry + seq, :, D_INNER:CONV_DIM] = bcin_ref[...]
    conv = cb_ref[...]
    for k in range(CONV_WIDTH):
        conv = conv + xp_ref[k:k + seq] * cw_ref[k:k + 1, :]
    act = _silu(conv).reshape(rows, CONV_DIM)
    for jb in range(CONV_DIM // LANES):
        act_ref[jb] = act[:, jb * LANES:(jb + 1) * LANES]
    nconv_ref[...] = xp_ref[seq:seq + n_carry]

    dt3 = dt_ref[...]
    a3 = dt3 * (-jnp.exp(alog_ref[...]))
    slabs = [a3[0]]
    for t in range(1, seq):
        slabs.append(slabs[-1] + a3[t])
    acum3 = jnp.stack(slabs, axis=0)
    w3 = dt3 * jnp.exp(acum3[seq - 1:seq] - acum3)
    acum = acum3.reshape(rows, LANES)

    shape3 = (seq, nseq, N_GROUPS * LANES)
    z3 = _select_lanes(acum, exps_ref[...], 3).reshape(shape3)
    dtz3 = _select_lanes(dt3.reshape(rows, LANES), exps_ref[...], 2).reshape(shape3)
    t3 = lax.broadcasted_iota(jnp.int32, shape3, 0)
    s3 = lax.broadcasted_iota(jnp.int32, shape3, 2) % seq
    diag = t3 == s3
    acum_s = jnp.sum(jnp.where(diag, z3, 0.0), axis=0, keepdims=True)
    dt_s = jnp.sum(jnp.where(diag, dtz3, 0.0), axis=0, keepdims=True)
    mfac = (jnp.exp(jnp.where(s3 <= t3, z3 - acum_s, NEG_BIG)) * dt_s).reshape(rows, N_GROUPS * LANES)
    for g in range(N_GROUPS):
        mfac_ref[g] = mfac[:, g * LANES:(g + 1) * LANES]
    e = _select_lanes(jnp.exp(acum), exp64_ref[...], 2)
    wx = _select_lanes(w3.reshape(rows, LANES), exp64_ref[...], 2)
    for jb in range(D_INNER // LANES):
        lanes = slice(jb * LANES, (jb + 1) * LANES)
        e_ref[jb] = e[:, lanes]
        xw_ref[jb] = act[:, lanes] * wx[:, lanes]

    width = HEADS_PER_GROUP * seq
    bd_r = lax.broadcasted_iota(jnp.int32, (width, GROUP_WIDTH), 0)
    bd_c = lax.broadcasted_iota(jnp.int32, (width, GROUP_WIDTH), 1)
    blockdiag = (bd_r // seq) == (bd_c // HEAD_DIM)
    t_row = lax.broadcasted_iota(jnp.int32, (seq, GROUP_WIDTH), 0)
    zeros_b = jnp.zeros((seq, D_STATE), _F32)
    ones_b = jnp.ones((seq, D_STATE), _F32)
    b_blk = D_INNER // LANES
    c_blk = b_blk + N_GROUPS

    def per_seq(s, carry):
        tokens = pl.ds(s, seq, stride=nseq)

        def pair(ref, g):
            return jnp.concatenate([ref[2 * g, tokens, :], ref[2 * g + 1, tokens, :]], axis=1)

        for g in range(N_GROUPS):
            b_g = act_ref[b_blk + g, tokens, :]
            c_bf = act_ref[c_blk + g, tokens, :].astype(_BF16)
            sl = slice(g * GROUP_WIDTH, (g + 1) * GROUP_WIDTH)
            xs_g = pair(act_ref, g)
            cbx = _dot_nt(c_bf, jnp.concatenate([b_g] * HEADS_PER_GROUP, axis=0).astype(_BF16))
            mp = (cbx * mfac_ref[g, tokens, :][:, 0:width]).astype(_BF16)
            rhs = jnp.where(blockdiag, jnp.concatenate([xs_g] * HEADS_PER_GROUP, axis=0), 0.0).astype(_BF16)
            h0 = hprev_ref[s, sl, :]
            e_g = pair(e_ref, g)
            y_g = _dot(mp, rhs) + _dot_nt(c_bf, h0.astype(_BF16)) * e_g + dsk_ref[:, sl] * xs_g
            y_ref[2 * g, tokens, :] = y_g[:, 0:LANES]
            y_ref[2 * g + 1, tokens, :] = y_g[:, LANES:GROUP_WIDTH]
            e_hi = e_g.astype(_BF16).astype(_F32)
            ez = jnp.where(t_row == seq - 1, e_hi, jnp.where(t_row == seq - 2, e_g - e_hi, 0.0))
            lhs_t = jnp.concatenate([pair(xw_ref, g), ez], axis=0).astype(_BF16)
            rhs_s = jnp.concatenate([jnp.concatenate([b_g, zeros_b], axis=1),
                                     jnp.concatenate([zeros_b, ones_b], axis=1)], axis=0).astype(_BF16)
            out = _dot_tn(lhs_t, rhs_s)
            nssm_ref[s, sl, :] = h0 * out[:, D_STATE:2 * D_STATE] + out[:, 0:D_STATE]
        return carry

    lax.fori_loop(0, nseq, per_seq, 0)

    z = z_ref[...].reshape(rows, D_INNER)
    parts = []
    for g in range(N_GROUPS):
        sl = slice(g * GROUP_WIDTH, (g + 1) * GROUP_WIDTH)
        gated = jnp.concatenate([y_ref[2 * g], y_ref[2 * g + 1]], axis=1) * _silu(z[:, sl])
        ms = jnp.mean(gated * gated, axis=-1, keepdims=True)
        parts.append((gated * lax.rsqrt(ms + EPS) * ng_ref[:, sl]).astype(_BF16))
    out = _dot(jnp.concatenate(parts, axis=1), wout_ref[...]).reshape(seq, nseq, D_MODEL)
    xo_ref[...] = _residual(x_ref[...], out, mod_ref[2:3], fin_ref, apply_final)


def _tm_ssd_layer(x, mod, w, i, state_conv_t, state_ssm, conv_acc, ssm_acc, apply_final):
    seq, b, _ = x.shape
    j = i // N_MIXERS
    nseq = SSD_SEQ_BLOCK
    rows = nseq * seq
    n_layers = state_ssm.shape[0]
    proj, dt = _tm_proj(x, mod, w, i, w["ssd_w_in"], D_INNER + CONV_DIM, True)
    proj = proj.reshape(seq, b, D_INNER + CONV_DIM)
    dt = dt.reshape(seq, b, LANES)
    kern = functools.partial(_tm_ssd_kernel, nseq=nseq, seq=seq, apply_final=apply_final)
    arrays, specs = _pick(_operands(w, i), _SSD_SAMPLE_OPS)
    exps, exp64 = _expansion_constants(seq)
    col_block = lambda n: pl.BlockSpec((seq, nseq, WEIGHT_COLS), lambda s: (0, s, n))
    in_specs = ([pl.BlockSpec((seq, nseq, D_MODEL), lambda s: (0, s, 0)),
                 pl.BlockSpec((3, nseq, D_MODEL), lambda s: (0, s, 0)),
                 col_block(0), col_block(1), col_block(2),
                 pl.BlockSpec((seq, nseq, LANES), lambda s: (0, s, 0))]
                + specs
                + [pl.BlockSpec((None, CONV_WIDTH - 1, nseq, CONV_DIM), lambda s: (j, 0, s, 0)),
                   pl.BlockSpec((None, nseq, D_INNER, D_STATE), lambda s: (j, s, 0, 0)),
                   _wspec(exps.shape, (0, 0)), _wspec(exp64.shape, (0, 0))])
    args = [x, mod, proj, proj, proj, dt, *arrays, state_conv_t, state_ssm, exps, exp64]
    n_in = len(args)
    aliases = {}
    if conv_acc is not None:
        in_specs += [pl.BlockSpec(memory_space=pl.ANY), pl.BlockSpec(memory_space=pl.ANY)]
        args += [conv_acc, ssm_acc]
        aliases = {n_in: 1, n_in + 1: 2}
        kern_fn = kern
    else:
        kern_fn = lambda *refs: kern(*refs[:n_in], None, None, *refs[n_in:])
    slab = lambda width: pltpu.VMEM((width // LANES, rows, LANES), _F32)
    return pl.pallas_call(
        kern_fn,
        grid=(b // nseq,),
        in_specs=in_specs,
        out_specs=[pl.BlockSpec((seq, nseq, D_MODEL), lambda s: (0, s, 0)),
                   pl.BlockSpec((None, CONV_WIDTH - 1, nseq, CONV_DIM), lambda s: (j, 0, s, 0)),
                   pl.BlockSpec((None, nseq, D_INNER, D_STATE), lambda s: (j, s, 0, 0))],
        out_shape=[jax.ShapeDtypeStruct((seq, b, D_MODEL), _F32),
                   jax.ShapeDtypeStruct((n_layers, CONV_WIDTH - 1, b, CONV_DIM), _F32),
                   jax.ShapeDtypeStruct((n_layers, b, D_INNER, D_STATE), _F32)],
        scratch_shapes=[pltpu.VMEM((seq + CONV_WIDTH - 1, nseq, CONV_DIM), _F32),
                        slab(CONV_DIM),
                        slab(N_GROUPS * LANES),
                        slab(D_INNER),
                        slab(D_INNER),
                        slab(D_INNER)],
        input_output_aliases=aliases,
        compiler_params=_params(1),
        name="ssd_sample",
    )(*args)


def _tm_pool_kernel(x_ref, mod_ref, g_ref, wu_ref, wz_ref, wg_ref, ps_ref, wout_ref, fin_ref, prev_ref,
                    _alias_ref, xo_ref, npool_ref, up_ref, *, nseq, seq, start_pos, apply_final):
    rows = nseq * seq
    x3 = x_ref[...]
    hn = _prenorm(x3, g_ref[...], mod_ref[0:1], mod_ref[1:2]).reshape(rows, D_MODEL).astype(_BF16)
    up_ref[0:POOL_STATE] = prev_ref[...]
    up_ref[POOL_STATE:POOL_STATE + seq] = _dot(hn, wu_ref[...]).reshape(seq, nseq, POOL_WIDTH)
    z = _dot(hn, wz_ref[...])
    pos3 = start_pos + lax.broadcasted_iota(jnp.int32, (seq, nseq, POOL_GROUP_DIM), 0)
    pos = pos3.astype(_F32).reshape(rows, POOL_GROUP_DIM)
    mixed = _pool_mix(
        lambda k, sl: up_ref[POOL_STATE - k:POOL_STATE - k + seq, :, sl].reshape(rows, POOL_GROUP_DIM),
        z, wg_ref, ps_ref, pos)
    out = _dot(mixed, wout_ref[...]).reshape(seq, nseq, D_MODEL)
    xo_ref[...] = _residual(x3, out, mod_ref[2:3], fin_ref, apply_final)
    npool_ref[...] = up_ref[seq:seq + POOL_STATE]


def _tm_pool_layer(x, mod, w, i, state_pool_t, pool_acc, start_pos, apply_final):
    seq, b, _ = x.shape
    j = i // N_MIXERS
    nseq = POOL_SEQ_BLOCK
    n_layers = state_pool_t.shape[0]
    kern = functools.partial(_tm_pool_kernel, nseq=nseq, seq=seq, start_pos=start_pos,
                             apply_final=apply_final)
    arrays, specs = _pick(_operands(w, i), _POOL_OPS)
    state_spec = pl.BlockSpec((None, POOL_STATE, nseq, POOL_WIDTH), lambda s: (j, 0, s, 0))
    in_specs = ([pl.BlockSpec((seq, nseq, D_MODEL), lambda s: (0, s, 0)),
                 pl.BlockSpec((3, nseq, D_MODEL), lambda s: (0, s, 0))] + specs + [state_spec])
    args = [x, mod, *arrays, state_pool_t]
    n_in = len(args)
    aliases = {}
    if pool_acc is not None:
        in_specs.append(pl.BlockSpec(memory_space=pl.ANY))
        args.append(pool_acc)
        aliases = {n_in: 1}
        kern_fn = kern
    else:
        kern_fn = lambda *refs: kern(*refs[:n_in], None, *refs[n_in:])
    return pl.pallas_call(
        kern_fn,
        grid=(b // nseq,),
        in_specs=in_specs,
        out_specs=[pl.BlockSpec((seq, nseq, D_MODEL), lambda s: (0, s, 0)), state_spec],
        out_shape=[jax.ShapeDtypeStruct((seq, b, D_MODEL), _F32),
                   jax.ShapeDtypeStruct((n_layers, POOL_STATE, b, POOL_WIDTH), _F32)],
        scratch_shapes=[pltpu.VMEM((POOL_STATE + seq, nseq, POOL_WIDTH), _F32)],
        input_output_aliases=aliases,
        compiler_params=_params(1),
        name="pool_sample",
    )(*args)


def _prepare_weights(norm_g, ssd_w_in, ssd_conv_w, ssd_conv_b, ssd_dt_bias, ssd_a_log, ssd_d, ssd_norm_g,
                     ssd_w_out, pool_w_in, pool_w_group, pool_scale, pool_w_out, final_norm_g):
    n_ssd = ssd_w_in.shape[0]
    n_pool = pool_w_in.shape[0]
    pad = LANES - N_HEADS
    return {
        "norm_g": norm_g.reshape(DEPTH, 1, D_MODEL),
        "final_g": final_norm_g.reshape(1, D_MODEL),
        "ssd_w_in": ssd_w_in.astype(_BF16),
        "ssd_w_dt": jnp.pad(ssd_w_in[:, :, D_INNER + CONV_DIM:], ((0, 0), (0, 0), (0, pad))).astype(_BF16),
        "ssd_conv_w": ssd_conv_w,
        "ssd_conv_b": ssd_conv_b.reshape(n_ssd, 1, CONV_DIM),
        "ssd_dt_bias": jnp.pad(ssd_dt_bias, ((0, 0), (0, pad))).reshape(n_ssd, 1, LANES),
        "ssd_a_log": jnp.pad(ssd_a_log, ((0, 0), (0, pad))).reshape(n_ssd, 1, LANES),
        "ssd_d": jnp.repeat(ssd_d, HEAD_DIM, axis=1).reshape(n_ssd, 1, D_INNER),
        "ssd_norm_g": ssd_norm_g.reshape(n_ssd, 1, D_INNER),
        "ssd_w_out": ssd_w_out.astype(_BF16),
        "pool_w_in": pool_w_in.astype(_BF16),
        "pool_w_group": pool_w_group.astype(_BF16),
        "pool_scale": pool_scale.reshape(n_pool, 1, POOL_WIDTH),
        "pool_w_out": pool_w_out.astype(_BF16),
    }


def kernel(x_prompt, x_sample, state_ssm, state_conv, state_pool, c_prompt, c_sample, ada_w, ada_b, norm_g,
           ssd_w_in, ssd_conv_w, ssd_conv_b, ssd_dt_bias, ssd_a_log, ssd_d, ssd_norm_g, ssd_w_out, pool_w_in,
           pool_w_group, pool_scale, pool_w_out, final_norm_g):
    b_p = x_prompt.shape[0]
    b_s = x_sample.shape[0]
    n_ssd = state_ssm.shape[0]
    w = _prepare_weights(norm_g, ssd_w_in, ssd_conv_w, ssd_conv_b, ssd_dt_bias, ssd_a_log, ssd_d, ssd_norm_g,
                         ssd_w_out, pool_w_in, pool_w_group, pool_scale, pool_w_out, final_norm_g)
    mod = _modulation(jnp.concatenate([c_prompt, c_sample], axis=0), ada_w, ada_b)
    mod_p = mod[:, :b_p].reshape(DEPTH, b_p, 3, D_MODEL)
    mod_s = mod[:, b_p:].reshape(DEPTH, b_s, 3, D_MODEL).transpose(0, 2, 1, 3)
    ssm_in = state_ssm.reshape(n_ssd, b_s, D_INNER, D_STATE)
    conv_in = state_conv.transpose(0, 2, 1, 3)
    pool_in = state_pool.transpose(0, 2, 1, 3)

    xp, xs = x_prompt, x_sample.transpose(1, 0, 2)
    ssm_p, conv_p, pool_p = [], [], []
    ssm_s = conv_s = pool_s = None
    for i in range(DEPTH):
        last = i == DEPTH - 1
        if i % N_MIXERS == 0:
            xp, cv, st = _ssd_prompt_layer(xp, mod_p[i], w, i, last)
            conv_p.append(cv)
            ssm_p.append(st)
            xs, conv_s, ssm_s = _tm_ssd_layer(xs, mod_s[i], w, i, conv_in, ssm_in, conv_s, ssm_s, last)
        else:
            xp, ps = _pool_prompt_layer(xp, mod_p[i], w, i, last)
            pool_p.append(ps)
            xs, pool_s = _tm_pool_layer(xs, mod_s[i], w, i, pool_in, pool_s, PAST_LEN, last)

    head_shape = (N_HEADS, HEAD_DIM, D_STATE)
    return (xp, xs.transpose(1, 0, 2),
            jnp.stack(ssm_p).reshape(n_ssd, b_p, *head_shape),
            jnp.stack(conv_p), jnp.stack(pool_p),
            ssm_s.reshape(n_ssd, b_s, *head_shape),
            conv_s.transpose(0, 2, 1, 3), pool_s.transpose(0, 2, 1, 3))
```

```python
import functools

import numpy as np
import jax
import jax.numpy as jnp
from jax import lax
from jax.experimental import pallas as pl
from jax.experimental.pallas import tpu as pltpu

D_MODEL = 1024
DEPTH = 4
PAST_LEN = 16384
N_MIXERS = 2
D_INNER = 2048
HEAD_DIM = 64
N_HEADS = 32
N_GROUPS = 8
HEADS_PER_GROUP = 4
D_STATE = 128
CONV_WIDTH = 4
CONV_DIM = D_INNER + 2 * N_GROUPS * D_STATE
GROUP_WIDTH = HEADS_PER_GROUP * HEAD_DIM
POOL_WIDTH = 2048
POOL_WINDOWS = (2, 4, 8, 16)
POOL_GROUP_DIM = 512
POOL_STATE = 15
EPS = 1e-6

LANES = 128
SUBLANES = 8
CHUNK = 128
N_PHASES = CHUNK // SUBLANES
PROMPT_TILE = 512
SSD_PROMPT_TILE = 512
SSD_SEQ_BLOCK = 8
POOL_SEQ_BLOCK = 16
POOL_HIST = 2 * SUBLANES
WEIGHT_COLS = 2048
PIECE_COLS = 256
PROJ_COLS = 1024
VMEM_LIMIT = 60 * 1024 * 1024
NEG_BIG = -1e30
LOG2E = 1.4426950408889634

_F32 = jnp.float32
_BF16 = jnp.bfloat16
_HI = lax.Precision.HIGHEST


def _silu(v):
    h = 0.5 * v
    return h + h * jnp.tanh(h)


def _softplus(v):
    return jnp.maximum(v, 0.0) + jnp.log(1.0 + jnp.exp(-jnp.abs(v)))


def _dot(a, b):
    return jnp.dot(a, b, preferred_element_type=_F32)


def _dot_exact(a, b):
    return jnp.dot(a, b, precision=_HI, preferred_element_type=_F32)


def _dot_nt(a, b):
    return lax.dot_general(a, b, (((1,), (1,)), ((), ())), preferred_element_type=_F32)


def _dot_tn(a, b):
    return lax.dot_general(a, b, (((0,), (0,)), ((), ())), preferred_element_type=_F32)


def _select_lanes(v, onehot_bf, terms):
    acc = None
    rest = v
    for _ in range(terms):
        piece = rest.astype(_BF16)
        part = _dot(piece, onehot_bf)
        acc = part if acc is None else acc + part
        rest = rest - piece.astype(_F32)
    return acc


def _prenorm(x, g, shift, scale):
    ms = jnp.mean(x * x, axis=-1, keepdims=True)
    return x * lax.rsqrt(ms + EPS) * g * (1.0 + scale) + shift


def _residual(x, out, gate, fin_ref, apply_final):
    y = x + (1.0 + gate) * out
    if apply_final:
        ms = jnp.mean(y * y, axis=-1, keepdims=True)
        y = y * lax.rsqrt(ms + EPS) * fin_ref[...]
    return y


def _group_rmsnorm_gate(y_ref, z_ref, ng_ref):
    parts = []
    for g in range(N_GROUPS):
        sl = slice(g * GROUP_WIDTH, (g + 1) * GROUP_WIDTH)
        gated = y_ref[:, sl] * _silu(z_ref[:, sl])
        ms = jnp.mean(gated * gated, axis=-1, keepdims=True)
        parts.append((gated * lax.rsqrt(ms + EPS) * ng_ref[:, sl]).astype(_BF16))
    return jnp.concatenate(parts, axis=1)


def _wspec(block_shape, index):
    return pl.BlockSpec(block_shape, lambda *_: index, pipeline_mode=pl.Buffered(1))


def _params(n_axes):
    return pltpu.CompilerParams(dimension_semantics=("arbitrary",) * n_axes, vmem_limit_bytes=VMEM_LIMIT)


def _operands(w, i):
    j = i // N_MIXERS
    wcols = (None, D_MODEL, WEIGHT_COLS)
    ops = {"norm_g": (w["norm_g"], _wspec((None, 1, D_MODEL), (i, 0, 0))),
           "final_g": (w["final_g"], _wspec((1, D_MODEL), (0, 0)))}
    if i % N_MIXERS == 0:
        ops.update(
            w_z=(w["ssd_w_in"], _wspec(wcols, (j, 0, 0))),
            w_x=(w["ssd_w_in"], _wspec(wcols, (j, 0, 1))),
            w_bc=(w["ssd_w_in"], _wspec(wcols, (j, 0, 2))),
            w_dt=(w["ssd_w_dt"], _wspec((None, D_MODEL, LANES), (j, 0, 0))),
            conv_w=(w["ssd_conv_w"], _wspec((None, CONV_WIDTH, CONV_DIM), (j, 0, 0))),
            conv_b=(w["ssd_conv_b"], _wspec((None, 1, CONV_DIM), (j, 0, 0))),
            dt_bias=(w["ssd_dt_bias"], _wspec((None, 1, LANES), (j, 0, 0))),
            a_log=(w["ssd_a_log"], _wspec((None, 1, LANES), (j, 0, 0))),
            d_skip=(w["ssd_d"], _wspec((None, 1, D_INNER), (j, 0, 0))),
            ssd_norm_g=(w["ssd_norm_g"], _wspec((None, 1, D_INNER), (j, 0, 0))),
            w_out=(w["ssd_w_out"], _wspec((None, D_INNER, D_MODEL), (j, 0, 0))),
        )
    else:
        ops.update(
            w_u=(w["pool_w_in"], _wspec(wcols, (j, 0, 0))),
            w_zp=(w["pool_w_in"], _wspec(wcols, (j, 0, 1))),
            w_group=(w["pool_w_group"],
                     _wspec((None, len(POOL_WINDOWS), POOL_GROUP_DIM, POOL_GROUP_DIM), (j, 0, 0, 0))),
            pool_scale=(w["pool_scale"], _wspec((None, 1, POOL_WIDTH), (j, 0, 0))),
            w_out=(w["pool_w_out"], _wspec((None, POOL_WIDTH, D_MODEL), (j, 0, 0))),
        )
    return ops


def _pick(ops, names):
    return [ops[n][0] for n in names], [ops[n][1] for n in names]


def _mod_kernel(c_ref, w_ref, b_ref, o_ref):
    sc = _silu(c_ref[...]).astype(_BF16)
    o_ref[...] = _dot(sc, w_ref[...].astype(_BF16)) + b_ref[...]


def _modulation(c_all, ada_w, ada_b):
    n = c_all.shape[0]
    tn = 1024
    return pl.pallas_call(
        _mod_kernel,
        grid=(DEPTH, 3 * D_MODEL // tn),
        in_specs=[
            pl.BlockSpec((n, D_MODEL), lambda i, j: (0, 0)),
            pl.BlockSpec((None, D_MODEL, tn), lambda i, j: (i, 0, j)),
            pl.BlockSpec((None, 1, tn), lambda i, j: (i, 0, j)),
        ],
        out_specs=pl.BlockSpec((None, n, tn), lambda i, j: (i, 0, j)),
        out_shape=jax.ShapeDtypeStruct((DEPTH, n, 3 * D_MODEL), _F32),
        compiler_params=_params(2),
        name="adaln_mod",
    )(c_all, ada_w, ada_b.reshape(DEPTH, 1, 3 * D_MODEL))


_SSD_PROMPT_OPS = ("norm_g", "w_z", "w_x", "w_bc", "w_dt", "conv_w", "conv_b", "dt_bias", "a_log", "d_skip",
                   "ssd_norm_g", "w_out", "final_g")


def _ssd_prompt_kernel(*refs, tile, n_tiles, apply_final):
    n_xcols = D_MODEL // LANES
    x_refs = refs[:n_xcols]
    (mod_ref, g_ref, wz_ref, wx_ref, wbc_ref, wdt_ref, cw_ref, cb_ref, dtb_ref, alog_ref, dsk_ref, ng_ref,
     wout_ref, fin_ref, xo_ref, nconv_ref, nssm_ref,
     ht_ref, carry_ref, act_ref, z_ref, y_ref, stage_ref) = refs[n_xcols:]
    l = pl.program_id(1)
    n_chunks = tile // CHUNK
    n_carry = CONV_WIDTH - 1

    @pl.when(l == 0)
    def _():
        ht_ref[...] = jnp.zeros_like(ht_ref)
        carry_ref[...] = jnp.zeros_like(carry_ref)

    def load_x():
        return jnp.concatenate(
            [jnp.concatenate([xr[pl.ds(c * CHUNK + r, SUBLANES, stride=N_PHASES), :] for xr in x_refs], axis=1)
             for c in range(n_chunks) for r in range(N_PHASES)], axis=0)

    hn = _prenorm(load_x(), g_ref[...], mod_ref[0:1, :], mod_ref[1:2, :]).astype(_BF16)
    dt = _softplus(_dot(hn, wdt_ref[...]) + dtb_ref[...])

    last = (N_PHASES - n_carry) * SUBLANES
    for k in range(CONV_DIM // PIECE_COLS):
        cols = slice(k * PIECE_COLS, (k + 1) * PIECE_COLS)
        half = D_INNER // PIECE_COLS
        w_ref, kw = (wx_ref, k) if k < half else (wbc_ref, k - half)
        xb = _dot(hn, w_ref[:, kw * PIECE_COLS:(kw + 1) * PIECE_COLS])
        prev = carry_ref[:, cols]
        for c in range(n_chunks):
            cur = xb[c * CHUNK:(c + 1) * CHUNK, :]
            tail = cur[last:CHUNK, :]
            wrapped = [jnp.concatenate([prev[j * SUBLANES + SUBLANES - 1:(j + 1) * SUBLANES, :],
                                        tail[j * SUBLANES:(j + 1) * SUBLANES - 1, :]], axis=0)
                       for j in range(n_carry)]
            ext = jnp.concatenate(wrapped + [cur], axis=0)
            conv = cb_ref[:, cols]
            for kk in range(CONV_WIDTH):
                conv = conv + ext[kk * SUBLANES:kk * SUBLANES + CHUNK, :] * cw_ref[kk:kk + 1, cols]
            act_ref[c * CHUNK:(c + 1) * CHUNK, cols] = _silu(conv)
            prev = tail
        carry_ref[:, cols] = prev
        for j in range(n_carry):
            nconv_ref[j:j + 1, cols] = prev[(j + 1) * SUBLANES - 1:(j + 1) * SUBLANES, :]

    row = lax.broadcasted_iota(jnp.int32, (CHUNK, CHUNK), 0)
    col = lax.broadcasted_iota(jnp.int32, (CHUNK, CHUNK), 1)
    token = lambda p: (p % SUBLANES) * N_PHASES + p // SUBLANES
    causal = token(col) <= token(row)
    tril = causal.astype(_F32)
    head_of_lane = lax.broadcasted_iota(jnp.int32, (CHUNK, GROUP_WIDTH), 1) // HEAD_DIM
    low_half = lax.broadcasted_iota(jnp.int32, (CHUNK, LANES), 1) < HEAD_DIM
    neg_a = -jnp.exp(alog_ref[...])

    def decay_stage(c):
        dt_c = dt[c * CHUNK:(c + 1) * CHUNK, :]
        acum = _dot_exact(tril, dt_c * neg_a)
        acum2 = acum * LOG2E
        w_c = dt_c * jnp.exp(acum[CHUNK - 1:CHUNK, :] - acum)
        srow_t = (jnp.log(dt_c) * LOG2E - acum2).T
        return acum2, srow_t, w_c.T

    def cb_stage(c, g):
        rows = slice(c * CHUNK, (c + 1) * CHUNK)
        b_g = act_ref[rows, D_INNER + g * D_STATE:D_INNER + (g + 1) * D_STATE]
        c_lo = D_INNER + (N_GROUPS + g) * D_STATE
        c_bf = act_ref[rows, c_lo:c_lo + D_STATE].astype(_BF16)
        bt_g = b_g.T
        return c_bf, bt_g, _dot(c_bf, bt_g.astype(_BF16))

    def operand_stage(c, g, decay, cbs):
        acum2, srow_t, w_t = decay
        _, bt_g, cb = cbs
        xs_g = act_ref[c * CHUNK:(c + 1) * CHUNK, g * GROUP_WIDTH:(g + 1) * GROUP_WIDTH]
        rhs = jnp.concatenate(
            [jnp.where(head_of_lane == hh, xs_g, 0.0).astype(_BF16) for hh in range(HEADS_PER_GROUP)],
            axis=0)
        m_parts, w_parts, e_cols = [], [], []
        for hh in range(HEADS_PER_GROUP):
            h = g * HEADS_PER_GROUP + hh
            acum_bc = jnp.broadcast_to(acum2[:, h:h + 1], (CHUNK, CHUNK))
            seg = acum_bc + srow_t[h:h + 1, :]
            m_parts.append((cb * jnp.exp2(jnp.where(causal, seg, NEG_BIG))).astype(_BF16))
            w_parts.append((bt_g * w_t[h:h + 1, :]).astype(_BF16))
            e_cols.append(jnp.exp2(acum_bc))
        lhs = jnp.concatenate(
            [jnp.concatenate(m_parts, axis=1), jnp.concatenate(w_parts, axis=1)], axis=0)
        e_g = jnp.concatenate(
            [jnp.where(low_half, e_cols[0], e_cols[1]), jnp.where(low_half, e_cols[2], e_cols[3])],
            axis=1)
        return lhs, rhs, e_g

    def output_stage(c, g, cbs, e_g, out):
        rows = slice(c * CHUNK, (c + 1) * CHUNK)
        sl = slice(g * GROUP_WIDTH, (g + 1) * GROUP_WIDTH)
        ht_g = ht_ref[g]
        y_inter = _dot(cbs[0], ht_g.astype(_BF16))
        y_ref[rows, sl] = out[0:CHUNK, :] + y_inter * e_g + dsk_ref[:, sl] * act_ref[rows, sl]
        ht_ref[g] = ht_g * e_g[CHUNK - 1:CHUNK, :] + out[CHUNK:2 * CHUNK, :]

    def gate_proj_piece(p):
        if p < D_INNER // PIECE_COLS:
            zc = slice(p * PIECE_COLS, (p + 1) * PIECE_COLS)
            z_ref[:, zc] = _dot(hn, wz_ref[:, zc])

    groups = range(N_GROUPS)
    decays = [decay_stage(c) for c in range(n_chunks)]
    cbs = [[cb_stage(c, g) for g in groups] for c in range(n_chunks)]
    operands = [operand_stage(0, g, decays[0], cbs[0][g]) for g in groups]
    for c in range(n_chunks):
        outs, nxt = [], []
        for g in groups:
            lhs, rhs, _ = operands[g]
            outs.append(_dot(lhs, rhs))
            if c + 1 < n_chunks:
                nxt.append(operand_stage(c + 1, g, decays[c + 1], cbs[c + 1][g]))
        for g in groups:
            output_stage(c, g, cbs[c][g], operands[g][2], outs[g])
            gate_proj_piece(c * N_GROUPS + g)
        operands = nxt

    gn = _group_rmsnorm_gate(y_ref, z_ref, ng_ref)
    xo = _residual(load_x(), _dot(gn, wout_ref[...]), mod_ref[2:3, :], fin_ref, apply_final)
    for c in range(n_chunks):
        for r in range(N_PHASES):
            p0 = c * CHUNK + r * SUBLANES
            for j in range(n_xcols):
                stage_ref[j, pl.ds(c * CHUNK + r, SUBLANES, stride=N_PHASES), :] = (
                    xo[p0:p0 + SUBLANES, j * LANES:(j + 1) * LANES])
    for j in range(n_xcols):
        xo_ref[:, j * LANES:(j + 1) * LANES] = stage_ref[j]

    @pl.when(l == n_tiles - 1)
    def _():
        for g in range(N_GROUPS):
            nssm_ref[g * GROUP_WIDTH:(g + 1) * GROUP_WIDTH, :] = ht_ref[g].T


def _ssd_prompt_layer(x, mod, w, i, apply_final):
    b, L, _ = x.shape
    tile = SSD_PROMPT_TILE
    n_tiles = L // tile
    assert tile // CHUNK * N_GROUPS >= D_INNER // PIECE_COLS
    kern = functools.partial(_ssd_prompt_kernel, tile=tile, n_tiles=n_tiles, apply_final=apply_final)
    arrays, specs = _pick(_operands(w, i), _SSD_PROMPT_OPS)
    n_xcols = D_MODEL // LANES
    x_specs = [pl.BlockSpec((None, tile, LANES), lambda s, l, j=j: (s, l, j)) for j in range(n_xcols)]
    return pl.pallas_call(
        kern,
        grid=(b, n_tiles),
        in_specs=x_specs + [pl.BlockSpec((None, 3, D_MODEL), lambda s, l: (s, 0, 0))] + specs,
        out_specs=[pl.BlockSpec((None, tile, D_MODEL), lambda s, l: (s, l, 0)),
                   pl.BlockSpec((None, CONV_WIDTH - 1, CONV_DIM), lambda s, l: (s, 0, 0)),
                   pl.BlockSpec((None, D_INNER, D_STATE), lambda s, l: (s, 0, 0))],
        out_shape=[jax.ShapeDtypeStruct((b, L, D_MODEL), _F32),
                   jax.ShapeDtypeStruct((b, CONV_WIDTH - 1, CONV_DIM), _F32),
                   jax.ShapeDtypeStruct((b, D_INNER, D_STATE), _F32)],
        scratch_shapes=[pltpu.VMEM((N_GROUPS, D_STATE, GROUP_WIDTH), _F32),
                        pltpu.VMEM(((CONV_WIDTH - 1) * SUBLANES, CONV_DIM), _F32),
                        pltpu.VMEM((tile, CONV_DIM), _F32),
                        pltpu.VMEM((tile, D_INNER), _F32),
                        pltpu.VMEM((tile, D_INNER), _F32),
                        pltpu.VMEM((n_xcols, tile, LANES), _F32)],
        compiler_params=_params(2),
        name="ssd_prompt",
    )(*([x] * n_xcols), mod, *arrays)


def _sample_proj_kernel(*refs, nseq, seq, has_dt):
    if has_dt:
        x_ref, mod_ref, g_ref, w_ref, wdt_ref, dtb_ref, o_ref, dt_ref, hn_ref = refs
    else:
        x_ref, mod_ref, g_ref, w_ref, o_ref, hn_ref = refs

    @pl.when(pl.program_id(0) == 0)
    def _():
        hn = _prenorm(x_ref[...], g_ref[...], mod_ref[:, 0:1, :], mod_ref[:, 1:2, :])
        hn_ref[...] = hn.reshape(nseq * seq, D_MODEL).astype(_BF16)
        if has_dt:
            dt_ref[...] = _softplus(_dot(hn_ref[...], wdt_ref[...]) + dtb_ref[...])

    o_ref[...] = _dot(hn_ref[...], w_ref[...])


def _sample_proj(x, mod, w, i, weight, n_cols, has_dt):
    nseq, seq, _ = x.shape
    rows = nseq * seq
    j = i // N_MIXERS
    ops = _operands(w, i)
    arrays = [x, mod, ops["norm_g"][0], weight]
    specs = [pl.BlockSpec((nseq, seq, D_MODEL), lambda n: (0, 0, 0)),
             pl.BlockSpec((nseq, 3, D_MODEL), lambda n: (0, 0, 0)),
             ops["norm_g"][1],
             pl.BlockSpec((None, D_MODEL, PROJ_COLS), lambda n: (j, 0, n))]
    out_shape = [jax.ShapeDtypeStruct((rows, n_cols), _F32)]
    out_specs = [pl.BlockSpec((rows, PROJ_COLS), lambda n: (0, n))]
    if has_dt:
        extra, extra_specs = _pick(ops, ("w_dt", "dt_bias"))
        arrays += extra
        specs += extra_specs
        out_shape.append(jax.ShapeDtypeStruct((rows, LANES), _F32))
        out_specs.append(pl.BlockSpec((rows, LANES), lambda n: (0, 0)))
    return pl.pallas_call(
        functools.partial(_sample_proj_kernel, nseq=nseq, seq=seq, has_dt=has_dt),
        grid=(n_cols // PROJ_COLS,),
        in_specs=specs,
        out_specs=out_specs,
        out_shape=out_shape,
        scratch_shapes=[pltpu.VMEM((rows, D_MODEL), _BF16)],
        compiler_params=_params(1),
        name="sample_proj",
    )(*arrays)


_SSD_SAMPLE_OPS = ("conv_w", "conv_b", "a_log", "d_skip", "ssd_norm_g", "w_out", "final_g")


def _ssd_sample_kernel(x_ref, mod_ref, z_ref, xin_ref, bcin_ref, dt_ref,
                       cw_ref, cb_ref, alog_ref, dsk_ref, ng_ref, wout_ref, fin_ref,
                       cprev_ref, hprev_ref, exps_ref, exp64_ref, _alias_ref,
                       xo_ref, nconv_ref, nssm_ref,
                       xp_ref, act_ref, mfac_ref, e_ref, xw_ref, y_ref, *, nseq, seq, apply_final):
    rows = nseq * seq
    base = SUBLANES - (CONV_WIDTH - 1)
    xp_ref[:, base:SUBLANES, :] = cprev_ref[...]
    xp_ref[:, SUBLANES:SUBLANES + seq, 0:D_INNER] = xin_ref[...].reshape(nseq, seq, D_INNER)
    xp_ref[:, SUBLANES:SUBLANES + seq, D_INNER:CONV_DIM] = bcin_ref[...].reshape(nseq, seq, D_INNER)
    conv = cb_ref[...]
    for k in range(CONV_WIDTH):
        conv = conv + xp_ref[:, base + k:base + k + seq, :] * cw_ref[k:k + 1, :]
    act_ref[...] = _silu(conv)
    nconv_ref[...] = xp_ref[:, SUBLANES + seq - (CONV_WIDTH - 1):SUBLANES + seq, :]

    dt = dt_ref[...]
    a = dt * (-jnp.exp(alog_ref[...]))
    r_i = lax.broadcasted_iota(jnp.int32, (rows, rows), 0)
    c_i = lax.broadcasted_iota(jnp.int32, (rows, rows), 1)
    same_seq = (r_i // seq) == (c_i // seq)
    acum = _dot_exact((same_seq & (c_i <= r_i)).astype(_F32), a)
    atot = _dot_exact(same_seq.astype(_F32), a)
    w = dt * jnp.exp(atot - acum)

    shape3 = (nseq, seq, N_GROUPS * LANES)
    z3 = _select_lanes(acum, exps_ref[...], 3).reshape(shape3)
    dtz3 = _select_lanes(dt, exps_ref[...], 2).reshape(shape3)
    t3 = lax.broadcasted_iota(jnp.int32, shape3, 1)
    s3 = lax.broadcasted_iota(jnp.int32, shape3, 2) % seq
    diag = t3 == s3
    acum_s = jnp.sum(jnp.where(diag, z3, 0.0), axis=1, keepdims=True)
    dt_s = jnp.sum(jnp.where(diag, dtz3, 0.0), axis=1, keepdims=True)
    mfac_ref[...] = jnp.exp(jnp.where(s3 <= t3, z3 - acum_s, NEG_BIG)) * dt_s
    e_ref[...] = _select_lanes(jnp.exp(acum), exp64_ref[...], 2).reshape(nseq, seq, D_INNER)
    xw_ref[...] = act_ref[:, :, 0:D_INNER] * _select_lanes(w, exp64_ref[...], 2).reshape(nseq, seq, D_INNER)

    width = HEADS_PER_GROUP * seq
    bd_r = lax.broadcasted_iota(jnp.int32, (width, GROUP_WIDTH), 0)
    bd_c = lax.broadcasted_iota(jnp.int32, (width, GROUP_WIDTH), 1)
    blockdiag = (bd_r // seq) == (bd_c // HEAD_DIM)
    t_row = lax.broadcasted_iota(jnp.int32, (seq, GROUP_WIDTH), 0)
    zeros_b = jnp.zeros((seq, D_STATE), _F32)
    ones_b = jnp.ones((seq, D_STATE), _F32)

    def per_seq(s, carry):
        for g in range(N_GROUPS):
            b_g = act_ref[s, :, D_INNER + g * D_STATE:D_INNER + (g + 1) * D_STATE]
            c_lo = D_INNER + (N_GROUPS + g) * D_STATE
            c_bf = act_ref[s, :, c_lo:c_lo + D_STATE].astype(_BF16)
            sl = slice(g * GROUP_WIDTH, (g + 1) * GROUP_WIDTH)
            xs_g = act_ref[s, :, sl]
            cbx = _dot_nt(c_bf, jnp.concatenate([b_g] * HEADS_PER_GROUP, axis=0).astype(_BF16))
            mp = (cbx * mfac_ref[s, :, g * LANES:g * LANES + width]).astype(_BF16)
            rhs = jnp.where(blockdiag, jnp.concatenate([xs_g] * HEADS_PER_GROUP, axis=0), 0.0).astype(_BF16)
            h0 = hprev_ref[s, sl, :]
            e_g = e_ref[s, :, sl]
            y_ref[pl.ds(pl.multiple_of(s * seq, seq), seq), sl] = (
                _dot(mp, rhs) + _dot_nt(c_bf, h0.astype(_BF16)) * e_g + dsk_ref[:, sl] * xs_g)
            e_hi = e_g.astype(_BF16).astype(_F32)
            ez = jnp.where(t_row == seq - 1, e_hi, jnp.where(t_row == seq - 2, e_g - e_hi, 0.0))
            lhs_t = jnp.concatenate([xw_ref[s, :, sl], ez], axis=0).astype(_BF16)
            rhs_s = jnp.concatenate([jnp.concatenate([b_g, zeros_b], axis=1),
                                     jnp.concatenate([zeros_b, ones_b], axis=1)], axis=0).astype(_BF16)
            out = _dot_tn(lhs_t, rhs_s)
            nssm_ref[s, sl, :] = h0 * out[:, D_STATE:2 * D_STATE] + out[:, 0:D_STATE]
        return carry

    lax.fori_loop(0, nseq, per_seq, 0)

    gn = _group_rmsnorm_gate(y_ref, z_ref, ng_ref)
    out = _dot(gn, wout_ref[...]).reshape(nseq, seq, D_MODEL)
    xo_ref[...] = _residual(x_ref[...], out, mod_ref[:, 2:3, :], fin_ref, apply_final)


def _expansion_constants(seq):
    exps = np.zeros((LANES, N_GROUPS * LANES), np.float32)
    for h in range(N_HEADS):
        g, hh = divmod(h, HEADS_PER_GROUP)
        exps[h, g * LANES + hh * seq:g * LANES + (hh + 1) * seq] = 1.0
    exp64 = np.zeros((LANES, D_INNER), np.float32)
    for h in range(N_HEADS):
        exp64[h, h * HEAD_DIM:(h + 1) * HEAD_DIM] = 1.0
    return jnp.asarray(exps, _BF16), jnp.asarray(exp64, _BF16)


def _ssd_sample_layer(x, mod, w, i, state_conv, state_ssm, ssm_acc, apply_final):
    b, seq, _ = x.shape
    j = i // N_MIXERS
    nseq = SSD_SEQ_BLOCK
    rows = nseq * seq
    proj, dt = _sample_proj(x, mod, w, i, w["ssd_w_in"], D_INNER + CONV_DIM, True)
    kern = functools.partial(_ssd_sample_kernel, nseq=nseq, seq=seq, apply_final=apply_final)
    arrays, specs = _pick(_operands(w, i), _SSD_SAMPLE_OPS)
    exps, exp64 = _expansion_constants(seq)
    n_layers = state_ssm.shape[0]
    col_block = lambda n: pl.BlockSpec((rows, WEIGHT_COLS), lambda s: (s, n))
    in_specs = ([pl.BlockSpec((nseq, seq, D_MODEL), lambda s: (s, 0, 0)),
                 pl.BlockSpec((nseq, 3, D_MODEL), lambda s: (s, 0, 0)),
                 col_block(0), col_block(1), col_block(2),
                 pl.BlockSpec((rows, LANES), lambda s: (s, 0))]
                + specs
                + [pl.BlockSpec((None, nseq, CONV_WIDTH - 1, CONV_DIM), lambda s: (j, s, 0, 0)),
                   pl.BlockSpec((None, nseq, D_INNER, D_STATE), lambda s: (j, s, 0, 0)),
                   _wspec(exps.shape, (0, 0)), _wspec(exp64.shape, (0, 0))])
    args = [x, mod, proj, proj, proj, dt, *arrays, state_conv, state_ssm, exps, exp64]
    aliases = {}
    if ssm_acc is not None:
        in_specs.append(pl.BlockSpec(memory_space=pl.ANY))
        args.append(ssm_acc)
        aliases = {len(args) - 1: 2}
        kern_fn = kern
    else:
        n_in = len(args)
        kern_fn = lambda *refs: kern(*refs[:n_in], None, *refs[n_in:])
    return pl.pallas_call(
        kern_fn,
        grid=(b // nseq,),
        in_specs=in_specs,
        out_specs=[pl.BlockSpec((nseq, seq, D_MODEL), lambda s: (s, 0, 0)),
                   pl.BlockSpec((nseq, CONV_WIDTH - 1, CONV_DIM), lambda s: (s, 0, 0)),
                   pl.BlockSpec((None, nseq, D_INNER, D_STATE), lambda s: (j, s, 0, 0))],
        out_shape=[jax.ShapeDtypeStruct((b, seq, D_MODEL), _F32),
                   jax.ShapeDtypeStruct((b, CONV_WIDTH - 1, CONV_DIM), _F32),
                   jax.ShapeDtypeStruct((n_layers, b, D_INNER, D_STATE), _F32)],
        scratch_shapes=[pltpu.VMEM((nseq, 2 * SUBLANES, CONV_DIM), _F32),
                        pltpu.VMEM((nseq, seq, CONV_DIM), _F32),
                        pltpu.VMEM((nseq, seq, N_GROUPS * LANES), _F32),
                        pltpu.VMEM((nseq, seq, D_INNER), _F32),
                        pltpu.VMEM((nseq, seq, D_INNER), _F32),
                        pltpu.VMEM((rows, D_INNER), _F32)],
        input_output_aliases=aliases,
        compiler_params=_params(1),
        name="ssd_sample",
    )(*args)


_POOL_OPS = ("norm_g", "w_u", "w_zp", "w_group", "pool_scale", "w_out", "final_g")


def _pool_mix(load_rows, z, wg_ref, ps_ref, pos):
    parts = []
    for g, w in enumerate(POOL_WINDOWS):
        sl = slice(g * POOL_GROUP_DIM, (g + 1) * POOL_GROUP_DIM)
        u_g = load_rows(0, sl)
        win = u_g
        for k in range(1, w):
            win = win + load_rows(k, sl)
        cnt = jnp.minimum(float(w), pos + 1.0)
        pooled = (win / cnt - u_g).astype(_BF16)
        mixed = _dot(pooled, wg_ref[g])
        parts.append((mixed * ps_ref[:, sl] * _silu(z[:, sl])).astype(_BF16))
    return jnp.concatenate(parts, axis=1)


def _pool_prompt_kernel(x_ref, mod_ref, g_ref, wu_ref, wz_ref, wg_ref, ps_ref, wout_ref, fin_ref,
                        xo_ref, npool_ref, up_ref, *, tile, apply_final):
    l = pl.program_id(1)
    hist = POOL_HIST

    @pl.when(l == 0)
    def _():
        up_ref[0:hist, :] = jnp.zeros((hist, POOL_WIDTH), _F32)

    x = x_ref[...]
    hn = _prenorm(x, g_ref[...], mod_ref[0:1, :], mod_ref[1:2, :]).astype(_BF16)
    up_ref[hist:hist + tile, :] = _dot(hn, wu_ref[...])
    z = _dot(hn, wz_ref[...])
    pos = (l * tile + lax.broadcasted_iota(jnp.int32, (tile, POOL_GROUP_DIM), 0)).astype(_F32)
    mixed = _pool_mix(lambda k, sl: up_ref[hist - k:hist - k + tile, sl], z, wg_ref, ps_ref, pos)
    xo_ref[...] = _residual(x, _dot(mixed, wout_ref[...]), mod_ref[2:3, :], fin_ref, apply_final)
    tail = up_ref[hist + tile - POOL_STATE:hist + tile, :]
    up_ref[hist - POOL_STATE:hist, :] = tail
    npool_ref[...] = tail


def _pool_prompt_layer(x, mod, w, i, apply_final):
    b, L, _ = x.shape
    tile = PROMPT_TILE
    kern = functools.partial(_pool_prompt_kernel, tile=tile, apply_final=apply_final)
    arrays, specs = _pick(_operands(w, i), _POOL_OPS)
    return pl.pallas_call(
        kern,
        grid=(b, L // tile),
        in_specs=[pl.BlockSpec((None, tile, D_MODEL), lambda s, l: (s, l, 0)),
                  pl.BlockSpec((None, 3, D_MODEL), lambda s, l: (s, 0, 0))] + specs,
        out_specs=[pl.BlockSpec((None, tile, D_MODEL), lambda s, l: (s, l, 0)),
                   pl.BlockSpec((None, POOL_STATE, POOL_WIDTH), lambda s, l: (s, 0, 0))],
        out_shape=[jax.ShapeDtypeStruct((b, L, D_MODEL), _F32),
                   jax.ShapeDtypeStruct((b, POOL_STATE, POOL_WIDTH), _F32)],
        scratch_shapes=[pltpu.VMEM((POOL_HIST + tile, POOL_WIDTH), _F32)],
        compiler_params=_params(2),
        name="pool_prompt",
    )(x, mod, *arrays)


def _pool_sample_kernel(x_ref, mod_ref, g_ref, wu_ref, wz_ref, wg_ref, ps_ref, wout_ref, fin_ref, prev_ref,
                        xo_ref, npool_ref, up_ref, *, nseq, seq, start_pos, apply_final):
    rows = nseq * seq
    hist = POOL_HIST
    x3 = x_ref[...]
    hn = _prenorm(x3, g_ref[...], mod_ref[:, 0:1, :], mod_ref[:, 1:2, :]).reshape(rows, D_MODEL).astype(_BF16)
    up_ref[:, hist - POOL_STATE:hist, :] = prev_ref[...]
    up_ref[:, hist:hist + seq, :] = _dot(hn, wu_ref[...]).reshape(nseq, seq, POOL_WIDTH)
    z = _dot(hn, wz_ref[...])
    pos3 = start_pos + lax.broadcasted_iota(jnp.int32, (nseq, seq, POOL_GROUP_DIM), 1)
    pos = pos3.astype(_F32).reshape(rows, POOL_GROUP_DIM)
    mixed = _pool_mix(lambda k, sl: up_ref[:, hist - k:hist - k + seq, sl].reshape(rows, POOL_GROUP_DIM),
                      z, wg_ref, ps_ref, pos)
    out = _dot(mixed, wout_ref[...]).reshape(nseq, seq, D_MODEL)
    xo_ref[...] = _residual(x3, out, mod_ref[:, 2:3, :], fin_ref, apply_final)
    npool_ref[...] = up_ref[:, hist + seq - POOL_STATE:hist + seq, :]


def _pool_sample_layer(x, mod, w, i, state_pool, start_pos, apply_final):
    b, seq, _ = x.shape
    j = i // N_MIXERS
    nseq = POOL_SEQ_BLOCK
    kern = functools.partial(_pool_sample_kernel, nseq=nseq, seq=seq, start_pos=start_pos,
                             apply_final=apply_final)
    arrays, specs = _pick(_operands(w, i), _POOL_OPS)
    return pl.pallas_call(
        kern,
        grid=(b // nseq,),
        in_specs=[pl.BlockSpec((nseq, seq, D_MODEL), lambda s: (s, 0, 0)),
                  pl.BlockSpec((nseq, 3, D_MODEL), lambda s: (s, 0, 0))] + specs
                 + [pl.BlockSpec((None, nseq, POOL_STATE, POOL_WIDTH), lambda s: (j, s, 0, 0))],
        out_specs=[pl.BlockSpec((nseq, seq, D_MODEL), lambda s: (s, 0, 0)),
                   pl.BlockSpec((nseq, POOL_STATE, POOL_WIDTH), lambda s: (s, 0, 0))],
        out_shape=[jax.ShapeDtypeStruct((b, seq, D_MODEL), _F32),
                   jax.ShapeDtypeStruct((b, POOL_STATE, POOL_WIDTH), _F32)],
        scratch_shapes=[pltpu.VMEM((nseq, POOL_HIST + seq, POOL_WIDTH), _F32)],
        compiler_params=_params(1),
        name="pool_sample",
    )(x, mod, *arrays, state_pool)


def _tm_proj_kernel(*refs, has_dt):
    if has_dt:
        x_ref, mod_ref, g_ref, w_ref, wdt_ref, dtb_ref, o_ref, dt_ref, hn_ref = refs
    else:
        x_ref, mod_ref, g_ref, w_ref, o_ref, hn_ref = refs

    @pl.when(pl.program_id(0) == 0)
    def _():
        hn = _prenorm(x_ref[...], g_ref[...], mod_ref[0:1], mod_ref[1:2])
        hn_ref[...] = hn.reshape(hn_ref.shape).astype(_BF16)
        if has_dt:
            dt_ref[...] = _softplus(_dot(hn_ref[...], wdt_ref[...]) + dtb_ref[...])

    o_ref[...] = _dot(hn_ref[...], w_ref[...])


def _tm_proj(x, mod, w, i, weight, n_cols, has_dt):
    seq, b, _ = x.shape
    rows = seq * b
    j = i // N_MIXERS
    ops = _operands(w, i)
    arrays = [x, mod, ops["norm_g"][0], weight]
    specs = [pl.BlockSpec((seq, b, D_MODEL), lambda n: (0, 0, 0)),
             pl.BlockSpec((3, b, D_MODEL), lambda n: (0, 0, 0)),
             ops["norm_g"][1],
             pl.BlockSpec((None, D_MODEL, PROJ_COLS), lambda n: (j, 0, n))]
    out_shape = [jax.ShapeDtypeStruct((rows, n_cols), _F32)]
    out_specs = [pl.BlockSpec((rows, PROJ_COLS), lambda n: (0, n))]
    if has_dt:
        extra, extra_specs = _pick(ops, ("w_dt", "dt_bias"))
        arrays += extra
        specs += extra_specs
        out_shape.append(jax.ShapeDtypeStruct((rows, LANES), _F32))
        out_specs.append(pl.BlockSpec((rows, LANES), lambda n: (0, 0)))
    return pl.pallas_call(
        functools.partial(_tm_proj_kernel, has_dt=has_dt),
        grid=(n_cols // PROJ_COLS,),
        in_specs=specs,
        out_specs=out_specs,
        out_shape=out_shape,
        scratch_shapes=[pltpu.VMEM((rows, D_MODEL), _BF16)],
        compiler_params=_params(1),
        name="sample_proj",
    )(*arrays)


def _tm_ssd_kernel(x_ref, mod_ref, z_ref, xin_ref, bcin_ref, dt_ref,
                   cw_ref, cb_ref, alog_ref, dsk_ref, ng_ref, wout_ref, fin_ref,
                   cprev_ref, hprev_ref, exps_ref, exp64_ref, _conv_alias_ref, _ssm_alias_ref,
                   xo_ref, nconv_ref, nssm_ref,
                   xp_ref, act_ref, mfac_ref, e_ref, xw_ref, y_ref, *, nseq, seq, apply_final):
    rows = nseq * seq
    n_carry = CONV_WIDTH - 1
    xp_ref[0:n_carry] = cprev_ref[...]
    xp_ref[n_carry:n_carry + seq, :, 0:D_INNER] = xin_ref[...]
    xp_ref[n_carry:n_carry + seq, :, D_INNER:CONV_DIM] = bcin_ref[...]
    conv = cb_ref[...]
    for k in range(CONV_WIDTH):
        conv = conv + xp_ref[k:k + seq] * cw_ref[k:k + 1, :]
    act = _silu(conv).reshape(rows, CONV_DIM)
    for jb in range(CONV_DIM // LANES):
        act_ref[jb] = act[:, jb * LANES:(jb + 1) * LANES]
    nconv_ref[...] = xp_ref[seq:seq + n_carry]

    dt3 = dt_ref[...]
    a3 = dt3 * (-jnp.exp(alog_ref[...]))
    slabs = [a3[0]]
    for t in range(1, seq):
        slabs.append(slabs[-1] + a3[t])
    acum3 = jnp.stack(slabs, axis=0)
    w3 = dt3 * jnp.exp(acum3[seq - 1:seq] - acum3)
    acum = acum3.reshape(rows, LANES)

    shape3 = (seq, nseq, N_GROUPS * LANES)
    z3 = _select_lanes(acum, exps_ref[...], 3).reshape(shape3)
    dtz3 = _select_lanes(dt3.reshape(rows, LANES), exps_ref[...], 2).reshape(shape3)
    t3 = lax.broadcasted_iota(jnp.int32, shape3, 0)
    s3 = lax.broadcasted_iota(jnp.int32, shape3, 2) % seq
    diag = t3 == s3
    acum_s = jnp.sum(jnp.where(diag, z3, 0.0), axis=0, keepdims=True)
    dt_s = jnp.sum(jnp.where(diag, dtz3, 0.0), axis=0, keepdims=True)
    mfac = (jnp.exp(jnp.where(s3 <= t3, z3 - acum_s, NEG_BIG)) * dt_s).reshape(rows, N_GROUPS * LANES)
    for g in range(N_GROUPS):
        mfac_ref[g] = mfac[:, g * LANES:(g + 1) * LANES]
    e = _select_lanes(jnp.exp(acum), exp64_ref[...], 2)
    wx = _select_lanes(w3.reshape(rows, LANES), exp64_ref[...], 2)
    for jb in range(D_INNER // LANES):
        lanes = slice(jb * LANES, (jb + 1) * LANES)
        e_ref[jb] = e[:, lanes]
        xw_ref[jb] = act[:, lanes] * wx[:, lanes]

    width = HEADS_PER_GROUP * seq
    bd_r = lax.broadcasted_iota(jnp.int32, (width, GROUP_WIDTH), 0)
    bd_c = lax.broadcasted_iota(jnp.int32, (width, GROUP_WIDTH), 1)
    blockdiag = (bd_r // seq) == (bd_c // HEAD_DIM)
    t_row = lax.broadcasted_iota(jnp.int32, (seq, GROUP_WIDTH), 0)
    zeros_b = jnp.zeros((seq, D_STATE), _F32)
    ones_b = jnp.ones((seq, D_STATE), _F32)
    b_blk = D_INNER // LANES
    c_blk = b_blk + N_GROUPS

    def per_seq(s, carry):
        tokens = pl.ds(s, seq, stride=nseq)

        def pair(ref, g):
            return jnp.concatenate([ref[2 * g, tokens, :], ref[2 * g + 1, tokens, :]], axis=1)

        for g in range(N_GROUPS):
            b_g = act_ref[b_blk + g, tokens, :]
            c_bf = act_ref[c_blk + g, tokens, :].astype(_BF16)
            sl = slice(g * GROUP_WIDTH, (g + 1) * GROUP_WIDTH)
            xs_g = pair(act_ref, g)
            cbx = _dot_nt(c_bf, jnp.concatenate([b_g] * HEADS_PER_GROUP, axis=0).astype(_BF16))
            mp = (cbx * mfac_ref[g, tokens, :][:, 0:width]).astype(_BF16)
            rhs = jnp.where(blockdiag, jnp.concatenate([xs_g] * HEADS_PER_GROUP, axis=0), 0.0).astype(_BF16)
            h0 = hprev_ref[s, sl, :]
            e_g = pair(e_ref, g)
            y_g = _dot(mp, rhs) + _dot_nt(c_bf, h0.astype(_BF16)) * e_g + dsk_ref[:, sl] * xs_g
            y_ref[2 * g, tokens, :] = y_g[:, 0:LANES]
            y_ref[2 * g + 1, tokens, :] = y_g[:, LANES:GROUP_WIDTH]
            e_hi = e_g.astype(_BF16).astype(_F32)
            ez = jnp.where(t_row == seq - 1, e_hi, jnp.where(t_row == seq - 2, e_g - e_hi, 0.0))
            lhs_t = jnp.concatenate([pair(xw_ref, g), ez], axis=0).astype(_BF16)
            rhs_s = jnp.concatenate([jnp.concatenate([b_g, zeros_b], axis=1),
                                     jnp.concatenate([zeros_b, ones_b], axis=1)], axis=0).astype(_BF16)
            out = _dot_tn(lhs_t, rhs_s)
            nssm_ref[s, sl, :] = h0 * out[:, D_STATE:2 * D_STATE] + out[:, 0:D_STATE]
        return carry

    lax.fori_loop(0, nseq, per_seq, 0)

    z = z_ref[...].reshape(rows, D_INNER)
    parts = []
    for g in range(N_GROUPS):
        sl = slice(g * GROUP_WIDTH, (g + 1) * GROUP_WIDTH)
        gated = jnp.concatenate([y_ref[2 * g], y_ref[2 * g + 1]], axis=1) * _silu(z[:, sl])
        ms = jnp.mean(gated * gated, axis=-1, keepdims=True)
        parts.append((gated * lax.rsqrt(ms + EPS) * ng_ref[:, sl]).astype(_BF16))
    out = _dot(jnp.concatenate(parts, axis=1), wout_ref[...]).reshape(seq, nseq, D_MODEL)
    xo_ref[...] = _residual(x_ref[...], out, mod_ref[2:3], fin_ref, apply_final)


def _tm_ssd_layer(x, mod, w, i, state_conv_t, state_ssm, conv_acc, ssm_acc, apply_final):
    seq, b, _ = x.shape
    j = i // N_MIXERS
    nseq = SSD_SEQ_BLOCK
    rows = nseq * seq
    n_layers = state_ssm.shape[0]
    proj, dt = _tm_proj(x, mod, w, i, w["ssd_w_in"], D_INNER + CONV_DIM, True)
    proj = proj.reshape(seq, b, D_INNER + CONV_DIM)
    dt = dt.reshape(seq, b, LANES)
    kern = functools.partial(_tm_ssd_kernel, nseq=nseq, seq=seq, apply_final=apply_final)
    arrays, specs = _pick(_operands(w, i), _SSD_SAMPLE_OPS)
    exps, exp64 = _expansion_constants(seq)
    col_block = lambda n: pl.BlockSpec((seq, nseq, WEIGHT_COLS), lambda s: (0, s, n))
    in_specs = ([pl.BlockSpec((seq, nseq, D_MODEL), lambda s: (0, s, 0)),
                 pl.BlockSpec((3, nseq, D_MODEL), lambda s: (0, s, 0)),
                 col_block(0), col_block(1), col_block(2),
                 pl.BlockSpec((seq, nseq, LANES), lambda s: (0, s, 0))]
                + specs
                + [pl.BlockSpec((None, CONV_WIDTH - 1, nseq, CONV_DIM), lambda s: (j, 0, s, 0)),
                   pl.BlockSpec((None, nseq, D_INNER, D_STATE), lambda s: (j, s, 0, 0)),
                   _wspec(exps.shape, (0, 0)), _wspec(exp64.shape, (0, 0))])
    args = [x, mod, proj, proj, proj, dt, *arrays, state_conv_t, state_ssm, exps, exp64]
    n_in = len(args)
    aliases = {}
    if conv_acc is not None:
        in_specs += [pl.BlockSpec(memory_space=pl.ANY), pl.BlockSpec(memory_space=pl.ANY)]
        args += [conv_acc, ssm_acc]
        aliases = {n_in: 1, n_in + 1: 2}
        kern_fn = kern
    else:
        kern_fn = lambda *refs: kern(*refs[:n_in], None, None, *refs[n_in:])
    slab = lambda width: pltpu.VMEM((width // LANES, rows, LANES), _F32)
    return pl.pallas_call(
        kern_fn,
        grid=(b // nseq,),
        in_specs=in_specs,
        out_specs=[pl.BlockSpec((seq, nseq, D_MODEL), lambda s: (0, s, 0)),
                   pl.BlockSpec((None, CONV_WIDTH - 1, nseq, CONV_DIM), lambda s: (j, 0, s, 0)),
                   pl.BlockSpec((None, nseq, D_INNER, D_STATE), lambda s: (j, s, 0, 0))],
        out_shape=[jax.ShapeDtypeStruct((seq, b, D_MODEL), _F32),
                   jax.ShapeDtypeStruct((n_layers, CONV_WIDTH - 1, b, CONV_DIM), _F32),
                   jax.ShapeDtypeStruct((n_layers, b, D_INNER, D_STATE), _F32)],
        scratch_shapes=[pltpu.VMEM((seq + CONV_WIDTH - 1, nseq, CONV_DIM), _F32),
                        slab(CONV_DIM),
                        slab(N_GROUPS * LANES),
                        slab(D_INNER),
                        slab(D_INNER),
                        slab(D_INNER)],
        input_output_aliases=aliases,
        compiler_params=_params(1),
        name="ssd_sample",
    )(*args)


def _tm_pool_kernel(x_ref, mod_ref, g_ref, wu_ref, wz_ref, wg_ref, ps_ref, wout_ref, fin_ref, prev_ref,
                    _alias_ref, xo_ref, npool_ref, up_ref, *, nseq, seq, start_pos, apply_final):
    rows = nseq * seq
    x3 = x_ref[...]
    hn = _prenorm(x3, g_ref[...], mod_ref[0:1], mod_ref[1:2]).reshape(rows, D_MODEL).astype(_BF16)
    up_ref[0:POOL_STATE] = prev_ref[...]
    up_ref[POOL_STATE:POOL_STATE + seq] = _dot(hn, wu_ref[...]).reshape(seq, nseq, POOL_WIDTH)
    z = _dot(hn, wz_ref[...])
    pos3 = start_pos + lax.broadcasted_iota(jnp.int32, (seq, nseq, POOL_GROUP_DIM), 0)
    pos = pos3.astype(_F32).reshape(rows, POOL_GROUP_DIM)
    mixed = _pool_mix(
        lambda k, sl: up_ref[POOL_STATE - k:POOL_STATE - k + seq, :, sl].reshape(rows, POOL_GROUP_DIM),
        z, wg_ref, ps_ref, pos)
    out = _dot(mixed, wout_ref[...]).reshape(seq, nseq, D_MODEL)
    xo_ref[...] = _residual(x3, out, mod_ref[2:3], fin_ref, apply_final)
    npool_ref[...] = up_ref[seq:seq + POOL_STATE]


def _tm_pool_layer(x, mod, w, i, state_pool_t, pool_acc, start_pos, apply_final):
    seq, b, _ = x.shape
    j = i // N_MIXERS
    nseq = POOL_SEQ_BLOCK
    n_layers = state_pool_t.shape[0]
    kern = functools.partial(_tm_pool_kernel, nseq=nseq, seq=seq, start_pos=start_pos,
                             apply_final=apply_final)
    arrays, specs = _pick(_operands(w, i), _POOL_OPS)
    state_spec = pl.BlockSpec((None, POOL_STATE, nseq, POOL_WIDTH), lambda s: (j, 0, s, 0))
    in_specs = ([pl.BlockSpec((seq, nseq, D_MODEL), lambda s: (0, s, 0)),
                 pl.BlockSpec((3, nseq, D_MODEL), lambda s: (0, s, 0))] + specs + [state_spec])
    args = [x, mod, *arrays, state_pool_t]
    n_in = len(args)
    aliases = {}
    if pool_acc is not None:
        in_specs.append(pl.BlockSpec(memory_space=pl.ANY))
        args.append(pool_acc)
        aliases = {n_in: 1}
        kern_fn = kern
    else:
        kern_fn = lambda *refs: kern(*refs[:n_in], None, *refs[n_in:])
    return pl.pallas_call(
        kern_fn,
        grid=(b // nseq,),
        in_specs=in_specs,
        out_specs=[pl.BlockSpec((seq, nseq, D_MODEL), lambda s: (0, s, 0)), state_spec],
        out_shape=[jax.ShapeDtypeStruct((seq, b, D_MODEL), _F32),
                   jax.ShapeDtypeStruct((n_layers, POOL_STATE, b, POOL_WIDTH), _F32)],
        scratch_shapes=[pltpu.VMEM((POOL_STATE + seq, nseq, POOL_WIDTH), _F32)],
        input_output_aliases=aliases,
        compiler_params=_params(1),
        name="pool_sample",
    )(*args)


def _prepare_weights(norm_g, ssd_w_in, ssd_conv_w, ssd_conv_b, ssd_dt_bias, ssd_a_log, ssd_d, ssd_norm_g,
                     ssd_w_out, pool_w_in, pool_w_group, pool_scale, pool_w_out, final_norm_g):
    n_ssd = ssd_w_in.shape[0]
    n_pool = pool_w_in.shape[0]
    pad = LANES - N_HEADS
    return {
        "norm_g": norm_g.reshape(DEPTH, 1, D_MODEL),
        "final_g": final_norm_g.reshape(1, D_MODEL),
        "ssd_w_in": ssd_w_in.astype(_BF16),
        "ssd_w_dt": jnp.pad(ssd_w_in[:, :, D_INNER + CONV_DIM:], ((0, 0), (0, 0), (0, pad))).astype(_BF16),
        "ssd_conv_w": ssd_conv_w,
        "ssd_conv_b": ssd_conv_b.reshape(n_ssd, 1, CONV_DIM),
        "ssd_dt_bias": jnp.pad(ssd_dt_bias, ((0, 0), (0, pad))).reshape(n_ssd, 1, LANES),
        "ssd_a_log": jnp.pad(ssd_a_log, ((0, 0), (0, pad))).reshape(n_ssd, 1, LANES),
        "ssd_d": jnp.repeat(ssd_d, HEAD_DIM, axis=1).reshape(n_ssd, 1, D_INNER),
        "ssd_norm_g": ssd_norm_g.reshape(n_ssd, 1, D_INNER),
        "ssd_w_out": ssd_w_out.astype(_BF16),
        "pool_w_in": pool_w_in.astype(_BF16),
        "pool_w_group": pool_w_group.astype(_BF16),
        "pool_scale": pool_scale.reshape(n_pool, 1, POOL_WIDTH),
        "pool_w_out": pool_w_out.astype(_BF16),
    }


def kernel(x_prompt, x_sample, state_ssm, state_conv, state_pool, c_prompt, c_sample, ada_w, ada_b, norm_g,
           ssd_w_in, ssd_conv_w, ssd_conv_b, ssd_dt_bias, ssd_a_log, ssd_d, ssd_norm_g, ssd_w_out, pool_w_in,
           pool_w_group, pool_scale, pool_w_out, final_norm_g):
    b_p = x_prompt.shape[0]
    b_s = x_sample.shape[0]
    n_ssd = state_ssm.shape[0]
    w = _prepare_weights(norm_g, ssd_w_in, ssd_conv_w, ssd_conv_b, ssd_dt_bias, ssd_a_log, ssd_d, ssd_norm_g,
                         ssd_w_out, pool_w_in, pool_w_group, pool_scale, pool_w_out, final_norm_g)
    mod = _modulation(jnp.concatenate([c_prompt, c_sample], axis=0), ada_w, ada_b)
    mod_p = mod[:, :b_p].reshape(DEPTH, b_p, 3, D_MODEL)
    mod_s = mod[:, b_p:].reshape(DEPTH, b_s, 3, D_MODEL).transpose(0, 2, 1, 3)
    ssm_in = state_ssm.reshape(n_ssd, b_s, D_INNER, D_STATE)
    conv_in = state_conv.transpose(0, 2, 1, 3)
    pool_in = state_pool.transpose(0, 2, 1, 3)

    xp, xs = x_prompt, x_sample.transpose(1, 0, 2)
    ssm_p, conv_p, pool_p = [], [], []
    ssm_s = conv_s = pool_s = None
    for i in range(DEPTH):
        last = i == DEPTH - 1
        if i % N_MIXERS == 0:
            xp, cv, st = _ssd_prompt_layer(xp, mod_p[i], w, i, last)
            conv_p.append(cv)
            ssm_p.append(st)
            xs, conv_s, ssm_s = _tm_ssd_layer(xs, mod_s[i], w, i, conv_in, ssm_in, conv_s, ssm_s, last)
        else:
            xp, ps = _pool_prompt_layer(xp, mod_p[i], w, i, last)
            pool_p.append(ps)
            xs, pool_s = _tm_pool_layer(xs, mod_s[i], w, i, pool_in, pool_s, PAST_LEN, last)

    head_shape = (N_HEADS, HEAD_DIM, D_STATE)
    return (xp, xs.transpose(1, 0, 2),
            jnp.stack(ssm_p).reshape(n_ssd, b_p, *head_shape),
            jnp.stack(conv_p), jnp.stack(pool_p),
            ssm_s.reshape(n_ssd, b_s, *head_shape),
            conv_s.transpose(0, 2, 1, 3), pool_s.transpose(0, 2, 1, 3))
```

```python
import functools

import numpy as np
import jax
import jax.numpy as jnp
from jax import lax
from jax.experimental import pallas as pl
from jax.experimental.pallas import tpu as pltpu

D_MODEL = 1024
DEPTH = 4
PAST_LEN = 16384
N_MIXERS = 2
D_INNER = 2048
HEAD_DIM = 64
N_HEADS = 32
N_GROUPS = 8
HEADS_PER_GROUP = 4
D_STATE = 128
CONV_WIDTH = 4
CONV_DIM = D_INNER + 2 * N_GROUPS * D_STATE
GROUP_WIDTH = HEADS_PER_GROUP * HEAD_DIM
POOL_WIDTH = 2048
POOL_WINDOWS = (2, 4, 8, 16)
POOL_GROUP_DIM = 512
POOL_STATE = 15
EPS = 1e-6

LANES = 128
SUBLANES = 8
CHUNK = 128
N_PHASES = CHUNK // SUBLANES
PROMPT_TILE = 512
SSD_PROMPT_TILE = 512
SSD_SEQ_BLOCK = 8
SSD_SEQ_UNROLL = 4
POOL_SEQ_BLOCK = 16
POOL_HIST = 2 * SUBLANES
WEIGHT_COLS = 2048
PIECE_COLS = 256
PROJ_COLS = 1024
VMEM_LIMIT = 60 * 1024 * 1024
NEG_BIG = -1e30
LOG2E = 1.4426950408889634

_F32 = jnp.float32
_BF16 = jnp.bfloat16
_HI = lax.Precision.HIGHEST


def _silu(v):
    h = 0.5 * v
    return h + h * jnp.tanh(h)


def _softplus(v):
    return jnp.maximum(v, 0.0) + jnp.log(1.0 + jnp.exp(-jnp.abs(v)))


def _dot(a, b):
    return jnp.dot(a, b, preferred_element_type=_F32)


def _dot_exact(a, b):
    return jnp.dot(a, b, precision=_HI, preferred_element_type=_F32)


def _dot_nt(a, b):
    return lax.dot_general(a, b, (((1,), (1,)), ((), ())), preferred_element_type=_F32)


def _dot_tn(a, b):
    return lax.dot_general(a, b, (((0,), (0,)), ((), ())), preferred_element_type=_F32)


def _select_lanes(v, onehot_bf, terms):
    acc = None
    rest = v
    for _ in range(terms):
        piece = rest.astype(_BF16)
        part = _dot(piece, onehot_bf)
        acc = part if acc is None else acc + part
        rest = rest - piece.astype(_F32)
    return acc


def _prenorm(x, g, shift, scale):
    ms = jnp.mean(x * x, axis=-1, keepdims=True)
    return x * lax.rsqrt(ms + EPS) * g * (1.0 + scale) + shift


def _residual(x, out, gate, fin_ref, apply_final):
    y = x + (1.0 + gate) * out
    if apply_final:
        ms = jnp.mean(y * y, axis=-1, keepdims=True)
        y = y * lax.rsqrt(ms + EPS) * fin_ref[...]
    return y


def _group_rmsnorm_gate(y_ref, z_ref, ng_ref):
    parts = []
    for g in range(N_GROUPS):
        sl = slice(g * GROUP_WIDTH, (g + 1) * GROUP_WIDTH)
        gated = y_ref[:, sl] * _silu(z_ref[:, sl])
        ms = jnp.mean(gated * gated, axis=-1, keepdims=True)
        parts.append((gated * lax.rsqrt(ms + EPS) * ng_ref[:, sl]).astype(_BF16))
    return jnp.concatenate(parts, axis=1)


def _wspec(block_shape, index):
    return pl.BlockSpec(block_shape, lambda *_: index, pipeline_mode=pl.Buffered(1))


def _params(n_axes):
    return pltpu.CompilerParams(dimension_semantics=("arbitrary",) * n_axes, vmem_limit_bytes=VMEM_LIMIT)


def _operands(w, i):
    j = i // N_MIXERS
    wcols = (None, D_MODEL, WEIGHT_COLS)
    ops = {"norm_g": (w["norm_g"], _wspec((None, 1, D_MODEL), (i, 0, 0))),
           "final_g": (w["final_g"], _wspec((1, D_MODEL), (0, 0)))}
    if i % N_MIXERS == 0:
        ops.update(
            w_z=(w["ssd_w_in"], _wspec(wcols, (j, 0, 0))),
            w_x=(w["ssd_w_in"], _wspec(wcols, (j, 0, 1))),
            w_bc=(w["ssd_w_in"], _wspec(wcols, (j, 0, 2))),
            w_dt=(w["ssd_w_dt"], _wspec((None, D_MODEL, LANES), (j, 0, 0))),
            conv_w=(w["ssd_conv_w"], _wspec((None, CONV_WIDTH, CONV_DIM), (j, 0, 0))),
            conv_b=(w["ssd_conv_b"], _wspec((None, 1, CONV_DIM), (j, 0, 0))),
            dt_bias=(w["ssd_dt_bias"], _wspec((None, 1, LANES), (j, 0, 0))),
            a_log=(w["ssd_a_log"], _wspec((None, 1, LANES), (j, 0, 0))),
            d_skip=(w["ssd_d"], _wspec((None, 1, D_INNER), (j, 0, 0))),
            ssd_norm_g=(w["ssd_norm_g"], _wspec((None, 1, D_INNER), (j, 0, 0))),
            w_out=(w["ssd_w_out"], _wspec((None, D_INNER, D_MODEL), (j, 0, 0))),
        )
    else:
        ops.update(
            w_u=(w["pool_w_in"], _wspec(wcols, (j, 0, 0))),
            w_zp=(w["pool_w_in"], _wspec(wcols, (j, 0, 1))),
            w_group=(w["pool_w_group"],
                     _wspec((None, len(POOL_WINDOWS), POOL_GROUP_DIM, POOL_GROUP_DIM), (j, 0, 0, 0))),
            pool_scale=(w["pool_scale"], _wspec((None, 1, POOL_WIDTH), (j, 0, 0))),
            w_out=(w["pool_w_out"], _wspec((None, POOL_WIDTH, D_MODEL), (j, 0, 0))),
        )
    return ops


def _pick(ops, names):
    return [ops[n][0] for n in names], [ops[n][1] for n in names]


def _mod_kernel(c_ref, w_ref, b_ref, o_ref):
    sc = _silu(c_ref[...]).astype(_BF16)
    o_ref[...] = _dot(sc, w_ref[...].astype(_BF16)) + b_ref[...]


def _modulation(c_all, ada_w, ada_b):
    n = c_all.shape[0]
    tn = 1024
    return pl.pallas_call(
        _mod_kernel,
        grid=(DEPTH, 3 * D_MODEL // tn),
        in_specs=[
            pl.BlockSpec((n, D_MODEL), lambda i, j: (0, 0)),
            pl.BlockSpec((None, D_MODEL, tn), lambda i, j: (i, 0, j)),
            pl.BlockSpec((None, 1, tn), lambda i, j: (i, 0, j)),
        ],
        out_specs=pl.BlockSpec((None, n, tn), lambda i, j: (i, 0, j)),
        out_shape=jax.ShapeDtypeStruct((DEPTH, n, 3 * D_MODEL), _F32),
        compiler_params=_params(2),
        name="adaln_mod",
    )(c_all, ada_w, ada_b.reshape(DEPTH, 1, 3 * D_MODEL))


_SSD_PROMPT_OPS = ("norm_g", "w_z", "w_x", "w_bc", "w_dt", "conv_w", "conv_b", "dt_bias", "a_log", "d_skip",
                   "ssd_norm_g", "w_out", "final_g")


def _ssd_prompt_kernel(*refs, tile, n_tiles, apply_final):
    n_xcols = D_MODEL // LANES
    x_refs = refs[:n_xcols]
    (mod_ref, g_ref, wz_ref, wx_ref, wbc_ref, wdt_ref, cw_ref, cb_ref, dtb_ref, alog_ref, dsk_ref, ng_ref,
     wout_ref, fin_ref, xo_ref, nconv_ref, nssm_ref,
     ht_ref, carry_ref, act_ref, z_ref, y_ref, stage_ref) = refs[n_xcols:]
    l = pl.program_id(1)
    n_chunks = tile // CHUNK
    n_carry = CONV_WIDTH - 1

    @pl.when(l == 0)
    def _():
        ht_ref[...] = jnp.zeros_like(ht_ref)
        carry_ref[...] = jnp.zeros_like(carry_ref)

    def load_x():
        return jnp.concatenate(
            [jnp.concatenate([xr[pl.ds(c * CHUNK + r, SUBLANES, stride=N_PHASES), :] for xr in x_refs], axis=1)
             for c in range(n_chunks) for r in range(N_PHASES)], axis=0)

    hn = _prenorm(load_x(), g_ref[...], mod_ref[0:1, :], mod_ref[1:2, :]).astype(_BF16)
    dt = _softplus(_dot(hn, wdt_ref[...]) + dtb_ref[...])

    last = (N_PHASES - n_carry) * SUBLANES
    for k in range(CONV_DIM // PIECE_COLS):
        cols = slice(k * PIECE_COLS, (k + 1) * PIECE_COLS)
        half = D_INNER // PIECE_COLS
        w_ref, kw = (wx_ref, k) if k < half else (wbc_ref, k - half)
        xb = _dot(hn, w_ref[:, kw * PIECE_COLS:(kw + 1) * PIECE_COLS])
        prev = carry_ref[:, cols]
        for c in range(n_chunks):
            cur = xb[c * CHUNK:(c + 1) * CHUNK, :]
            tail = cur[last:CHUNK, :]
            wrapped = [jnp.concatenate([prev[j * SUBLANES + SUBLANES - 1:(j + 1) * SUBLANES, :],
                                        tail[j * SUBLANES:(j + 1) * SUBLANES - 1, :]], axis=0)
                       for j in range(n_carry)]
            ext = jnp.concatenate(wrapped + [cur], axis=0)
            conv = cb_ref[:, cols]
            for kk in range(CONV_WIDTH):
                conv = conv + ext[kk * SUBLANES:kk * SUBLANES + CHUNK, :] * cw_ref[kk:kk + 1, cols]
            act_ref[c * CHUNK:(c + 1) * CHUNK, cols] = _silu(conv)
            prev = tail
        carry_ref[:, cols] = prev
        for j in range(n_carry):
            nconv_ref[j:j + 1, cols] = prev[(j + 1) * SUBLANES - 1:(j + 1) * SUBLANES, :]

    row = lax.broadcasted_iota(jnp.int32, (CHUNK, CHUNK), 0)
    col = lax.broadcasted_iota(jnp.int32, (CHUNK, CHUNK), 1)
    token = lambda p: (p % SUBLANES) * N_PHASES + p // SUBLANES
    causal = token(col) <= token(row)
    tril = causal.astype(_F32)
    head_of_lane = lax.broadcasted_iota(jnp.int32, (CHUNK, GROUP_WIDTH), 1) // HEAD_DIM
    low_half = lax.broadcasted_iota(jnp.int32, (CHUNK, LANES), 1) < HEAD_DIM
    neg_a = -jnp.exp(alog_ref[...])

    def decay_stage(c):
        dt_c = dt[c * CHUNK:(c + 1) * CHUNK, :]
        acum = _dot_exact(tril, dt_c * neg_a)
        acum2 = acum * LOG2E
        w_c = dt_c * jnp.exp(acum[CHUNK - 1:CHUNK, :] - acum)
        srow_t = (jnp.log(dt_c) * LOG2E - acum2).T
        return acum2, srow_t, w_c.T

    def cb_stage(c, g):
        rows = slice(c * CHUNK, (c + 1) * CHUNK)
        b_g = act_ref[rows, D_INNER + g * D_STATE:D_INNER + (g + 1) * D_STATE]
        c_lo = D_INNER + (N_GROUPS + g) * D_STATE
        c_bf = act_ref[rows, c_lo:c_lo + D_STATE].astype(_BF16)
        bt_g = b_g.T
        return c_bf, bt_g, _dot(c_bf, bt_g.astype(_BF16))

    def operand_stage(c, g, decay, cbs):
        acum2, srow_t, w_t = decay
        _, bt_g, cb = cbs
        xs_g = act_ref[c * CHUNK:(c + 1) * CHUNK, g * GROUP_WIDTH:(g + 1) * GROUP_WIDTH]
        rhs = jnp.concatenate(
            [jnp.where(head_of_lane == hh, xs_g, 0.0).astype(_BF16) for hh in range(HEADS_PER_GROUP)],
            axis=0)
        m_parts, w_parts, e_cols = [], [], []
        for hh in range(HEADS_PER_GROUP):
            h = g * HEADS_PER_GROUP + hh
            acum_bc = jnp.broadcast_to(acum2[:, h:h + 1], (CHUNK, CHUNK))
            seg = acum_bc + srow_t[h:h + 1, :]
            m_parts.append((cb * jnp.exp2(jnp.where(causal, seg, NEG_BIG))).astype(_BF16))
            w_parts.append((bt_g * w_t[h:h + 1, :]).astype(_BF16))
            e_cols.append(jnp.exp2(acum_bc))
        lhs = jnp.concatenate(
            [jnp.concatenate(m_parts, axis=1), jnp.concatenate(w_parts, axis=1)], axis=0)
        e_g = jnp.concatenate(
            [jnp.where(low_half, e_cols[0], e_cols[1]), jnp.where(low_half, e_cols[2], e_cols[3])],
            axis=1)
        return lhs, rhs, e_g

    def output_stage(c, g, cbs, e_g, out):
        rows = slice(c * CHUNK, (c + 1) * CHUNK)
        sl = slice(g * GROUP_WIDTH, (g + 1) * GROUP_WIDTH)
        ht_g = ht_ref[g]
        y_inter = _dot(cbs[0], ht_g.astype(_BF16))
        y_ref[rows, sl] = out[0:CHUNK, :] + y_inter * e_g + dsk_ref[:, sl] * act_ref[rows, sl]
        ht_ref[g] = ht_g * e_g[CHUNK - 1:CHUNK, :] + out[CHUNK:2 * CHUNK, :]

    def gate_proj_piece(p):
        if p < D_INNER // PIECE_COLS:
            zc = slice(p * PIECE_COLS, (p + 1) * PIECE_COLS)
            z_ref[:, zc] = _dot(hn, wz_ref[:, zc])

    groups = range(N_GROUPS)
    decays = [decay_stage(c) for c in range(n_chunks)]
    cbs = [[cb_stage(c, g) for g in groups] for c in range(n_chunks)]
    operands = [operand_stage(0, g, decays[0], cbs[0][g]) for g in groups]
    for c in range(n_chunks):
        outs, nxt = [], []
        for g in groups:
            lhs, rhs, _ = operands[g]
            outs.append(_dot(lhs, rhs))
            if c + 1 < n_chunks:
                nxt.append(operand_stage(c + 1, g, decays[c + 1], cbs[c + 1][g]))
        for g in groups:
            output_stage(c, g, cbs[c][g], operands[g][2], outs[g])
            gate_proj_piece(c * N_GROUPS + g)
        operands = nxt

    gn = _group_rmsnorm_gate(y_ref, z_ref, ng_ref)
    xo = _residual(load_x(), _dot(gn, wout_ref[...]), mod_ref[2:3, :], fin_ref, apply_final)
    for c in range(n_chunks):
        for r in range(N_PHASES):
            p0 = c * CHUNK + r * SUBLANES
            for j in range(n_xcols):
                stage_ref[j, pl.ds(c * CHUNK + r, SUBLANES, stride=N_PHASES), :] = (
                    xo[p0:p0 + SUBLANES, j * LANES:(j + 1) * LANES])
    for j in range(n_xcols):
        xo_ref[:, j * LANES:(j + 1) * LANES] = stage_ref[j]

    @pl.when(l == n_tiles - 1)
    def _():
        for g in range(N_GROUPS):
            nssm_ref[g * GROUP_WIDTH:(g + 1) * GROUP_WIDTH, :] = ht_ref[g].T


def _ssd_prompt_layer(x, mod, w, i, apply_final):
    b, L, _ = x.shape
    tile = SSD_PROMPT_TILE
    n_tiles = L // tile
    assert tile // CHUNK * N_GROUPS >= D_INNER // PIECE_COLS
    kern = functools.partial(_ssd_prompt_kernel, tile=tile, n_tiles=n_tiles, apply_final=apply_final)
    arrays, specs = _pick(_operands(w, i), _SSD_PROMPT_OPS)
    n_xcols = D_MODEL // LANES
    x_specs = [pl.BlockSpec((None, tile, LANES), lambda s, l, j=j: (s, l, j)) for j in range(n_xcols)]
    return pl.pallas_call(
        kern,
        grid=(b, n_tiles),
        in_specs=x_specs + [pl.BlockSpec((None, 3, D_MODEL), lambda s, l: (s, 0, 0))] + specs,
        out_specs=[pl.BlockSpec((None, tile, D_MODEL), lambda s, l: (s, l, 0)),
                   pl.BlockSpec((None, CONV_WIDTH - 1, CONV_DIM), lambda s, l: (s, 0, 0)),
                   pl.BlockSpec((None, D_INNER, D_STATE), lambda s, l: (s, 0, 0))],
        out_shape=[jax.ShapeDtypeStruct((b, L, D_MODEL), _F32),
                   jax.ShapeDtypeStruct((b, CONV_WIDTH - 1, CONV_DIM), _F32),
                   jax.ShapeDtypeStruct((b, D_INNER, D_STATE), _F32)],
        scratch_shapes=[pltpu.VMEM((N_GROUPS, D_STATE, GROUP_WIDTH), _F32),
                        pltpu.VMEM(((CONV_WIDTH - 1) * SUBLANES, CONV_DIM), _F32),
                        pltpu.VMEM((tile, CONV_DIM), _F32),
                        pltpu.VMEM((tile, D_INNER), _F32),
                        pltpu.VMEM((tile, D_INNER), _F32),
                        pltpu.VMEM((n_xcols, tile, LANES), _F32)],
        compiler_params=_params(2),
        name="ssd_prompt",
    )(*([x] * n_xcols), mod, *arrays)


_SSD_SAMPLE_OPS = ("conv_w", "conv_b", "a_log", "d_skip", "ssd_norm_g", "w_out", "final_g")


def _expansion_constants(seq):
    exps = np.zeros((LANES, N_GROUPS * LANES), np.float32)
    for h in range(N_HEADS):
        g, hh = divmod(h, HEADS_PER_GROUP)
        exps[h, g * LANES + hh * seq:g * LANES + (hh + 1) * seq] = 1.0
    exp64 = np.zeros((LANES, D_INNER), np.float32)
    for h in range(N_HEADS):
        exp64[h, h * HEAD_DIM:(h + 1) * HEAD_DIM] = 1.0
    return jnp.asarray(exps, _BF16), jnp.asarray(exp64, _BF16)


_POOL_OPS = ("norm_g", "w_u", "w_zp", "w_group", "pool_scale", "w_out", "final_g")


def _pool_mix(load_rows, z, wg_ref, ps_ref, pos):
    parts = []
    for g, w in enumerate(POOL_WINDOWS):
        sl = slice(g * POOL_GROUP_DIM, (g + 1) * POOL_GROUP_DIM)
        u_g = load_rows(0, sl)
        win = u_g
        for k in range(1, w):
            win = win + load_rows(k, sl)
        cnt = jnp.minimum(float(w), pos + 1.0)
        pooled = (win / cnt - u_g).astype(_BF16)
        mixed = _dot(pooled, wg_ref[g])
        parts.append((mixed * ps_ref[:, sl] * _silu(z[:, sl])).astype(_BF16))
    return jnp.concatenate(parts, axis=1)


def _pool_prompt_kernel(x_ref, mod_ref, g_ref, wu_ref, wz_ref, wg_ref, ps_ref, wout_ref, fin_ref,
                        xo_ref, npool_ref, up_ref, *, tile, apply_final):
    l = pl.program_id(1)
    hist = POOL_HIST

    @pl.when(l == 0)
    def _():
        up_ref[0:hist, :] = jnp.zeros((hist, POOL_WIDTH), _F32)

    x = x_ref[...]
    hn = _prenorm(x, g_ref[...], mod_ref[0:1, :], mod_ref[1:2, :]).astype(_BF16)
    up_ref[hist:hist + tile, :] = _dot(hn, wu_ref[...])
    z = _dot(hn, wz_ref[...])
    pos = (l * tile + lax.broadcasted_iota(jnp.int32, (tile, POOL_GROUP_DIM), 0)).astype(_F32)
    mixed = _pool_mix(lambda k, sl: up_ref[hist - k:hist - k + tile, sl], z, wg_ref, ps_ref, pos)
    xo_ref[...] = _residual(x, _dot(mixed, wout_ref[...]), mod_ref[2:3, :], fin_ref, apply_final)
    tail = up_ref[hist + tile - POOL_STATE:hist + tile, :]
    up_ref[hist - POOL_STATE:hist, :] = tail
    npool_ref[...] = tail


def _pool_prompt_layer(x, mod, w, i, apply_final):
    b, L, _ = x.shape
    tile = PROMPT_TILE
    kern = functools.partial(_pool_prompt_kernel, tile=tile, apply_final=apply_final)
    arrays, specs = _pick(_operands(w, i), _POOL_OPS)
    return pl.pallas_call(
        kern,
        grid=(b, L // tile),
        in_specs=[pl.BlockSpec((None, tile, D_MODEL), lambda s, l: (s, l, 0)),
                  pl.BlockSpec((None, 3, D_MODEL), lambda s, l: (s, 0, 0))] + specs,
        out_specs=[pl.BlockSpec((None, tile, D_MODEL), lambda s, l: (s, l, 0)),
                   pl.BlockSpec((None, POOL_STATE, POOL_WIDTH), lambda s, l: (s, 0, 0))],
        out_shape=[jax.ShapeDtypeStruct((b, L, D_MODEL), _F32),
                   jax.ShapeDtypeStruct((b, POOL_STATE, POOL_WIDTH), _F32)],
        scratch_shapes=[pltpu.VMEM((POOL_HIST + tile, POOL_WIDTH), _F32)],
        compiler_params=_params(2),
        name="pool_prompt",
    )(x, mod, *arrays)


def _tm_proj_kernel(*refs, has_dt):
    if has_dt:
        x_ref, mod_ref, g_ref, w_ref, wdt_ref, dtb_ref, alog_ref, o_ref, dt_ref, dec_ref, hn_ref = refs
    else:
        x_ref, mod_ref, g_ref, w_ref, o_ref, hn_ref = refs

    @pl.when(pl.program_id(0) == 0)
    def _():
        hn = _prenorm(x_ref[...], g_ref[...], mod_ref[0:1], mod_ref[1:2])
        hn_ref[...] = hn.reshape(hn_ref.shape).astype(_BF16)
        if has_dt:
            dt = _softplus(_dot(hn_ref[...], wdt_ref[...]) + dtb_ref[...])
            dt_ref[...] = dt
            total = jnp.sum(dt.reshape(x_ref.shape[0], x_ref.shape[1], LANES), axis=0)
            dec_ref[...] = jnp.exp(total * (-jnp.exp(alog_ref[...])))

    o_ref[...] = _dot(hn_ref[...], w_ref[...])


def _tm_proj(x, mod, w, i, weight, n_cols, has_dt):
    seq, b, _ = x.shape
    rows = seq * b
    j = i // N_MIXERS
    ops = _operands(w, i)
    arrays = [x, mod, ops["norm_g"][0], weight]
    specs = [pl.BlockSpec((seq, b, D_MODEL), lambda n: (0, 0, 0)),
             pl.BlockSpec((3, b, D_MODEL), lambda n: (0, 0, 0)),
             ops["norm_g"][1],
             pl.BlockSpec((None, D_MODEL, PROJ_COLS), lambda n: (j, 0, n))]
    out_shape = [jax.ShapeDtypeStruct((rows, n_cols), _F32)]
    out_specs = [pl.BlockSpec((rows, PROJ_COLS), lambda n: (0, n))]
    if has_dt:
        extra, extra_specs = _pick(ops, ("w_dt", "dt_bias", "a_log"))
        arrays += extra
        specs += extra_specs
        out_shape += [jax.ShapeDtypeStruct((rows, LANES), _F32), jax.ShapeDtypeStruct((b, LANES), _F32)]
        out_specs += [pl.BlockSpec((rows, LANES), lambda n: (0, 0)), pl.BlockSpec((b, LANES), lambda n: (0, 0))]
    return pl.pallas_call(
        functools.partial(_tm_proj_kernel, has_dt=has_dt),
        grid=(n_cols // PROJ_COLS,),
        in_specs=specs,
        out_specs=out_specs,
        out_shape=out_shape,
        scratch_shapes=[pltpu.VMEM((rows, D_MODEL), _BF16)],
        compiler_params=_params(1),
        name="sample_proj",
    )(*arrays)


def _tm_ssd_kernel(dec_ref, x_ref, mod_ref, z_ref, xin_ref, bcin_ref, dt_ref,
                   cw_ref, cb_ref, alog_ref, dsk_ref, ng_ref, wout_ref, fin_ref,
                   cprev_ref, hprev_ref, exps_ref, exp64_ref, _conv_alias_ref, _ssm_alias_ref,
                   xo_ref, nconv_ref, nssm_ref,
                   xp_ref, act_ref, mfac_ref, e_ref, xw_ref, y_ref, *, nseq, seq, apply_final):
    rows = nseq * seq
    n_carry = CONV_WIDTH - 1
    xp_ref[0:n_carry] = cprev_ref[...]
    xp_ref[n_carry:n_carry + seq, :, 0:D_INNER] = xin_ref[...]
    xp_ref[n_carry:n_carry + seq, :, D_INNER:CONV_DIM] = bcin_ref[...]
    conv = cb_ref[...]
    for k in range(CONV_WIDTH):
        conv = conv + xp_ref[k:k + seq] * cw_ref[k:k + 1, :]
    act = _silu(conv).reshape(rows, CONV_DIM)
    for jb in range(CONV_DIM // LANES):
        act_ref[jb] = act[:, jb * LANES:(jb + 1) * LANES]
    nconv_ref[...] = xp_ref[seq:seq + n_carry]

    dt3 = dt_ref[...]
    a3 = dt3 * (-jnp.exp(alog_ref[...]))
    slabs = [a3[0]]
    for t in range(1, seq):
        slabs.append(slabs[-1] + a3[t])
    acum3 = jnp.stack(slabs, axis=0)
    w3 = dt3 * jnp.exp(acum3[seq - 1:seq] - acum3)
    acum = acum3.reshape(rows, LANES)

    shape3 = (seq, nseq, N_GROUPS * LANES)
    z3 = _select_lanes(acum, exps_ref[...], 3).reshape(shape3)
    dtz3 = _select_lanes(dt3.reshape(rows, LANES), exps_ref[...], 2).reshape(shape3)
    t3 = lax.broadcasted_iota(jnp.int32, shape3, 0)
    s3 = lax.broadcasted_iota(jnp.int32, shape3, 2) % seq
    diag = t3 == s3
    acum_s = jnp.sum(jnp.where(diag, z3, 0.0), axis=0, keepdims=True)
    dt_s = jnp.sum(jnp.where(diag, dtz3, 0.0), axis=0, keepdims=True)
    mfac = (jnp.exp(jnp.where(s3 <= t3, z3 - acum_s, NEG_BIG)) * dt_s).reshape(rows, N_GROUPS * LANES)
    for g in range(N_GROUPS):
        mfac_ref[g] = mfac[:, g * LANES:(g + 1) * LANES]
    e = _select_lanes(jnp.exp(acum), exp64_ref[...], 2)
    wx = _select_lanes(w3.reshape(rows, LANES), exp64_ref[...], 2)
    for jb in range(D_INNER // LANES):
        lanes = slice(jb * LANES, (jb + 1) * LANES)
        e_ref[jb] = e[:, lanes]
        xw_ref[jb] = act[:, lanes] * wx[:, lanes]

    width = HEADS_PER_GROUP * seq
    bd_r = lax.broadcasted_iota(jnp.int32, (width, GROUP_WIDTH), 0)
    bd_c = lax.broadcasted_iota(jnp.int32, (width, GROUP_WIDTH), 1)
    blockdiag = (bd_r // seq) == (bd_c // HEAD_DIM)
    zeros_b = jnp.zeros((seq, D_STATE), _F32)
    b_blk = D_INNER // LANES
    c_blk = b_blk + N_GROUPS
    zeros_x = jnp.zeros((seq, GROUP_WIDTH), _F32)
    first_seq = pl.program_id(0) * nseq

    def per_seq(s, carry):
        tokens = pl.ds(s, seq, stride=nseq)

        def pair(ref, g):
            return jnp.concatenate([ref[2 * g, tokens, :], ref[2 * g + 1, tokens, :]], axis=1)

        for g in range(N_GROUPS):
            b_g = act_ref[b_blk + g, tokens, :]
            c_bf = act_ref[c_blk + g, tokens, :].astype(_BF16)
            sl = slice(g * GROUP_WIDTH, (g + 1) * GROUP_WIDTH)
            xs_g = pair(act_ref, g)
            cbx = _dot_nt(c_bf, jnp.concatenate([b_g] * HEADS_PER_GROUP, axis=0).astype(_BF16))
            mp = (cbx * mfac_ref[g, tokens, :][:, 0:width]).astype(_BF16)
            rhs = jnp.where(blockdiag, jnp.concatenate([xs_g] * HEADS_PER_GROUP, axis=0), 0.0).astype(_BF16)
            h0 = hprev_ref[s, sl, :]
            e_g = pair(e_ref, g)
            y_g = _dot(mp, rhs) + _dot_nt(c_bf, h0.astype(_BF16)) * e_g + dsk_ref[:, sl] * xs_g
            y_ref[2 * g, tokens, :] = y_g[:, 0:LANES]
            y_ref[2 * g + 1, tokens, :] = y_g[:, LANES:GROUP_WIDTH]
            lhs_t = jnp.concatenate([pair(xw_ref, g), zeros_x], axis=0).astype(_BF16)
            rhs_s = jnp.concatenate([b_g, zeros_b], axis=0).astype(_BF16)
            ds = _dot_tn(lhs_t, rhs_s)
            for hh in range(HEADS_PER_GROUP):
                decay = dec_ref[first_seq + s, g * HEADS_PER_GROUP + hh]
                hr = slice(hh * HEAD_DIM, (hh + 1) * HEAD_DIM)
                nssm_ref[s, g * GROUP_WIDTH + hh * HEAD_DIM:g * GROUP_WIDTH + (hh + 1) * HEAD_DIM, :] = (
                    h0[hr, :] * decay + ds[hr, :])
        return carry

    lax.fori_loop(0, nseq, per_seq, 0, unroll=SSD_SEQ_UNROLL)

    z = z_ref[...].reshape(rows, D_INNER)
    parts = []
    for g in range(N_GROUPS):
        sl = slice(g * GROUP_WIDTH, (g + 1) * GROUP_WIDTH)
        gated = jnp.concatenate([y_ref[2 * g], y_ref[2 * g + 1]], axis=1) * _silu(z[:, sl])
        ms = jnp.mean(gated * gated, axis=-1, keepdims=True)
        parts.append((gated * lax.rsqrt(ms + EPS) * ng_ref[:, sl]).astype(_BF16))
    out = _dot(jnp.concatenate(parts, axis=1), wout_ref[...]).reshape(seq, nseq, D_MODEL)
    xo_ref[...] = _residual(x_ref[...], out, mod_ref[2:3], fin_ref, apply_final)


def _tm_ssd_layer(x, mod, w, i, state_conv_t, state_ssm, conv_acc, ssm_acc, apply_final):
    seq, b, _ = x.shape
    j = i // N_MIXERS
    nseq = SSD_SEQ_BLOCK
    rows = nseq * seq
    n_layers = state_ssm.shape[0]
    proj, dt, decay = _tm_proj(x, mod, w, i, w["ssd_w_in"], D_INNER + CONV_DIM, True)
    proj = proj.reshape(seq, b, D_INNER + CONV_DIM)
    dt = dt.reshape(seq, b, LANES)
    kern = functools.partial(_tm_ssd_kernel, nseq=nseq, seq=seq, apply_final=apply_final)
    arrays, specs = _pick(_operands(w, i), _SSD_SAMPLE_OPS)
    exps, exp64 = _expansion_constants(seq)
    col_block = lambda n: pl.BlockSpec((seq, nseq, WEIGHT_COLS), lambda s: (0, s, n))
    in_specs = ([pl.BlockSpec(memory_space=pltpu.SMEM),
                 pl.BlockSpec((seq, nseq, D_MODEL), lambda s: (0, s, 0)),
                 pl.BlockSpec((3, nseq, D_MODEL), lambda s: (0, s, 0)),
                 col_block(0), col_block(1), col_block(2),
                 pl.BlockSpec((seq, nseq, LANES), lambda s: (0, s, 0))]
                + specs
                + [pl.BlockSpec((None, CONV_WIDTH - 1, nseq, CONV_DIM), lambda s: (j, 0, s, 0)),
                   pl.BlockSpec((None, nseq, D_INNER, D_STATE), lambda s: (j, s, 0, 0)),
                   _wspec(exps.shape, (0, 0)), _wspec(exp64.shape, (0, 0))])
    args = [decay, x, mod, proj, proj, proj, dt, *arrays, state_conv_t, state_ssm, exps, exp64]
    n_in = len(args)
    aliases = {}
    if conv_acc is not None:
        in_specs += [pl.BlockSpec(memory_space=pl.ANY), pl.BlockSpec(memory_space=pl.ANY)]
        args += [conv_acc, ssm_acc]
        aliases = {n_in: 1, n_in + 1: 2}
        kern_fn = kern
    else:
        kern_fn = lambda *refs: kern(*refs[:n_in], None, None, *refs[n_in:])
    slab = lambda width: pltpu.VMEM((width // LANES, rows, LANES), _F32)
    return pl.pallas_call(
        kern_fn,
        grid=(b // nseq,),
        in_specs=in_specs,
        out_specs=[pl.BlockSpec((seq, nseq, D_MODEL), lambda s: (0, s, 0)),
                   pl.BlockSpec((None, CONV_WIDTH - 1, nseq, CONV_DIM), lambda s: (j, 0, s, 0)),
                   pl.BlockSpec((None, nseq, D_INNER, D_STATE), lambda s: (j, s, 0, 0))],
        out_shape=[jax.ShapeDtypeStruct((seq, b, D_MODEL), _F32),
                   jax.ShapeDtypeStruct((n_layers, CONV_WIDTH - 1, b, CONV_DIM), _F32),
                   jax.ShapeDtypeStruct((n_layers, b, D_INNER, D_STATE), _F32)],
        scratch_shapes=[pltpu.VMEM((seq + CONV_WIDTH - 1, nseq, CONV_DIM), _F32),
                        slab(CONV_DIM),
                        slab(N_GROUPS * LANES),
                        slab(D_INNER),
                        slab(D_INNER),
                        slab(D_INNER)],
        input_output_aliases=aliases,
        compiler_params=_params(1),
        name="ssd_sample",
    )(*args)


def _tm_pool_kernel(x_ref, mod_ref, g_ref, wu_ref, wz_ref, wg_ref, ps_ref, wout_ref, fin_ref, prev_ref,
                    _alias_ref, xo_ref, npool_ref, up_ref, *, nseq, seq, start_pos, apply_final):
    rows = nseq * seq
    x3 = x_ref[...]
    hn = _prenorm(x3, g_ref[...], mod_ref[0:1], mod_ref[1:2]).reshape(rows, D_MODEL).astype(_BF16)
    up_ref[0:POOL_STATE] = prev_ref[...]
    up_ref[POOL_STATE:POOL_STATE + seq] = _dot(hn, wu_ref[...]).reshape(seq, nseq, POOL_WIDTH)
    z = _dot(hn, wz_ref[...])
    pos3 = start_pos + lax.broadcasted_iota(jnp.int32, (seq, nseq, POOL_GROUP_DIM), 0)
    pos = pos3.astype(_F32).reshape(rows, POOL_GROUP_DIM)
    mixed = _pool_mix(
        lambda k, sl: up_ref[POOL_STATE - k:POOL_STATE - k + seq, :, sl].reshape(rows, POOL_GROUP_DIM),
        z, wg_ref, ps_ref, pos)
    out = _dot(mixed, wout_ref[...]).reshape(seq, nseq, D_MODEL)
    xo_ref[...] = _residual(x3, out, mod_ref[2:3], fin_ref, apply_final)
    npool_ref[...] = up_ref[seq:seq + POOL_STATE]


def _tm_pool_layer(x, mod, w, i, state_pool_t, pool_acc, start_pos, apply_final):
    seq, b, _ = x.shape
    j = i // N_MIXERS
    nseq = POOL_SEQ_BLOCK
    n_layers = state_pool_t.shape[0]
    kern = functools.partial(_tm_pool_kernel, nseq=nseq, seq=seq, start_pos=start_pos,
                             apply_final=apply_final)
    arrays, specs = _pick(_operands(w, i), _POOL_OPS)
    state_spec = pl.BlockSpec((None, POOL_STATE, nseq, POOL_WIDTH), lambda s: (j, 0, s, 0))
    in_specs = ([pl.BlockSpec((seq, nseq, D_MODEL), lambda s: (0, s, 0)),
                 pl.BlockSpec((3, nseq, D_MODEL), lambda s: (0, s, 0))] + specs + [state_spec])
    args = [x, mod, *arrays, state_pool_t]
    n_in = len(args)
    aliases = {}
    if pool_acc is not None:
        in_specs.append(pl.BlockSpec(memory_space=pl.ANY))
        args.append(pool_acc)
        aliases = {n_in: 1}
        kern_fn = kern
    else:
        kern_fn = lambda *refs: kern(*refs[:n_in], None, *refs[n_in:])
    return pl.pallas_call(
        kern_fn,
        grid=(b // nseq,),
        in_specs=in_specs,
        out_specs=[pl.BlockSpec((seq, nseq, D_MODEL), lambda s: (0, s, 0)), state_spec],
        out_shape=[jax.ShapeDtypeStruct((seq, b, D_MODEL), _F32),
                   jax.ShapeDtypeStruct((n_layers, POOL_STATE, b, POOL_WIDTH), _F32)],
        scratch_shapes=[pltpu.VMEM((POOL_STATE + seq, nseq, POOL_WIDTH), _F32)],
        input_output_aliases=aliases,
        compiler_params=_params(1),
        name="pool_sample",
    )(*args)


def _prepare_weights(norm_g, ssd_w_in, ssd_conv_w, ssd_conv_b, ssd_dt_bias, ssd_a_log, ssd_d, ssd_norm_g,
                     ssd_w_out, pool_w_in, pool_w_group, pool_scale, pool_w_out, final_norm_g):
    n_ssd = ssd_w_in.shape[0]
    n_pool = pool_w_in.shape[0]
    pad = LANES - N_HEADS
    return {
        "norm_g": norm_g.reshape(DEPTH, 1, D_MODEL),
        "final_g": final_norm_g.reshape(1, D_MODEL),
        "ssd_w_in": ssd_w_in.astype(_BF16),
        "ssd_w_dt": jnp.pad(ssd_w_in[:, :, D_INNER + CONV_DIM:], ((0, 0), (0, 0), (0, pad))).astype(_BF16),
        "ssd_conv_w": ssd_conv_w,
        "ssd_conv_b": ssd_conv_b.reshape(n_ssd, 1, CONV_DIM),
        "ssd_dt_bias": jnp.pad(ssd_dt_bias, ((0, 0), (0, pad))).reshape(n_ssd, 1, LANES),
        "ssd_a_log": jnp.pad(ssd_a_log, ((0, 0), (0, pad))).reshape(n_ssd, 1, LANES),
        "ssd_d": jnp.repeat(ssd_d, HEAD_DIM, axis=1).reshape(n_ssd, 1, D_INNER),
        "ssd_norm_g": ssd_norm_g.reshape(n_ssd, 1, D_INNER),
        "ssd_w_out": ssd_w_out.astype(_BF16),
        "pool_w_in": pool_w_in.astype(_BF16),
        "pool_w_group": pool_w_group.astype(_BF16),
        "pool_scale": pool_scale.reshape(n_pool, 1, POOL_WIDTH),
        "pool_w_out": pool_w_out.astype(_BF16),
    }


def kernel(x_prompt, x_sample, state_ssm, state_conv, state_pool, c_prompt, c_sample, ada_w, ada_b, norm_g,
           ssd_w_in, ssd_conv_w, ssd_conv_b, ssd_dt_bias, ssd_a_log, ssd_d, ssd_norm_g, ssd_w_out, pool_w_in,
           pool_w_group, pool_scale, pool_w_out, final_norm_g):
    b_p = x_prompt.shape[0]
    b_s = x_sample.shape[0]
    n_ssd = state_ssm.shape[0]
    w = _prepare_weights(norm_g, ssd_w_in, ssd_conv_w, ssd_conv_b, ssd_dt_bias, ssd_a_log, ssd_d, ssd_norm_g,
                         ssd_w_out, pool_w_in, pool_w_group, pool_scale, pool_w_out, final_norm_g)
    mod = _modulation(jnp.concatenate([c_prompt, c_sample], axis=0), ada_w, ada_b)
    mod_p = mod[:, :b_p].reshape(DEPTH, b_p, 3, D_MODEL)
    mod_s = mod[:, b_p:].reshape(DEPTH, b_s, 3, D_MODEL).transpose(0, 2, 1, 3)
    ssm_in = state_ssm.reshape(n_ssd, b_s, D_INNER, D_STATE)
    conv_in = state_conv.transpose(0, 2, 1, 3)
    pool_in = state_pool.transpose(0, 2, 1, 3)

    xp, xs = x_prompt, x_sample.transpose(1, 0, 2)
    ssm_p, conv_p, pool_p = [], [], []
    ssm_s = conv_s = pool_s = None
    for i in range(DEPTH):
        last = i == DEPTH - 1
        if i % N_MIXERS == 0:
            xp, cv, st = _ssd_prompt_layer(xp, mod_p[i], w, i, last)
            conv_p.append(cv)
            ssm_p.append(st)
            xs, conv_s, ssm_s = _tm_ssd_layer(xs, mod_s[i], w, i, conv_in, ssm_in, conv_s, ssm_s, last)
        else:
            xp, ps = _pool_prompt_layer(xp, mod_p[i], w, i, last)
            pool_p.append(ps)
            xs, pool_s = _tm_pool_layer(xs, mod_s[i], w, i, pool_in, pool_s, PAST_LEN, last)

    head_shape = (N_HEADS, HEAD_DIM, D_STATE)
    return (xp, xs.transpose(1, 0, 2),
            jnp.stack(ssm_p).reshape(n_ssd, b_p, *head_shape),
            jnp.stack(conv_p), jnp.stack(pool_p),
            ssm_s.reshape(n_ssd, b_s, *head_shape),
            conv_s.transpose(0, 2, 1, 3), pool_s.transpose(0, 2, 1, 3))
```

```python
import functools

import numpy as np
import jax
import jax.numpy as jnp
from jax import lax
from jax.experimental import pallas as pl
from jax.experimental.pallas import tpu as pltpu

D_MODEL = 1024
DEPTH = 4
PAST_LEN = 16384
N_MIXERS = 2
D_INNER = 2048
HEAD_DIM = 64
N_HEADS = 32
N_GROUPS = 8
HEADS_PER_GROUP = 4
D_STATE = 128
CONV_WIDTH = 4
CONV_DIM = D_INNER + 2 * N_GROUPS * D_STATE
GROUP_WIDTH = HEADS_PER_GROUP * HEAD_DIM
POOL_WIDTH = 2048
POOL_WINDOWS = (2, 4, 8, 16)
POOL_GROUP_DIM = 512
POOL_STATE = 15
EPS = 1e-6

LANES = 128
SUBLANES = 8
CHUNK = 128
N_PHASES = CHUNK // SUBLANES
PROMPT_TILE = 512
SSD_PROMPT_TILE = 512
SSD_SEQ_BLOCK = 8
SSD_SEQ_UNROLL = 4
POOL_SEQ_BLOCK = 16
POOL_HIST = 2 * SUBLANES
WEIGHT_COLS = 2048
PIECE_COLS = 256
PROJ_COLS = 1024
VMEM_LIMIT = 60 * 1024 * 1024
NEG_BIG = -1e30
LOG2E = 1.4426950408889634

_F32 = jnp.float32
_BF16 = jnp.bfloat16
_HI = lax.Precision.HIGHEST


def _silu(v):
    h = 0.5 * v
    return h + h * jnp.tanh(h)


def _softplus(v):
    return jnp.maximum(v, 0.0) + jnp.log(1.0 + jnp.exp(-jnp.abs(v)))


def _dot(a, b):
    return jnp.dot(a, b, preferred_element_type=_F32)


def _dot_exact(a, b):
    return jnp.dot(a, b, precision=_HI, preferred_element_type=_F32)


def _dot_nt(a, b):
    return lax.dot_general(a, b, (((1,), (1,)), ((), ())), preferred_element_type=_F32)


def _dot_tn(a, b):
    return lax.dot_general(a, b, (((0,), (0,)), ((), ())), preferred_element_type=_F32)


def _select_lanes(v, onehot_bf, terms):
    acc = None
    rest = v
    for _ in range(terms):
        piece = rest.astype(_BF16)
        part = _dot(piece, onehot_bf)
        acc = part if acc is None else acc + part
        rest = rest - piece.astype(_F32)
    return acc


def _prenorm(x, g, shift, scale):
    ms = jnp.mean(x * x, axis=-1, keepdims=True)
    return x * lax.rsqrt(ms + EPS) * g * (1.0 + scale) + shift


def _residual(x, out, gate, fin_ref, apply_final):
    y = x + (1.0 + gate) * out
    if apply_final:
        ms = jnp.mean(y * y, axis=-1, keepdims=True)
        y = y * lax.rsqrt(ms + EPS) * fin_ref[...]
    return y


def _group_rmsnorm_gate(y_ref, z_ref, ng_ref):
    parts = []
    for g in range(N_GROUPS):
        sl = slice(g * GROUP_WIDTH, (g + 1) * GROUP_WIDTH)
        gated = y_ref[:, sl] * _silu(z_ref[:, sl])
        ms = jnp.mean(gated * gated, axis=-1, keepdims=True)
        parts.append((gated * lax.rsqrt(ms + EPS) * ng_ref[:, sl]).astype(_BF16))
    return jnp.concatenate(parts, axis=1)


def _wspec(block_shape, index):
    return pl.BlockSpec(block_shape, lambda *_: index, pipeline_mode=pl.Buffered(1))


def _params(n_axes):
    return pltpu.CompilerParams(dimension_semantics=("arbitrary",) * n_axes, vmem_limit_bytes=VMEM_LIMIT)


def _operands(w, i):
    j = i // N_MIXERS
    wcols = (None, D_MODEL, WEIGHT_COLS)
    ops = {"norm_g": (w["norm_g"], _wspec((None, 1, D_MODEL), (i, 0, 0))),
           "final_g": (w["final_g"], _wspec((1, D_MODEL), (0, 0)))}
    if i % N_MIXERS == 0:
        ops.update(
            w_z=(w["ssd_w_in"], _wspec(wcols, (j, 0, 0))),
            w_x=(w["ssd_w_in"], _wspec(wcols, (j, 0, 1))),
            w_bc=(w["ssd_w_in"], _wspec(wcols, (j, 0, 2))),
            w_dt=(w["ssd_w_dt"], _wspec((None, D_MODEL, LANES), (j, 0, 0))),
            conv_w=(w["ssd_conv_w"], _wspec((None, CONV_WIDTH, CONV_DIM), (j, 0, 0))),
            conv_b=(w["ssd_conv_b"], _wspec((None, 1, CONV_DIM), (j, 0, 0))),
            dt_bias=(w["ssd_dt_bias"], _wspec((None, 1, LANES), (j, 0, 0))),
            a_log=(w["ssd_a_log"], _wspec((None, 1, LANES), (j, 0, 0))),
            d_skip=(w["ssd_d"], _wspec((None, 1, D_INNER), (j, 0, 0))),
            ssd_norm_g=(w["ssd_norm_g"], _wspec((None, 1, D_INNER), (j, 0, 0))),
            w_out=(w["ssd_w_out"], _wspec((None, D_INNER, D_MODEL), (j, 0, 0))),
        )
    else:
        ops.update(
            w_v=(w["pool_w_v"], _wspec(wcols, (j, 0, 0))),
            w_u=(w["pool_w_in"], _wspec(wcols, (j, 0, 0))),
            w_zp=(w["pool_w_in"], _wspec(wcols, (j, 0, 1))),
            w_group=(w["pool_w_group"],
                     _wspec((None, len(POOL_WINDOWS), POOL_GROUP_DIM, POOL_GROUP_DIM), (j, 0, 0, 0))),
            pool_scale=(w["pool_scale"], _wspec((None, 1, POOL_WIDTH), (j, 0, 0))),
            w_out=(w["pool_w_out"], _wspec((None, POOL_WIDTH, D_MODEL), (j, 0, 0))),
        )
    return ops


def _pick(ops, names):
    return [ops[n][0] for n in names], [ops[n][1] for n in names]


def _mod_kernel(c_ref, w_ref, b_ref, o_ref):
    sc = _silu(c_ref[...]).astype(_BF16)
    o_ref[...] = _dot(sc, w_ref[...].astype(_BF16)) + b_ref[...]


def _modulation(c_all, ada_w, ada_b):
    n = c_all.shape[0]
    tn = 1024
    return pl.pallas_call(
        _mod_kernel,
        grid=(DEPTH, 3 * D_MODEL // tn),
        in_specs=[
            pl.BlockSpec((n, D_MODEL), lambda i, j: (0, 0)),
            pl.BlockSpec((None, D_MODEL, tn), lambda i, j: (i, 0, j)),
            pl.BlockSpec((None, 1, tn), lambda i, j: (i, 0, j)),
        ],
        out_specs=pl.BlockSpec((None, n, tn), lambda i, j: (i, 0, j)),
        out_shape=jax.ShapeDtypeStruct((DEPTH, n, 3 * D_MODEL), _F32),
        compiler_params=_params(2),
        name="adaln_mod",
    )(c_all, ada_w, ada_b.reshape(DEPTH, 1, 3 * D_MODEL))


_SSD_PROMPT_OPS = ("norm_g", "w_z", "w_x", "w_bc", "w_dt", "conv_w", "conv_b", "dt_bias", "a_log", "d_skip",
                   "ssd_norm_g", "w_out", "final_g")


def _ssd_prompt_kernel(*refs, tile, n_tiles, apply_final):
    n_xcols = D_MODEL // LANES
    x_refs = refs[:n_xcols]
    (mod_ref, g_ref, wz_ref, wx_ref, wbc_ref, wdt_ref, cw_ref, cb_ref, dtb_ref, alog_ref, dsk_ref, ng_ref,
     wout_ref, fin_ref, xo_ref, nconv_ref, nssm_ref,
     ht_ref, carry_ref, act_ref, z_ref, y_ref, stage_ref) = refs[n_xcols:]
    l = pl.program_id(1)
    n_chunks = tile // CHUNK
    n_carry = CONV_WIDTH - 1

    @pl.when(l == 0)
    def _():
        ht_ref[...] = jnp.zeros_like(ht_ref)
        carry_ref[...] = jnp.zeros_like(carry_ref)

    def load_x():
        return jnp.concatenate(
            [jnp.concatenate([xr[pl.ds(c * CHUNK + r, SUBLANES, stride=N_PHASES), :] for xr in x_refs], axis=1)
             for c in range(n_chunks) for r in range(N_PHASES)], axis=0)

    hn = _prenorm(load_x(), g_ref[...], mod_ref[0:1, :], mod_ref[1:2, :]).astype(_BF16)
    dt = _softplus(_dot(hn, wdt_ref[...]) + dtb_ref[...])

    last = (N_PHASES - n_carry) * SUBLANES
    for k in range(CONV_DIM // PIECE_COLS):
        cols = slice(k * PIECE_COLS, (k + 1) * PIECE_COLS)
        half = D_INNER // PIECE_COLS
        w_ref, kw = (wx_ref, k) if k < half else (wbc_ref, k - half)
        xb = _dot(hn, w_ref[:, kw * PIECE_COLS:(kw + 1) * PIECE_COLS])
        prev = carry_ref[:, cols]
        for c in range(n_chunks):
            cur = xb[c * CHUNK:(c + 1) * CHUNK, :]
            tail = cur[last:CHUNK, :]
            wrapped = [jnp.concatenate([prev[j * SUBLANES + SUBLANES - 1:(j + 1) * SUBLANES, :],
                                        tail[j * SUBLANES:(j + 1) * SUBLANES - 1, :]], axis=0)
                       for j in range(n_carry)]
            ext = jnp.concatenate(wrapped + [cur], axis=0)
            conv = cb_ref[:, cols]
            for kk in range(CONV_WIDTH):
                conv = conv + ext[kk * SUBLANES:kk * SUBLANES + CHUNK, :] * cw_ref[kk:kk + 1, cols]
            act_ref[c * CHUNK:(c + 1) * CHUNK, cols] = _silu(conv)
            prev = tail
        carry_ref[:, cols] = prev
        for j in range(n_carry):
            nconv_ref[j:j + 1, cols] = prev[(j + 1) * SUBLANES - 1:(j + 1) * SUBLANES, :]

    row = lax.broadcasted_iota(jnp.int32, (CHUNK, CHUNK), 0)
    col = lax.broadcasted_iota(jnp.int32, (CHUNK, CHUNK), 1)
    token = lambda p: (p % SUBLANES) * N_PHASES + p // SUBLANES
    causal = token(col) <= token(row)
    tril = causal.astype(_F32)
    head_of_lane = lax.broadcasted_iota(jnp.int32, (CHUNK, GROUP_WIDTH), 1) // HEAD_DIM
    low_half = lax.broadcasted_iota(jnp.int32, (CHUNK, LANES), 1) < HEAD_DIM
    neg_a = -jnp.exp(alog_ref[...])

    def decay_stage(c):
        dt_c = dt[c * CHUNK:(c + 1) * CHUNK, :]
        acum = _dot_exact(tril, dt_c * neg_a)
        acum2 = acum * LOG2E
        w_c = dt_c * jnp.exp(acum[CHUNK - 1:CHUNK, :] - acum)
        srow_t = (jnp.log(dt_c) * LOG2E - acum2).T
        return acum2, srow_t, w_c.T

    def cb_stage(c, g):
        rows = slice(c * CHUNK, (c + 1) * CHUNK)
        b_g = act_ref[rows, D_INNER + g * D_STATE:D_INNER + (g + 1) * D_STATE]
        c_lo = D_INNER + (N_GROUPS + g) * D_STATE
        c_bf = act_ref[rows, c_lo:c_lo + D_STATE].astype(_BF16)
        bt_g = b_g.T
        return c_bf, bt_g, _dot(c_bf, bt_g.astype(_BF16))

    def operand_stage(c, g, decay, cbs):
        acum2, srow_t, w_t = decay
        _, bt_g, cb = cbs
        xs_g = act_ref[c * CHUNK:(c + 1) * CHUNK, g * GROUP_WIDTH:(g + 1) * GROUP_WIDTH]
        rhs = jnp.concatenate(
            [jnp.where(head_of_lane == hh, xs_g, 0.0).astype(_BF16) for hh in range(HEADS_PER_GROUP)],
            axis=0)
        m_parts, w_parts, e_cols = [], [], []
        for hh in range(HEADS_PER_GROUP):
            h = g * HEADS_PER_GROUP + hh
            acum_bc = jnp.broadcast_to(acum2[:, h:h + 1], (CHUNK, CHUNK))
            seg = acum_bc + srow_t[h:h + 1, :]
            m_parts.append((cb * jnp.exp2(jnp.where(causal, seg, NEG_BIG))).astype(_BF16))
            w_parts.append((bt_g * w_t[h:h + 1, :]).astype(_BF16))
            e_cols.append(jnp.exp2(acum_bc))
        lhs = jnp.concatenate(
            [jnp.concatenate(m_parts, axis=1), jnp.concatenate(w_parts, axis=1)], axis=0)
        e_g = jnp.concatenate(
            [jnp.where(low_half, e_cols[0], e_cols[1]), jnp.where(low_half, e_cols[2], e_cols[3])],
            axis=1)
        return lhs, rhs, e_g

    def output_stage(c, g, cbs, e_g, out):
        rows = slice(c * CHUNK, (c + 1) * CHUNK)
        sl = slice(g * GROUP_WIDTH, (g + 1) * GROUP_WIDTH)
        ht_g = ht_ref[g]
        y_inter = _dot(cbs[0], ht_g.astype(_BF16))
        y_ref[rows, sl] = out[0:CHUNK, :] + y_inter * e_g + dsk_ref[:, sl] * act_ref[rows, sl]
        ht_ref[g] = ht_g * e_g[CHUNK - 1:CHUNK, :] + out[CHUNK:2 * CHUNK, :]

    def gate_proj_piece(p):
        if p < D_INNER // PIECE_COLS:
            zc = slice(p * PIECE_COLS, (p + 1) * PIECE_COLS)
            z_ref[:, zc] = _dot(hn, wz_ref[:, zc])

    groups = range(N_GROUPS)
    decays = [decay_stage(c) for c in range(n_chunks)]
    cbs = [[cb_stage(c, g) for g in groups] for c in range(n_chunks)]
    operands = [operand_stage(0, g, decays[0], cbs[0][g]) for g in groups]
    for c in range(n_chunks):
        outs, nxt = [], []
        for g in groups:
            lhs, rhs, _ = operands[g]
            outs.append(_dot(lhs, rhs))
            if c + 1 < n_chunks:
                nxt.append(operand_stage(c + 1, g, decays[c + 1], cbs[c + 1][g]))
        for g in groups:
            output_stage(c, g, cbs[c][g], operands[g][2], outs[g])
            gate_proj_piece(c * N_GROUPS + g)
        operands = nxt

    gn = _group_rmsnorm_gate(y_ref, z_ref, ng_ref)
    xo = _residual(load_x(), _dot(gn, wout_ref[...]), mod_ref[2:3, :], fin_ref, apply_final)
    for c in range(n_chunks):
        for r in range(N_PHASES):
            p0 = c * CHUNK + r * SUBLANES
            for j in range(n_xcols):
                stage_ref[j, pl.ds(c * CHUNK + r, SUBLANES, stride=N_PHASES), :] = (
                    xo[p0:p0 + SUBLANES, j * LANES:(j + 1) * LANES])
    for j in range(n_xcols):
        xo_ref[:, j * LANES:(j + 1) * LANES] = stage_ref[j]

    @pl.when(l == n_tiles - 1)
    def _():
        for g in range(N_GROUPS):
            nssm_ref[g * GROUP_WIDTH:(g + 1) * GROUP_WIDTH, :] = ht_ref[g].T


def _ssd_prompt_layer(x, mod, w, i, apply_final):
    b, L, _ = x.shape
    tile = SSD_PROMPT_TILE
    n_tiles = L // tile
    assert tile // CHUNK * N_GROUPS >= D_INNER // PIECE_COLS
    kern = functools.partial(_ssd_prompt_kernel, tile=tile, n_tiles=n_tiles, apply_final=apply_final)
    arrays, specs = _pick(_operands(w, i), _SSD_PROMPT_OPS)
    n_xcols = D_MODEL // LANES
    x_specs = [pl.BlockSpec((None, tile, LANES), lambda s, l, j=j: (s, l, j)) for j in range(n_xcols)]
    return pl.pallas_call(
        kern,
        grid=(b, n_tiles),
        in_specs=x_specs + [pl.BlockSpec((None, 3, D_MODEL), lambda s, l: (s, 0, 0))] + specs,
        out_specs=[pl.BlockSpec((None, tile, D_MODEL), lambda s, l: (s, l, 0)),
                   pl.BlockSpec((None, CONV_WIDTH - 1, CONV_DIM), lambda s, l: (s, 0, 0)),
                   pl.BlockSpec((None, D_INNER, D_STATE), lambda s, l: (s, 0, 0))],
        out_shape=[jax.ShapeDtypeStruct((b, L, D_MODEL), _F32),
                   jax.ShapeDtypeStruct((b, CONV_WIDTH - 1, CONV_DIM), _F32),
                   jax.ShapeDtypeStruct((b, D_INNER, D_STATE), _F32)],
        scratch_shapes=[pltpu.VMEM((N_GROUPS, D_STATE, GROUP_WIDTH), _F32),
                        pltpu.VMEM(((CONV_WIDTH - 1) * SUBLANES, CONV_DIM), _F32),
                        pltpu.VMEM((tile, CONV_DIM), _F32),
                        pltpu.VMEM((tile, D_INNER), _F32),
                        pltpu.VMEM((tile, D_INNER), _F32),
                        pltpu.VMEM((n_xcols, tile, LANES), _F32)],
        compiler_params=_params(2),
        name="ssd_prompt",
    )(*([x] * n_xcols), mod, *arrays)


_SSD_SAMPLE_OPS = ("conv_w", "conv_b", "a_log", "d_skip", "ssd_norm_g", "w_out", "final_g")


def _expansion_constants(seq):
    exps = np.zeros((LANES, N_GROUPS * LANES), np.float32)
    for h in range(N_HEADS):
        g, hh = divmod(h, HEADS_PER_GROUP)
        exps[h, g * LANES + hh * seq:g * LANES + (hh + 1) * seq] = 1.0
    exp64 = np.zeros((LANES, D_INNER), np.float32)
    for h in range(N_HEADS):
        exp64[h, h * HEAD_DIM:(h + 1) * HEAD_DIM] = 1.0
    return jnp.asarray(exps, _BF16), jnp.asarray(exp64, _BF16)


_POOL_OPS = ("norm_g", "w_u", "w_zp", "w_group", "pool_scale", "w_out", "final_g")


def _pool_mix(load_rows, z, wg_ref, ps_ref, pos):
    parts = []
    for g, w in enumerate(POOL_WINDOWS):
        sl = slice(g * POOL_GROUP_DIM, (g + 1) * POOL_GROUP_DIM)
        u_g = load_rows(0, sl)
        win = u_g
        for k in range(1, w):
            win = win + load_rows(k, sl)
        cnt = jnp.minimum(float(w), pos + 1.0)
        pooled = win / cnt - u_g
        mixed = pooled if wg_ref is None else _dot(pooled.astype(_BF16), wg_ref[g])
        parts.append((mixed * ps_ref[:, sl] * _silu(z[:, sl])).astype(_BF16))
    return jnp.concatenate(parts, axis=1)


def _fold_kernel(wu_ref, wg_ref, o_ref):
    o_ref[...] = _dot(wu_ref[...], wg_ref[...]).astype(_BF16)


def _fold_group_mix(w_in, w_group):
    n_layers = w_in.shape[0]
    n_groups = len(POOL_WINDOWS)
    return pl.pallas_call(
        _fold_kernel,
        grid=(n_layers, n_groups),
        in_specs=[pl.BlockSpec((None, D_MODEL, POOL_GROUP_DIM), lambda j, g: (j, 0, g)),
                  pl.BlockSpec((None, None, POOL_GROUP_DIM, POOL_GROUP_DIM), lambda j, g: (j, g, 0, 0))],
        out_specs=pl.BlockSpec((None, D_MODEL, POOL_GROUP_DIM), lambda j, g: (j, 0, g)),
        out_shape=jax.ShapeDtypeStruct((n_layers, D_MODEL, POOL_WIDTH), _BF16),
        compiler_params=_params(2),
        name="fold_group_mix",
    )(w_in, w_group)


def _pool_prompt_kernel(x_ref, mod_ref, g_ref, wv_ref, wu_ref, wz_ref, ps_ref, wout_ref, fin_ref,
                        xo_ref, npool_ref, vp_ref, *, tile, n_tiles, apply_final):
    l = pl.program_id(1)
    hist = POOL_HIST

    @pl.when(l == 0)
    def _():
        vp_ref[0:hist, :] = jnp.zeros((hist, POOL_WIDTH), _F32)

    x = x_ref[...]
    hn = _prenorm(x, g_ref[...], mod_ref[0:1, :], mod_ref[1:2, :]).astype(_BF16)
    vp_ref[hist:hist + tile, :] = _dot(hn, wv_ref[...])
    z = _dot(hn, wz_ref[...])
    pos = (l * tile + lax.broadcasted_iota(jnp.int32, (tile, POOL_GROUP_DIM), 0)).astype(_F32)
    mixed = _pool_mix(lambda k, sl: vp_ref[hist - k:hist - k + tile, sl], z, None, ps_ref, pos)
    xo_ref[...] = _residual(x, _dot(mixed, wout_ref[...]), mod_ref[2:3, :], fin_ref, apply_final)
    vp_ref[hist - POOL_STATE:hist, :] = vp_ref[hist + tile - POOL_STATE:hist + tile, :]

    @pl.when(l == n_tiles - 1)
    def _():
        u_tail = _dot(hn[tile - POOL_HIST:tile, :], wu_ref[...])
        npool_ref[...] = u_tail[POOL_HIST - POOL_STATE:POOL_HIST, :]


_POOL_PROMPT_OPS = ("norm_g", "w_v", "w_u", "w_zp", "pool_scale", "w_out", "final_g")


def _pool_prompt_layer(x, mod, w, i, apply_final):
    b, L, _ = x.shape
    tile = PROMPT_TILE
    n_tiles = L // tile
    kern = functools.partial(_pool_prompt_kernel, tile=tile, n_tiles=n_tiles, apply_final=apply_final)
    arrays, specs = _pick(_operands(w, i), _POOL_PROMPT_OPS)
    return pl.pallas_call(
        kern,
        grid=(b, n_tiles),
        in_specs=[pl.BlockSpec((None, tile, D_MODEL), lambda s, l: (s, l, 0)),
                  pl.BlockSpec((None, 3, D_MODEL), lambda s, l: (s, 0, 0))] + specs,
        out_specs=[pl.BlockSpec((None, tile, D_MODEL), lambda s, l: (s, l, 0)),
                   pl.BlockSpec((None, POOL_STATE, POOL_WIDTH), lambda s, l: (s, 0, 0))],
        out_shape=[jax.ShapeDtypeStruct((b, L, D_MODEL), _F32),
                   jax.ShapeDtypeStruct((b, POOL_STATE, POOL_WIDTH), _F32)],
        scratch_shapes=[pltpu.VMEM((POOL_HIST + tile, POOL_WIDTH), _F32)],
        compiler_params=_params(2),
        name="pool_prompt",
    )(x, mod, *arrays)


def _tm_proj_kernel(*refs, has_dt):
    if has_dt:
        x_ref, mod_ref, g_ref, w_ref, wdt_ref, dtb_ref, alog_ref, o_ref, dt_ref, dec_ref, hn_ref = refs
    else:
        x_ref, mod_ref, g_ref, w_ref, o_ref, hn_ref = refs

    @pl.when(pl.program_id(0) == 0)
    def _():
        hn = _prenorm(x_ref[...], g_ref[...], mod_ref[0:1], mod_ref[1:2])
        hn_ref[...] = hn.reshape(hn_ref.shape).astype(_BF16)
        if has_dt:
            dt = _softplus(_dot(hn_ref[...], wdt_ref[...]) + dtb_ref[...])
            dt_ref[...] = dt
            total = jnp.sum(dt.reshape(x_ref.shape[0], x_ref.shape[1], LANES), axis=0)
            dec_ref[...] = jnp.exp(total * (-jnp.exp(alog_ref[...])))

    o_ref[...] = _dot(hn_ref[...], w_ref[...])


def _tm_proj(x, mod, w, i, weight, n_cols, has_dt):
    seq, b, _ = x.shape
    rows = seq * b
    j = i // N_MIXERS
    ops = _operands(w, i)
    arrays = [x, mod, ops["norm_g"][0], weight]
    specs = [pl.BlockSpec((seq, b, D_MODEL), lambda n: (0, 0, 0)),
             pl.BlockSpec((3, b, D_MODEL), lambda n: (0, 0, 0)),
             ops["norm_g"][1],
             pl.BlockSpec((None, D_MODEL, PROJ_COLS), lambda n: (j, 0, n))]
    out_shape = [jax.ShapeDtypeStruct((rows, n_cols), _F32)]
    out_specs = [pl.BlockSpec((rows, PROJ_COLS), lambda n: (0, n))]
    if has_dt:
        extra, extra_specs = _pick(ops, ("w_dt", "dt_bias", "a_log"))
        arrays += extra
        specs += extra_specs
        out_shape += [jax.ShapeDtypeStruct((rows, LANES), _F32), jax.ShapeDtypeStruct((b, LANES), _F32)]
        out_specs += [pl.BlockSpec((rows, LANES), lambda n: (0, 0)), pl.BlockSpec((b, LANES), lambda n: (0, 0))]
    return pl.pallas_call(
        functools.partial(_tm_proj_kernel, has_dt=has_dt),
        grid=(n_cols // PROJ_COLS,),
        in_specs=specs,
        out_specs=out_specs,
        out_shape=out_shape,
        scratch_shapes=[pltpu.VMEM((rows, D_MODEL), _BF16)],
        compiler_params=_params(1),
        name="sample_proj",
    )(*arrays)


def _tm_ssd_kernel(dec_ref, x_ref, mod_ref, z_ref, xin_ref, bcin_ref, dt_ref,
                   cw_ref, cb_ref, alog_ref, dsk_ref, ng_ref, wout_ref, fin_ref,
                   cprev_ref, hprev_ref, exps_ref, exp64_ref, _conv_alias_ref, _ssm_alias_ref,
                   xo_ref, nconv_ref, nssm_ref,
                   xp_ref, act_ref, mfac_ref, e_ref, xw_ref, y_ref, *, nseq, seq, apply_final):
    rows = nseq * seq
    n_carry = CONV_WIDTH - 1
    xp_ref[0:n_carry] = cprev_ref[...]
    xp_ref[n_carry:n_carry + seq, :, 0:D_INNER] = xin_ref[...]
    xp_ref[n_carry:n_carry + seq, :, D_INNER:CONV_DIM] = bcin_ref[...]
    conv = cb_ref[...]
    for k in range(CONV_WIDTH):
        conv = conv + xp_ref[k:k + seq] * cw_ref[k:k + 1, :]
    act = _silu(conv).reshape(rows, CONV_DIM)
    for jb in range(CONV_DIM // LANES):
        act_ref[jb] = act[:, jb * LANES:(jb + 1) * LANES]
    nconv_ref[...] = xp_ref[seq:seq + n_carry]

    dt3 = dt_ref[...]
    a3 = dt3 * (-jnp.exp(alog_ref[...]))
    slabs = [a3[0]]
    for t in range(1, seq):
        slabs.append(slabs[-1] + a3[t])
    acum3 = jnp.stack(slabs, axis=0)
    w3 = dt3 * jnp.exp(acum3[seq - 1:seq] - acum3)
    acum = acum3.reshape(rows, LANES)

    shape3 = (seq, nseq, N_GROUPS * LANES)
    z3 = _select_lanes(acum, exps_ref[...], 3).reshape(shape3)
    dtz3 = _select_lanes(dt3.reshape(rows, LANES), exps_ref[...], 2).reshape(shape3)
    t3 = lax.broadcasted_iota(jnp.int32, shape3, 0)
    s3 = lax.broadcasted_iota(jnp.int32, shape3, 2) % seq
    diag = t3 == s3
    acum_s = jnp.sum(jnp.where(diag, z3, 0.0), axis=0, keepdims=True)
    dt_s = jnp.sum(jnp.where(diag, dtz3, 0.0), axis=0, keepdims=True)
    mfac = (jnp.exp(jnp.where(s3 <= t3, z3 - acum_s, NEG_BIG)) * dt_s).reshape(rows, N_GROUPS * LANES)
    for g in range(N_GROUPS):
        mfac_ref[g] = mfac[:, g * LANES:(g + 1) * LANES]
    e = _select_lanes(jnp.exp(acum), exp64_ref[...], 2)
    wx = _select_lanes(w3.reshape(rows, LANES), exp64_ref[...], 2)
    for jb in range(D_INNER // LANES):
        lanes = slice(jb * LANES, (jb + 1) * LANES)
        e_ref[jb] = e[:, lanes]
        xw_ref[jb] = act[:, lanes] * wx[:, lanes]

    width = HEADS_PER_GROUP * seq
    bd_r = lax.broadcasted_iota(jnp.int32, (width, GROUP_WIDTH), 0)
    bd_c = lax.broadcasted_iota(jnp.int32, (width, GROUP_WIDTH), 1)
    blockdiag = (bd_r // seq) == (bd_c // HEAD_DIM)
    zeros_b = jnp.zeros((seq, D_STATE), _F32)
    b_blk = D_INNER // LANES
    c_blk = b_blk + N_GROUPS
    zeros_x = jnp.zeros((seq, GROUP_WIDTH), _F32)
    first_seq = pl.program_id(0) * nseq

    def per_seq(s, carry):
        tokens = pl.ds(s, seq, stride=nseq)

        def pair(ref, g):
            return jnp.concatenate([ref[2 * g, tokens, :], ref[2 * g + 1, tokens, :]], axis=1)

        for g in range(N_GROUPS):
            b_g = act_ref[b_blk + g, tokens, :]
            c_bf = act_ref[c_blk + g, tokens, :].astype(_BF16)
            sl = slice(g * GROUP_WIDTH, (g + 1) * GROUP_WIDTH)
            xs_g = pair(act_ref, g)
            cbx = _dot_nt(c_bf, jnp.concatenate([b_g] * HEADS_PER_GROUP, axis=0).astype(_BF16))
            mp = (cbx * mfac_ref[g, tokens, :][:, 0:width]).astype(_BF16)
            rhs = jnp.where(blockdiag, jnp.concatenate([xs_g] * HEADS_PER_GROUP, axis=0), 0.0).astype(_BF16)
            h0 = hprev_ref[s, sl, :]
            e_g = pair(e_ref, g)
            y_g = _dot(mp, rhs) + _dot_nt(c_bf, h0.astype(_BF16)) * e_g + dsk_ref[:, sl] * xs_g
            y_ref[2 * g, tokens, :] = y_g[:, 0:LANES]
            y_ref[2 * g + 1, tokens, :] = y_g[:, LANES:GROUP_WIDTH]
            lhs_t = jnp.concatenate([pair(xw_ref, g), zeros_x], axis=0).astype(_BF16)
            rhs_s = jnp.concatenate([b_g, zeros_b], axis=0).astype(_BF16)
            ds = _dot_tn(lhs_t, rhs_s)
            for hh in range(HEADS_PER_GROUP):
                decay = dec_ref[first_seq + s, g * HEADS_PER_GROUP + hh]
                hr = slice(hh * HEAD_DIM, (hh + 1) * HEAD_DIM)
                nssm_ref[s, g * GROUP_WIDTH + hh * HEAD_DIM:g * GROUP_WIDTH + (hh + 1) * HEAD_DIM, :] = (
                    h0[hr, :] * decay + ds[hr, :])
        return carry

    lax.fori_loop(0, nseq, per_seq, 0, unroll=SSD_SEQ_UNROLL)

    z = z_ref[...].reshape(rows, D_INNER)
    parts = []
    for g in range(N_GROUPS):
        sl = slice(g * GROUP_WIDTH, (g + 1) * GROUP_WIDTH)
        gated = jnp.concatenate([y_ref[2 * g], y_ref[2 * g + 1]], axis=1) * _silu(z[:, sl])
        ms = jnp.mean(gated * gated, axis=-1, keepdims=True)
        parts.append((gated * lax.rsqrt(ms + EPS) * ng_ref[:, sl]).astype(_BF16))
    out = _dot(jnp.concatenate(parts, axis=1), wout_ref[...]).reshape(seq, nseq, D_MODEL)
    xo_ref[...] = _residual(x_ref[...], out, mod_ref[2:3], fin_ref, apply_final)


def _tm_ssd_layer(x, mod, w, i, state_conv_t, state_ssm, conv_acc, ssm_acc, apply_final):
    seq, b, _ = x.shape
    j = i // N_MIXERS
    nseq = SSD_SEQ_BLOCK
    rows = nseq * seq
    n_layers = state_ssm.shape[0]
    proj, dt, decay = _tm_proj(x, mod, w, i, w["ssd_w_in"], D_INNER + CONV_DIM, True)
    proj = proj.reshape(seq, b, D_INNER + CONV_DIM)
    dt = dt.reshape(seq, b, LANES)
    kern = functools.partial(_tm_ssd_kernel, nseq=nseq, seq=seq, apply_final=apply_final)
    arrays, specs = _pick(_operands(w, i), _SSD_SAMPLE_OPS)
    exps, exp64 = _expansion_constants(seq)
    col_block = lambda n: pl.BlockSpec((seq, nseq, WEIGHT_COLS), lambda s: (0, s, n))
    in_specs = ([pl.BlockSpec(memory_space=pltpu.SMEM),
                 pl.BlockSpec((seq, nseq, D_MODEL), lambda s: (0, s, 0)),
                 pl.BlockSpec((3, nseq, D_MODEL), lambda s: (0, s, 0)),
                 col_block(0), col_block(1), col_block(2),
                 pl.BlockSpec((seq, nseq, LANES), lambda s: (0, s, 0))]
                + specs
                + [pl.BlockSpec((None, CONV_WIDTH - 1, nseq, CONV_DIM), lambda s: (j, 0, s, 0)),
                   pl.BlockSpec((None, nseq, D_INNER, D_STATE), lambda s: (j, s, 0, 0)),
                   _wspec(exps.shape, (0, 0)), _wspec(exp64.shape, (0, 0))])
    args = [decay, x, mod, proj, proj, proj, dt, *arrays, state_conv_t, state_ssm, exps, exp64]
    n_in = len(args)
    aliases = {}
    if conv_acc is not None:
        in_specs += [pl.BlockSpec(memory_space=pl.ANY), pl.BlockSpec(memory_space=pl.ANY)]
        args += [conv_acc, ssm_acc]
        aliases = {n_in: 1, n_in + 1: 2}
        kern_fn = kern
    else:
        kern_fn = lambda *refs: kern(*refs[:n_in], None, None, *refs[n_in:])
    slab = lambda width: pltpu.VMEM((width // LANES, rows, LANES), _F32)
    return pl.pallas_call(
        kern_fn,
        grid=(b // nseq,),
        in_specs=in_specs,
        out_specs=[pl.BlockSpec((seq, nseq, D_MODEL), lambda s: (0, s, 0)),
                   pl.BlockSpec((None, CONV_WIDTH - 1, nseq, CONV_DIM), lambda s: (j, 0, s, 0)),
                   pl.BlockSpec((None, nseq, D_INNER, D_STATE), lambda s: (j, s, 0, 0))],
        out_shape=[jax.ShapeDtypeStruct((seq, b, D_MODEL), _F32),
                   jax.ShapeDtypeStruct((n_layers, CONV_WIDTH - 1, b, CONV_DIM), _F32),
                   jax.ShapeDtypeStruct((n_layers, b, D_INNER, D_STATE), _F32)],
        scratch_shapes=[pltpu.VMEM((seq + CONV_WIDTH - 1, nseq, CONV_DIM), _F32),
                        slab(CONV_DIM),
                        slab(N_GROUPS * LANES),
                        slab(D_INNER),
                        slab(D_INNER),
                        slab(D_INNER)],
        input_output_aliases=aliases,
        compiler_params=_params(1),
        name="ssd_sample",
    )(*args)


def _tm_pool_kernel(x_ref, mod_ref, g_ref, wu_ref, wz_ref, wg_ref, ps_ref, wout_ref, fin_ref, prev_ref,
                    _alias_ref, xo_ref, npool_ref, up_ref, *, nseq, seq, start_pos, apply_final):
    rows = nseq * seq
    x3 = x_ref[...]
    hn = _prenorm(x3, g_ref[...], mod_ref[0:1], mod_ref[1:2]).reshape(rows, D_MODEL).astype(_BF16)
    up_ref[0:POOL_STATE] = prev_ref[...]
    up_ref[POOL_STATE:POOL_STATE + seq] = _dot(hn, wu_ref[...]).reshape(seq, nseq, POOL_WIDTH)
    z = _dot(hn, wz_ref[...])
    pos3 = start_pos + lax.broadcasted_iota(jnp.int32, (seq, nseq, POOL_GROUP_DIM), 0)
    pos = pos3.astype(_F32).reshape(rows, POOL_GROUP_DIM)
    mixed = _pool_mix(
        lambda k, sl: up_ref[POOL_STATE - k:POOL_STATE - k + seq, :, sl].reshape(rows, POOL_GROUP_DIM),
        z, wg_ref, ps_ref, pos)
    out = _dot(mixed, wout_ref[...]).reshape(seq, nseq, D_MODEL)
    xo_ref[...] = _residual(x3, out, mod_ref[2:3], fin_ref, apply_final)
    npool_ref[...] = up_ref[seq:seq + POOL_STATE]


def _tm_pool_layer(x, mod, w, i, state_pool_t, pool_acc, start_pos, apply_final):
    seq, b, _ = x.shape
    j = i // N_MIXERS
    nseq = POOL_SEQ_BLOCK
    n_layers = state_pool_t.shape[0]
    kern = functools.partial(_tm_pool_kernel, nseq=nseq, seq=seq, start_pos=start_pos,
                             apply_final=apply_final)
    arrays, specs = _pick(_operands(w, i), _POOL_OPS)
    state_spec = pl.BlockSpec((None, POOL_STATE, nseq, POOL_WIDTH), lambda s: (j, 0, s, 0))
    in_specs = ([pl.BlockSpec((seq, nseq, D_MODEL), lambda s: (0, s, 0)),
                 pl.BlockSpec((3, nseq, D_MODEL), lambda s: (0, s, 0))] + specs + [state_spec])
    args = [x, mod, *arrays, state_pool_t]
    n_in = len(args)
    aliases = {}
    if pool_acc is not None:
        in_specs.append(pl.BlockSpec(memory_space=pl.ANY))
        args.append(pool_acc)
        aliases = {n_in: 1}
        kern_fn = kern
    else:
        kern_fn = lambda *refs: kern(*refs[:n_in], None, *refs[n_in:])
    return pl.pallas_call(
        kern_fn,
        grid=(b // nseq,),
        in_specs=in_specs,
        out_specs=[pl.BlockSpec((seq, nseq, D_MODEL), lambda s: (0, s, 0)), state_spec],
        out_shape=[jax.ShapeDtypeStruct((seq, b, D_MODEL), _F32),
                   jax.ShapeDtypeStruct((n_layers, POOL_STATE, b, POOL_WIDTH), _F32)],
        scratch_shapes=[pltpu.VMEM((POOL_STATE + seq, nseq, POOL_WIDTH), _F32)],
        input_output_aliases=aliases,
        compiler_params=_params(1),
        name="pool_sample",
    )(*args)


def _prepare_weights(norm_g, ssd_w_in, ssd_conv_w, ssd_conv_b, ssd_dt_bias, ssd_a_log, ssd_d, ssd_norm_g,
                     ssd_w_out, pool_w_in, pool_w_group, pool_scale, pool_w_out, final_norm_g):
    n_ssd = ssd_w_in.shape[0]
    n_pool = pool_w_in.shape[0]
    pad = LANES - N_HEADS
    pool_w_in_bf = pool_w_in.astype(_BF16)
    pool_w_group_bf = pool_w_group.astype(_BF16)
    return {
        "pool_w_v": _fold_group_mix(pool_w_in_bf, pool_w_group_bf),
        "norm_g": norm_g.reshape(DEPTH, 1, D_MODEL),
        "final_g": final_norm_g.reshape(1, D_MODEL),
        "ssd_w_in": ssd_w_in.astype(_BF16),
        "ssd_w_dt": jnp.pad(ssd_w_in[:, :, D_INNER + CONV_DIM:], ((0, 0), (0, 0), (0, pad))).astype(_BF16),
        "ssd_conv_w": ssd_conv_w,
        "ssd_conv_b": ssd_conv_b.reshape(n_ssd, 1, CONV_DIM),
        "ssd_dt_bias": jnp.pad(ssd_dt_bias, ((0, 0), (0, pad))).reshape(n_ssd, 1, LANES),
        "ssd_a_log": jnp.pad(ssd_a_log, ((0, 0), (0, pad))).reshape(n_ssd, 1, LANES),
        "ssd_d": jnp.repeat(ssd_d, HEAD_DIM, axis=1).reshape(n_ssd, 1, D_INNER),
        "ssd_norm_g": ssd_norm_g.reshape(n_ssd, 1, D_INNER),
        "ssd_w_out": ssd_w_out.astype(_BF16),
        "pool_w_in": pool_w_in_bf,
        "pool_w_group": pool_w_group_bf,
        "pool_scale": pool_scale.reshape(n_pool, 1, POOL_WIDTH),
        "pool_w_out": pool_w_out.astype(_BF16),
    }


def kernel(x_prompt, x_sample, state_ssm, state_conv, state_pool, c_prompt, c_sample, ada_w, ada_b, norm_g,
           ssd_w_in, ssd_conv_w, ssd_conv_b, ssd_dt_bias, ssd_a_log, ssd_d, ssd_norm_g, ssd_w_out, pool_w_in,
           pool_w_group, pool_scale, pool_w_out, final_norm_g):
    b_p = x_prompt.shape[0]
    b_s = x_sample.shape[0]
    n_ssd = state_ssm.shape[0]
    w = _prepare_weights(norm_g, ssd_w_in, ssd_conv_w, ssd_conv_b, ssd_dt_bias, ssd_a_log, ssd_d, ssd_norm_g,
                         ssd_w_out, pool_w_in, pool_w_group, pool_scale, pool_w_out, final_norm_g)
    mod = _modulation(jnp.concatenate([c_prompt, c_sample], axis=0), ada_w, ada_b)
    mod_p = mod[:, :b_p].reshape(DEPTH, b_p, 3, D_MODEL)
    mod_s = mod[:, b_p:].reshape(DEPTH, b_s, 3, D_MODEL).transpose(0, 2, 1, 3)
    ssm_in = state_ssm.reshape(n_ssd, b_s, D_INNER, D_STATE)
    conv_in = state_conv.transpose(0, 2, 1, 3)
    pool_in = state_pool.transpose(0, 2, 1, 3)

    xp, xs = x_prompt, x_sample.transpose(1, 0, 2)
    ssm_p, conv_p, pool_p = [], [], []
    ssm_s = conv_s = pool_s = None
    for i in range(DEPTH):
        last = i == DEPTH - 1
        if i % N_MIXERS == 0:
            xp, cv, st = _ssd_prompt_layer(xp, mod_p[i], w, i, last)
            conv_p.append(cv)
            ssm_p.append(st)
            xs, conv_s, ssm_s = _tm_ssd_layer(xs, mod_s[i], w, i, conv_in, ssm_in, conv_s, ssm_s, last)
        else:
            xp, ps = _pool_prompt_layer(xp, mod_p[i], w, i, last)
            pool_p.append(ps)
            xs, pool_s = _tm_pool_layer(xs, mod_s[i], w, i, pool_in, pool_s, PAST_LEN, last)

    head_shape = (N_HEADS, HEAD_DIM, D_STATE)
    return (xp, xs.transpose(1, 0, 2),
            jnp.stack(ssm_p).reshape(n_ssd, b_p, *head_shape),
            jnp.stack(conv_p), jnp.stack(pool_p),
            ssm_s.reshape(n_ssd, b_s, *head_shape),
            conv_s.transpose(0, 2, 1, 3), pool_s.transpose(0, 2, 1, 3))
```

```python
import functools

import numpy as np
import jax
import jax.numpy as jnp
from jax import lax
from jax.experimental import pallas as pl
from jax.experimental.pallas import tpu as pltpu

D_MODEL = 1024
DEPTH = 4
PAST_LEN = 16384
N_MIXERS = 2
D_INNER = 2048
HEAD_DIM = 64
N_HEADS = 32
N_GROUPS = 8
HEADS_PER_GROUP = 4
D_STATE = 128
CONV_WIDTH = 4
CONV_DIM = D_INNER + 2 * N_GROUPS * D_STATE
GROUP_WIDTH = HEADS_PER_GROUP * HEAD_DIM
POOL_WIDTH = 2048
POOL_WINDOWS = (2, 4, 8, 16)
POOL_GROUP_DIM = 512
POOL_STATE = 15
EPS = 1e-6

LANES = 128
SUBLANES = 8
CHUNK = 128
N_PHASES = CHUNK // SUBLANES
PROMPT_TILE = 512
SSD_PROMPT_TILE = 512
SSD_SEQ_BLOCK = 8
SSD_SEQ_UNROLL = 8
POOL_SEQ_BLOCK = 16
POOL_HIST = 2 * SUBLANES
WEIGHT_COLS = 2048
PIECE_COLS = 256
PROJ_COLS = 1024
VMEM_LIMIT = 60 * 1024 * 1024
NEG_BIG = -1e30
LOG2E = 1.4426950408889634

_F32 = jnp.float32
_BF16 = jnp.bfloat16
_HI = lax.Precision.HIGHEST


def _silu(v):
    h = 0.5 * v
    return h + h * jnp.tanh(h)


def _softplus(v):
    return jnp.maximum(v, 0.0) + jnp.log(1.0 + jnp.exp(-jnp.abs(v)))


def _dot(a, b):
    return jnp.dot(a, b, preferred_element_type=_F32)


def _dot_exact(a, b):
    return jnp.dot(a, b, precision=_HI, preferred_element_type=_F32)


def _dot_nt(a, b):
    return lax.dot_general(a, b, (((1,), (1,)), ((), ())), preferred_element_type=_F32)


def _dot_tn(a, b):
    return lax.dot_general(a, b, (((0,), (0,)), ((), ())), preferred_element_type=_F32)


def _select_lanes(v, onehot_bf, terms):
    acc = None
    rest = v
    for _ in range(terms):
        piece = rest.astype(_BF16)
        part = _dot(piece, onehot_bf)
        acc = part if acc is None else acc + part
        rest = rest - piece.astype(_F32)
    return acc


def _prenorm(x, g, shift, scale):
    ms = jnp.mean(x * x, axis=-1, keepdims=True)
    return x * lax.rsqrt(ms + EPS) * g * (1.0 + scale) + shift


def _residual(x, out, gate, fin_ref, apply_final):
    y = x + (1.0 + gate) * out
    if apply_final:
        ms = jnp.mean(y * y, axis=-1, keepdims=True)
        y = y * lax.rsqrt(ms + EPS) * fin_ref[...]
    return y


def _group_rmsnorm_gate(y_ref, z_ref, ng_ref):
    parts = []
    for g in range(N_GROUPS):
        sl = slice(g * GROUP_WIDTH, (g + 1) * GROUP_WIDTH)
        gated = y_ref[:, sl] * _silu(z_ref[:, sl])
        ms = jnp.mean(gated * gated, axis=-1, keepdims=True)
        parts.append((gated * lax.rsqrt(ms + EPS) * ng_ref[:, sl]).astype(_BF16))
    return jnp.concatenate(parts, axis=1)


def _wspec(block_shape, index):
    return pl.BlockSpec(block_shape, lambda *_: index, pipeline_mode=pl.Buffered(1))


def _params(n_axes):
    return pltpu.CompilerParams(dimension_semantics=("arbitrary",) * n_axes, vmem_limit_bytes=VMEM_LIMIT)


def _operands(w, i):
    j = i // N_MIXERS
    wcols = (None, D_MODEL, WEIGHT_COLS)
    ops = {"norm_g": (w["norm_g"], _wspec((None, 1, D_MODEL), (i, 0, 0))),
           "final_g": (w["final_g"], _wspec((1, D_MODEL), (0, 0)))}
    if i % N_MIXERS == 0:
        ops.update(
            w_z=(w["ssd_w_in"], _wspec(wcols, (j, 0, 0))),
            w_x=(w["ssd_w_in"], _wspec(wcols, (j, 0, 1))),
            w_bc=(w["ssd_w_in"], _wspec(wcols, (j, 0, 2))),
            w_dt=(w["ssd_w_dt"], _wspec((None, D_MODEL, LANES), (j, 0, 0))),
            conv_w=(w["ssd_conv_w"], _wspec((None, CONV_WIDTH, CONV_DIM), (j, 0, 0))),
            conv_b=(w["ssd_conv_b"], _wspec((None, 1, CONV_DIM), (j, 0, 0))),
            dt_bias=(w["ssd_dt_bias"], _wspec((None, 1, LANES), (j, 0, 0))),
            a_log=(w["ssd_a_log"], _wspec((None, 1, LANES), (j, 0, 0))),
            d_skip=(w["ssd_d"], _wspec((None, 1, D_INNER), (j, 0, 0))),
            ssd_norm_g=(w["ssd_norm_g"], _wspec((None, 1, D_INNER), (j, 0, 0))),
            w_out=(w["ssd_w_out"], _wspec((None, D_INNER, D_MODEL), (j, 0, 0))),
        )
    else:
        ops.update(
            w_v=(w["pool_w_v"], _wspec(wcols, (j, 0, 0))),
            w_u=(w["pool_w_in"], _wspec(wcols, (j, 0, 0))),
            w_zp=(w["pool_w_in"], _wspec(wcols, (j, 0, 1))),
            w_group=(w["pool_w_group"],
                     _wspec((None, len(POOL_WINDOWS), POOL_GROUP_DIM, POOL_GROUP_DIM), (j, 0, 0, 0))),
            pool_scale=(w["pool_scale"], _wspec((None, 1, POOL_WIDTH), (j, 0, 0))),
            w_out=(w["pool_w_out"], _wspec((None, POOL_WIDTH, D_MODEL), (j, 0, 0))),
        )
    return ops


def _pick(ops, names):
    return [ops[n][0] for n in names], [ops[n][1] for n in names]


def _mod_kernel(c_ref, w_ref, b_ref, o_ref):
    sc = _silu(c_ref[...]).astype(_BF16)
    o_ref[...] = _dot(sc, w_ref[...].astype(_BF16)) + b_ref[...]


def _modulation(c_all, ada_w, ada_b):
    n = c_all.shape[0]
    tn = 1024
    return pl.pallas_call(
        _mod_kernel,
        grid=(DEPTH, 3 * D_MODEL // tn),
        in_specs=[
            pl.BlockSpec((n, D_MODEL), lambda i, j: (0, 0)),
            pl.BlockSpec((None, D_MODEL, tn), lambda i, j: (i, 0, j)),
            pl.BlockSpec((None, 1, tn), lambda i, j: (i, 0, j)),
        ],
        out_specs=pl.BlockSpec((None, n, tn), lambda i, j: (i, 0, j)),
        out_shape=jax.ShapeDtypeStruct((DEPTH, n, 3 * D_MODEL), _F32),
        compiler_params=_params(2),
        name="adaln_mod",
    )(c_all, ada_w, ada_b.reshape(DEPTH, 1, 3 * D_MODEL))


_SSD_PROMPT_OPS = ("norm_g", "w_z", "w_x", "w_bc", "w_dt", "conv_w", "conv_b", "dt_bias", "a_log", "d_skip",
                   "ssd_norm_g", "w_out", "final_g")


def _ssd_prompt_kernel(*refs, tile, n_tiles, apply_final):
    n_xcols = D_MODEL // LANES
    x_refs = refs[:n_xcols]
    (mod_ref, g_ref, wz_ref, wx_ref, wbc_ref, wdt_ref, cw_ref, cb_ref, dtb_ref, alog_ref, dsk_ref, ng_ref,
     wout_ref, fin_ref, xo_ref, nconv_ref, nssm_ref,
     ht_ref, carry_ref, act_ref, z_ref, y_ref, stage_ref) = refs[n_xcols:]
    l = pl.program_id(1)
    n_chunks = tile // CHUNK
    n_carry = CONV_WIDTH - 1

    @pl.when(l == 0)
    def _():
        ht_ref[...] = jnp.zeros_like(ht_ref)
        carry_ref[...] = jnp.zeros_like(carry_ref)

    def load_x():
        return jnp.concatenate(
            [jnp.concatenate([xr[pl.ds(c * CHUNK + r, SUBLANES, stride=N_PHASES), :] for xr in x_refs], axis=1)
             for c in range(n_chunks) for r in range(N_PHASES)], axis=0)

    hn = _prenorm(load_x(), g_ref[...], mod_ref[0:1, :], mod_ref[1:2, :]).astype(_BF16)
    dt = _softplus(_dot(hn, wdt_ref[...]) + dtb_ref[...])

    last = (N_PHASES - n_carry) * SUBLANES
    for k in range(CONV_DIM // PIECE_COLS):
        cols = slice(k * PIECE_COLS, (k + 1) * PIECE_COLS)
        half = D_INNER // PIECE_COLS
        w_ref, kw = (wx_ref, k) if k < half else (wbc_ref, k - half)
        xb = _dot(hn, w_ref[:, kw * PIECE_COLS:(kw + 1) * PIECE_COLS])
        prev = carry_ref[:, cols]
        for c in range(n_chunks):
            cur = xb[c * CHUNK:(c + 1) * CHUNK, :]
            tail = cur[last:CHUNK, :]
            wrapped = [jnp.concatenate([prev[j * SUBLANES + SUBLANES - 1:(j + 1) * SUBLANES, :],
                                        tail[j * SUBLANES:(j + 1) * SUBLANES - 1, :]], axis=0)
                       for j in range(n_carry)]
            ext = jnp.concatenate(wrapped + [cur], axis=0)
            conv = cb_ref[:, cols]
            for kk in range(CONV_WIDTH):
                conv = conv + ext[kk * SUBLANES:kk * SUBLANES + CHUNK, :] * cw_ref[kk:kk + 1, cols]
            act_ref[c * CHUNK:(c + 1) * CHUNK, cols] = _silu(conv)
            prev = tail
        carry_ref[:, cols] = prev
        for j in range(n_carry):
            nconv_ref[j:j + 1, cols] = prev[(j + 1) * SUBLANES - 1:(j + 1) * SUBLANES, :]

    row = lax.broadcasted_iota(jnp.int32, (CHUNK, CHUNK), 0)
    col = lax.broadcasted_iota(jnp.int32, (CHUNK, CHUNK), 1)
    token = lambda p: (p % SUBLANES) * N_PHASES + p // SUBLANES
    causal = token(col) <= token(row)
    tril = causal.astype(_F32)
    head_of_lane = lax.broadcasted_iota(jnp.int32, (CHUNK, GROUP_WIDTH), 1) // HEAD_DIM
    low_half = lax.broadcasted_iota(jnp.int32, (CHUNK, LANES), 1) < HEAD_DIM
    neg_a = -jnp.exp(alog_ref[...])

    def decay_stage(c):
        dt_c = dt[c * CHUNK:(c + 1) * CHUNK, :]
        acum = _dot_exact(tril, dt_c * neg_a)
        acum2 = acum * LOG2E
        w_c = dt_c * jnp.exp(acum[CHUNK - 1:CHUNK, :] - acum)
        srow_t = (jnp.log(dt_c) * LOG2E - acum2).T
        return acum2, srow_t, w_c.T

    def cb_stage(c, g):
        rows = slice(c * CHUNK, (c + 1) * CHUNK)
        b_g = act_ref[rows, D_INNER + g * D_STATE:D_INNER + (g + 1) * D_STATE]
        c_lo = D_INNER + (N_GROUPS + g) * D_STATE
        c_bf = act_ref[rows, c_lo:c_lo + D_STATE].astype(_BF16)
        bt_g = b_g.T
        return c_bf, bt_g, _dot(c_bf, bt_g.astype(_BF16))

    def operand_stage(c, g, decay, cbs):
        acum2, srow_t, w_t = decay
        _, bt_g, cb = cbs
        xs_g = act_ref[c * CHUNK:(c + 1) * CHUNK, g * GROUP_WIDTH:(g + 1) * GROUP_WIDTH]
        rhs = jnp.concatenate(
            [jnp.where(head_of_lane == hh, xs_g, 0.0).astype(_BF16) for hh in range(HEADS_PER_GROUP)],
            axis=0)
        m_parts, w_parts, e_cols = [], [], []
        for hh in range(HEADS_PER_GROUP):
            h = g * HEADS_PER_GROUP + hh
            acum_bc = jnp.broadcast_to(acum2[:, h:h + 1], (CHUNK, CHUNK))
            seg = acum_bc + srow_t[h:h + 1, :]
            m_parts.append((cb * jnp.exp2(jnp.where(causal, seg, NEG_BIG))).astype(_BF16))
            w_parts.append((bt_g * w_t[h:h + 1, :]).astype(_BF16))
            e_cols.append(jnp.exp2(acum_bc))
        lhs = jnp.concatenate(
            [jnp.concatenate(m_parts, axis=1), jnp.concatenate(w_parts, axis=1)], axis=0)
        e_g = jnp.concatenate(
            [jnp.where(low_half, e_cols[0], e_cols[1]), jnp.where(low_half, e_cols[2], e_cols[3])],
            axis=1)
        return lhs, rhs, e_g

    def output_stage(c, g, cbs, e_g, out):
        rows = slice(c * CHUNK, (c + 1) * CHUNK)
        sl = slice(g * GROUP_WIDTH, (g + 1) * GROUP_WIDTH)
        ht_g = ht_ref[g]
        y_inter = _dot(cbs[0], ht_g.astype(_BF16))
        y_ref[rows, sl] = out[0:CHUNK, :] + y_inter * e_g + dsk_ref[:, sl] * act_ref[rows, sl]
        ht_ref[g] = ht_g * e_g[CHUNK - 1:CHUNK, :] + out[CHUNK:2 * CHUNK, :]

    def gate_proj_piece(p):
        if p < D_INNER // PIECE_COLS:
            zc = slice(p * PIECE_COLS, (p + 1) * PIECE_COLS)
            z_ref[:, zc] = _dot(hn, wz_ref[:, zc])

    groups = range(N_GROUPS)
    decays = [decay_stage(c) for c in range(n_chunks)]
    cbs = [[cb_stage(c, g) for g in groups] for c in range(n_chunks)]
    operands = [operand_stage(0, g, decays[0], cbs[0][g]) for g in groups]
    for c in range(n_chunks):
        outs, nxt = [], []
        for g in groups:
            lhs, rhs, _ = operands[g]
            outs.append(_dot(lhs, rhs))
            if c + 1 < n_chunks:
                nxt.append(operand_stage(c + 1, g, decays[c + 1], cbs[c + 1][g]))
        for g in groups:
            output_stage(c, g, cbs[c][g], operands[g][2], outs[g])
            gate_proj_piece(c * N_GROUPS + g)
        operands = nxt

    gn = _group_rmsnorm_gate(y_ref, z_ref, ng_ref)
    xo = _residual(load_x(), _dot(gn, wout_ref[...]), mod_ref[2:3, :], fin_ref, apply_final)
    for c in range(n_chunks):
        for r in range(N_PHASES):
            p0 = c * CHUNK + r * SUBLANES
            for j in range(n_xcols):
                stage_ref[j, pl.ds(c * CHUNK + r, SUBLANES, stride=N_PHASES), :] = (
                    xo[p0:p0 + SUBLANES, j * LANES:(j + 1) * LANES])
    for j in range(n_xcols):
        xo_ref[:, j * LANES:(j + 1) * LANES] = stage_ref[j]

    @pl.when(l == n_tiles - 1)
    def _():
        for g in range(N_GROUPS):
            nssm_ref[g * GROUP_WIDTH:(g + 1) * GROUP_WIDTH, :] = ht_ref[g].T


def _ssd_prompt_layer(x, mod, w, i, apply_final):
    b, L, _ = x.shape
    tile = SSD_PROMPT_TILE
    n_tiles = L // tile
    assert tile // CHUNK * N_GROUPS >= D_INNER // PIECE_COLS
    kern = functools.partial(_ssd_prompt_kernel, tile=tile, n_tiles=n_tiles, apply_final=apply_final)
    arrays, specs = _pick(_operands(w, i), _SSD_PROMPT_OPS)
    n_xcols = D_MODEL // LANES
    x_specs = [pl.BlockSpec((None, tile, LANES), lambda s, l, j=j: (s, l, j)) for j in range(n_xcols)]
    return pl.pallas_call(
        kern,
        grid=(b, n_tiles),
        in_specs=x_specs + [pl.BlockSpec((None, 3, D_MODEL), lambda s, l: (s, 0, 0))] + specs,
        out_specs=[pl.BlockSpec((None, tile, D_MODEL), lambda s, l: (s, l, 0)),
                   pl.BlockSpec((None, CONV_WIDTH - 1, CONV_DIM), lambda s, l: (s, 0, 0)),
                   pl.BlockSpec((None, D_INNER, D_STATE), lambda s, l: (s, 0, 0))],
        out_shape=[jax.ShapeDtypeStruct((b, L, D_MODEL), _F32),
                   jax.ShapeDtypeStruct((b, CONV_WIDTH - 1, CONV_DIM), _F32),
                   jax.ShapeDtypeStruct((b, D_INNER, D_STATE), _F32)],
        scratch_shapes=[pltpu.VMEM((N_GROUPS, D_STATE, GROUP_WIDTH), _F32),
                        pltpu.VMEM(((CONV_WIDTH - 1) * SUBLANES, CONV_DIM), _F32),
                        pltpu.VMEM((tile, CONV_DIM), _F32),
                        pltpu.VMEM((tile, D_INNER), _F32),
                        pltpu.VMEM((tile, D_INNER), _F32),
                        pltpu.VMEM((n_xcols, tile, LANES), _F32)],
        compiler_params=_params(2),
        name="ssd_prompt",
    )(*([x] * n_xcols), mod, *arrays)


_SSD_SAMPLE_OPS = ("conv_w", "conv_b", "a_log", "d_skip", "ssd_norm_g", "w_out", "final_g")


def _expansion_constants(seq):
    exps = np.zeros((LANES, N_GROUPS * LANES), np.float32)
    for h in range(N_HEADS):
        g, hh = divmod(h, HEADS_PER_GROUP)
        exps[h, g * LANES + hh * seq:g * LANES + (hh + 1) * seq] = 1.0
    exp64 = np.zeros((LANES, D_INNER), np.float32)
    for h in range(N_HEADS):
        exp64[h, h * HEAD_DIM:(h + 1) * HEAD_DIM] = 1.0
    return jnp.asarray(exps, _BF16), jnp.asarray(exp64, _BF16)


_POOL_OPS = ("norm_g", "w_u", "w_zp", "w_group", "pool_scale", "w_out", "final_g")


def _pool_mix(load_rows, z, wg_ref, ps_ref, pos):
    parts = []
    for g, w in enumerate(POOL_WINDOWS):
        sl = slice(g * POOL_GROUP_DIM, (g + 1) * POOL_GROUP_DIM)
        u_g = load_rows(0, sl)
        win = u_g
        for k in range(1, w):
            win = win + load_rows(k, sl)
        cnt = jnp.minimum(float(w), pos + 1.0)
        pooled = win / cnt - u_g
        mixed = pooled if wg_ref is None else _dot(pooled.astype(_BF16), wg_ref[g])
        parts.append((mixed * ps_ref[:, sl] * _silu(z[:, sl])).astype(_BF16))
    return jnp.concatenate(parts, axis=1)


def _fold_kernel(wu_ref, wg_ref, o_ref):
    o_ref[...] = _dot(wu_ref[...], wg_ref[...]).astype(_BF16)


def _fold_group_mix(w_in, w_group):
    n_layers = w_in.shape[0]
    n_groups = len(POOL_WINDOWS)
    return pl.pallas_call(
        _fold_kernel,
        grid=(n_layers, n_groups),
        in_specs=[pl.BlockSpec((None, D_MODEL, POOL_GROUP_DIM), lambda j, g: (j, 0, g)),
                  pl.BlockSpec((None, None, POOL_GROUP_DIM, POOL_GROUP_DIM), lambda j, g: (j, g, 0, 0))],
        out_specs=pl.BlockSpec((None, D_MODEL, POOL_GROUP_DIM), lambda j, g: (j, 0, g)),
        out_shape=jax.ShapeDtypeStruct((n_layers, D_MODEL, POOL_WIDTH), _BF16),
        compiler_params=_params(2),
        name="fold_group_mix",
    )(w_in, w_group)


def _pool_prompt_kernel(x_ref, mod_ref, g_ref, wv_ref, wu_ref, wz_ref, ps_ref, wout_ref, fin_ref,
                        xo_ref, npool_ref, vp_ref, *, tile, n_tiles, apply_final):
    l = pl.program_id(1)
    hist = POOL_HIST

    @pl.when(l == 0)
    def _():
        vp_ref[0:hist, :] = jnp.zeros((hist, POOL_WIDTH), _F32)

    x = x_ref[...]
    hn = _prenorm(x, g_ref[...], mod_ref[0:1, :], mod_ref[1:2, :]).astype(_BF16)
    vp_ref[hist:hist + tile, :] = _dot(hn, wv_ref[...])
    z = _dot(hn, wz_ref[...])
    pos = (l * tile + lax.broadcasted_iota(jnp.int32, (tile, POOL_GROUP_DIM), 0)).astype(_F32)
    mixed = _pool_mix(lambda k, sl: vp_ref[hist - k:hist - k + tile, sl], z, None, ps_ref, pos)
    xo_ref[...] = _residual(x, _dot(mixed, wout_ref[...]), mod_ref[2:3, :], fin_ref, apply_final)
    vp_ref[hist - POOL_STATE:hist, :] = vp_ref[hist + tile - POOL_STATE:hist + tile, :]

    @pl.when(l == n_tiles - 1)
    def _():
        u_tail = _dot(hn[tile - POOL_HIST:tile, :], wu_ref[...])
        npool_ref[...] = u_tail[POOL_HIST - POOL_STATE:POOL_HIST, :]


_POOL_PROMPT_OPS = ("norm_g", "w_v", "w_u", "w_zp", "pool_scale", "w_out", "final_g")


def _pool_prompt_layer(x, mod, w, i, apply_final):
    b, L, _ = x.shape
    tile = PROMPT_TILE
    n_tiles = L // tile
    kern = functools.partial(_pool_prompt_kernel, tile=tile, n_tiles=n_tiles, apply_final=apply_final)
    arrays, specs = _pick(_operands(w, i), _POOL_PROMPT_OPS)
    return pl.pallas_call(
        kern,
        grid=(b, n_tiles),
        in_specs=[pl.BlockSpec((None, tile, D_MODEL), lambda s, l: (s, l, 0)),
                  pl.BlockSpec((None, 3, D_MODEL), lambda s, l: (s, 0, 0))] + specs,
        out_specs=[pl.BlockSpec((None, tile, D_MODEL), lambda s, l: (s, l, 0)),
                   pl.BlockSpec((None, POOL_STATE, POOL_WIDTH), lambda s, l: (s, 0, 0))],
        out_shape=[jax.ShapeDtypeStruct((b, L, D_MODEL), _F32),
                   jax.ShapeDtypeStruct((b, POOL_STATE, POOL_WIDTH), _F32)],
        scratch_shapes=[pltpu.VMEM((POOL_HIST + tile, POOL_WIDTH), _F32)],
        compiler_params=_params(2),
        name="pool_prompt",
    )(x, mod, *arrays)


def _tm_proj_kernel(*refs, has_dt):
    if has_dt:
        x_ref, mod_ref, g_ref, w_ref, wdt_ref, dtb_ref, alog_ref, o_ref, dt_ref, dec_ref, hn_ref = refs
    else:
        x_ref, mod_ref, g_ref, w_ref, o_ref, hn_ref = refs

    @pl.when(pl.program_id(0) == 0)
    def _():
        hn = _prenorm(x_ref[...], g_ref[...], mod_ref[0:1], mod_ref[1:2])
        hn_ref[...] = hn.reshape(hn_ref.shape).astype(_BF16)
        if has_dt:
            dt = _softplus(_dot(hn_ref[...], wdt_ref[...]) + dtb_ref[...])
            dt_ref[...] = dt
            total = jnp.sum(dt.reshape(x_ref.shape[0], x_ref.shape[1], LANES), axis=0)
            dec_ref[...] = jnp.exp(total * (-jnp.exp(alog_ref[...])))

    o_ref[...] = _dot(hn_ref[...], w_ref[...])


def _tm_proj(x, mod, w, i, weight, n_cols, has_dt):
    seq, b, _ = x.shape
    rows = seq * b
    j = i // N_MIXERS
    ops = _operands(w, i)
    arrays = [x, mod, ops["norm_g"][0], weight]
    specs = [pl.BlockSpec((seq, b, D_MODEL), lambda n: (0, 0, 0)),
             pl.BlockSpec((3, b, D_MODEL), lambda n: (0, 0, 0)),
             ops["norm_g"][1],
             pl.BlockSpec((None, D_MODEL, PROJ_COLS), lambda n: (j, 0, n))]
    out_shape = [jax.ShapeDtypeStruct((rows, n_cols), _F32)]
    out_specs = [pl.BlockSpec((rows, PROJ_COLS), lambda n: (0, n))]
    if has_dt:
        extra, extra_specs = _pick(ops, ("w_dt", "dt_bias", "a_log"))
        arrays += extra
        specs += extra_specs
        out_shape += [jax.ShapeDtypeStruct((rows, LANES), _F32), jax.ShapeDtypeStruct((b, LANES), _F32)]
        out_specs += [pl.BlockSpec((rows, LANES), lambda n: (0, 0)), pl.BlockSpec((b, LANES), lambda n: (0, 0))]
    return pl.pallas_call(
        functools.partial(_tm_proj_kernel, has_dt=has_dt),
        grid=(n_cols // PROJ_COLS,),
        in_specs=specs,
        out_specs=out_specs,
        out_shape=out_shape,
        scratch_shapes=[pltpu.VMEM((rows, D_MODEL), _BF16)],
        compiler_params=_params(1),
        name="sample_proj",
    )(*arrays)


def _tm_ssd_kernel(dec_ref, x_ref, mod_ref, z_ref, xin_ref, bcin_ref, dt_ref,
                   cw_ref, cb_ref, alog_ref, dsk_ref, ng_ref, wout_ref, fin_ref,
                   cprev_ref, hprev_ref, exps_ref, exp64_ref, _conv_alias_ref, _ssm_alias_ref,
                   xo_ref, nconv_ref, nssm_ref,
                   xp_ref, act_ref, mfac_ref, e_ref, xw_ref, y_ref, *, nseq, seq, apply_final):
    rows = nseq * seq
    n_carry = CONV_WIDTH - 1
    xp_ref[0:n_carry] = cprev_ref[...]
    xp_ref[n_carry:n_carry + seq, :, 0:D_INNER] = xin_ref[...]
    xp_ref[n_carry:n_carry + seq, :, D_INNER:CONV_DIM] = bcin_ref[...]
    conv = cb_ref[...]
    for k in range(CONV_WIDTH):
        conv = conv + xp_ref[k:k + seq] * cw_ref[k:k + 1, :]
    act = _silu(conv).reshape(rows, CONV_DIM)
    for jb in range(CONV_DIM // LANES):
        act_ref[jb] = act[:, jb * LANES:(jb + 1) * LANES]
    nconv_ref[...] = xp_ref[seq:seq + n_carry]

    dt3 = dt_ref[...]
    a3 = dt3 * (-jnp.exp(alog_ref[...]))
    slabs = [a3[0]]
    for t in range(1, seq):
        slabs.append(slabs[-1] + a3[t])
    acum3 = jnp.stack(slabs, axis=0)
    w3 = dt3 * jnp.exp(acum3[seq - 1:seq] - acum3)
    acum = acum3.reshape(rows, LANES)

    shape3 = (seq, nseq, N_GROUPS * LANES)
    z3 = _select_lanes(acum, exps_ref[...], 3).reshape(shape3)
    dtz3 = _select_lanes(dt3.reshape(rows, LANES), exps_ref[...], 2).reshape(shape3)
    t3 = lax.broadcasted_iota(jnp.int32, shape3, 0)
    s3 = lax.broadcasted_iota(jnp.int32, shape3, 2) % seq
    diag = t3 == s3
    acum_s = jnp.sum(jnp.where(diag, z3, 0.0), axis=0, keepdims=True)
    dt_s = jnp.sum(jnp.where(diag, dtz3, 0.0), axis=0, keepdims=True)
    mfac = (jnp.exp(jnp.where(s3 <= t3, z3 - acum_s, NEG_BIG)) * dt_s).reshape(rows, N_GROUPS * LANES)
    for g in range(N_GROUPS):
        mfac_ref[g] = mfac[:, g * LANES:(g + 1) * LANES]
    e = _select_lanes(jnp.exp(acum), exp64_ref[...], 2)
    wx = _select_lanes(w3.reshape(rows, LANES), exp64_ref[...], 2)
    for jb in range(D_INNER // LANES):
        lanes = slice(jb * LANES, (jb + 1) * LANES)
        e_ref[jb] = e[:, lanes]
        xw_ref[jb] = act[:, lanes] * wx[:, lanes]

    width = HEADS_PER_GROUP * seq
    bd_r = lax.broadcasted_iota(jnp.int32, (width, GROUP_WIDTH), 0)
    bd_c = lax.broadcasted_iota(jnp.int32, (width, GROUP_WIDTH), 1)
    blockdiag = (bd_r // seq) == (bd_c // HEAD_DIM)
    zeros_b = jnp.zeros((seq, D_STATE), _F32)
    b_blk = D_INNER // LANES
    c_blk = b_blk + N_GROUPS
    zeros_x = jnp.zeros((seq, GROUP_WIDTH), _F32)
    first_seq = pl.program_id(0) * nseq

    def per_seq(s, carry):
        tokens = pl.ds(s, seq, stride=nseq)

        def pair(ref, g):
            return jnp.concatenate([ref[2 * g, tokens, :], ref[2 * g + 1, tokens, :]], axis=1)

        for g in range(N_GROUPS):
            b_g = act_ref[b_blk + g, tokens, :]
            c_bf = act_ref[c_blk + g, tokens, :].astype(_BF16)
            sl = slice(g * GROUP_WIDTH, (g + 1) * GROUP_WIDTH)
            xs_g = pair(act_ref, g)
            cbx = _dot_nt(c_bf, jnp.concatenate([b_g] * HEADS_PER_GROUP, axis=0).astype(_BF16))
            mp = (cbx * mfac_ref[g, tokens, :][:, 0:width]).astype(_BF16)
            rhs = jnp.where(blockdiag, jnp.concatenate([xs_g] * HEADS_PER_GROUP, axis=0), 0.0).astype(_BF16)
            h0 = hprev_ref[s, sl, :]
            e_g = pair(e_ref, g)
            y_g = _dot(mp, rhs) + _dot_nt(c_bf, h0.astype(_BF16)) * e_g + dsk_ref[:, sl] * xs_g
            y_ref[2 * g, tokens, :] = y_g[:, 0:LANES]
            y_ref[2 * g + 1, tokens, :] = y_g[:, LANES:GROUP_WIDTH]
            lhs_t = jnp.concatenate([pair(xw_ref, g), zeros_x], axis=0).astype(_BF16)
            rhs_s = jnp.concatenate([b_g, zeros_b], axis=0).astype(_BF16)
            ds = _dot_tn(lhs_t, rhs_s)
            for hh in range(HEADS_PER_GROUP):
                decay = dec_ref[first_seq + s, g * HEADS_PER_GROUP + hh]
                hr = slice(hh * HEAD_DIM, (hh + 1) * HEAD_DIM)
                nssm_ref[s, g * GROUP_WIDTH + hh * HEAD_DIM:g * GROUP_WIDTH + (hh + 1) * HEAD_DIM, :] = (
                    h0[hr, :] * decay + ds[hr, :])
        return carry

    lax.fori_loop(0, nseq, per_seq, 0, unroll=SSD_SEQ_UNROLL)

    z = z_ref[...].reshape(rows, D_INNER)
    parts = []
    for g in range(N_GROUPS):
        sl = slice(g * GROUP_WIDTH, (g + 1) * GROUP_WIDTH)
        gated = jnp.concatenate([y_ref[2 * g], y_ref[2 * g + 1]], axis=1) * _silu(z[:, sl])
        ms = jnp.mean(gated * gated, axis=-1, keepdims=True)
        parts.append((gated * lax.rsqrt(ms + EPS) * ng_ref[:, sl]).astype(_BF16))
    out = _dot(jnp.concatenate(parts, axis=1), wout_ref[...]).reshape(seq, nseq, D_MODEL)
    xo_ref[...] = _residual(x_ref[...], out, mod_ref[2:3], fin_ref, apply_final)


def _tm_ssd_layer(x, mod, w, i, state_conv_t, state_ssm, conv_acc, ssm_acc, apply_final):
    seq, b, _ = x.shape
    j = i // N_MIXERS
    nseq = SSD_SEQ_BLOCK
    rows = nseq * seq
    n_layers = state_ssm.shape[0]
    proj, dt, decay = _tm_proj(x, mod, w, i, w["ssd_w_in"], D_INNER + CONV_DIM, True)
    proj = proj.reshape(seq, b, D_INNER + CONV_DIM)
    dt = dt.reshape(seq, b, LANES)
    kern = functools.partial(_tm_ssd_kernel, nseq=nseq, seq=seq, apply_final=apply_final)
    arrays, specs = _pick(_operands(w, i), _SSD_SAMPLE_OPS)
    exps, exp64 = _expansion_constants(seq)
    col_block = lambda n: pl.BlockSpec((seq, nseq, WEIGHT_COLS), lambda s: (0, s, n))
    in_specs = ([pl.BlockSpec(memory_space=pltpu.SMEM),
                 pl.BlockSpec((seq, nseq, D_MODEL), lambda s: (0, s, 0)),
                 pl.BlockSpec((3, nseq, D_MODEL), lambda s: (0, s, 0)),
                 col_block(0), col_block(1), col_block(2),
                 pl.BlockSpec((seq, nseq, LANES), lambda s: (0, s, 0))]
                + specs
                + [pl.BlockSpec((None, CONV_WIDTH - 1, nseq, CONV_DIM), lambda s: (j, 0, s, 0)),
                   pl.BlockSpec((None, nseq, D_INNER, D_STATE), lambda s: (j, s, 0, 0)),
                   _wspec(exps.shape, (0, 0)), _wspec(exp64.shape, (0, 0))])
    args = [decay, x, mod, proj, proj, proj, dt, *arrays, state_conv_t, state_ssm, exps, exp64]
    n_in = len(args)
    aliases = {}
    if conv_acc is not None:
        in_specs += [pl.BlockSpec(memory_space=pl.ANY), pl.BlockSpec(memory_space=pl.ANY)]
        args += [conv_acc, ssm_acc]
        aliases = {n_in: 1, n_in + 1: 2}
        kern_fn = kern
    else:
        kern_fn = lambda *refs: kern(*refs[:n_in], None, None, *refs[n_in:])
    slab = lambda width: pltpu.VMEM((width // LANES, rows, LANES), _F32)
    return pl.pallas_call(
        kern_fn,
        grid=(b // nseq,),
        in_specs=in_specs,
        out_specs=[pl.BlockSpec((seq, nseq, D_MODEL), lambda s: (0, s, 0)),
                   pl.BlockSpec((None, CONV_WIDTH - 1, nseq, CONV_DIM), lambda s: (j, 0, s, 0)),
                   pl.BlockSpec((None, nseq, D_INNER, D_STATE), lambda s: (j, s, 0, 0))],
        out_shape=[jax.ShapeDtypeStruct((seq, b, D_MODEL), _F32),
                   jax.ShapeDtypeStruct((n_layers, CONV_WIDTH - 1, b, CONV_DIM), _F32),
                   jax.ShapeDtypeStruct((n_layers, b, D_INNER, D_STATE), _F32)],
        scratch_shapes=[pltpu.VMEM((seq + CONV_WIDTH - 1, nseq, CONV_DIM), _F32),
                        slab(CONV_DIM),
                        slab(N_GROUPS * LANES),
                        slab(D_INNER),
                        slab(D_INNER),
                        slab(D_INNER)],
        input_output_aliases=aliases,
        compiler_params=_params(1),
        name="ssd_sample",
    )(*args)


def _tm_pool_kernel(x_ref, mod_ref, g_ref, wu_ref, wz_ref, wg_ref, ps_ref, wout_ref, fin_ref, prev_ref,
                    _alias_ref, xo_ref, npool_ref, up_ref, *, nseq, seq, start_pos, apply_final):
    rows = nseq * seq
    x3 = x_ref[...]
    hn = _prenorm(x3, g_ref[...], mod_ref[0:1], mod_ref[1:2]).reshape(rows, D_MODEL).astype(_BF16)
    up_ref[0:POOL_STATE] = prev_ref[...]
    up_ref[POOL_STATE:POOL_STATE + seq] = _dot(hn, wu_ref[...]).reshape(seq, nseq, POOL_WIDTH)
    z = _dot(hn, wz_ref[...])
    pos3 = start_pos + lax.broadcasted_iota(jnp.int32, (seq, nseq, POOL_GROUP_DIM), 0)
    pos = pos3.astype(_F32).reshape(rows, POOL_GROUP_DIM)
    mixed = _pool_mix(
        lambda k, sl: up_ref[POOL_STATE - k:POOL_STATE - k + seq, :, sl].reshape(rows, POOL_GROUP_DIM),
        z, wg_ref, ps_ref, pos)
    out = _dot(mixed, wout_ref[...]).reshape(seq, nseq, D_MODEL)
    xo_ref[...] = _residual(x3, out, mod_ref[2:3], fin_ref, apply_final)
    npool_ref[...] = up_ref[seq:seq + POOL_STATE]


def _tm_pool_layer(x, mod, w, i, state_pool_t, pool_acc, start_pos, apply_final):
    seq, b, _ = x.shape
    j = i // N_MIXERS
    nseq = POOL_SEQ_BLOCK
    n_layers = state_pool_t.shape[0]
    kern = functools.partial(_tm_pool_kernel, nseq=nseq, seq=seq, start_pos=start_pos,
                             apply_final=apply_final)
    arrays, specs = _pick(_operands(w, i), _POOL_OPS)
    state_spec = pl.BlockSpec((None, POOL_STATE, nseq, POOL_WIDTH), lambda s: (j, 0, s, 0))
    in_specs = ([pl.BlockSpec((seq, nseq, D_MODEL), lambda s: (0, s, 0)),
                 pl.BlockSpec((3, nseq, D_MODEL), lambda s: (0, s, 0))] + specs + [state_spec])
    args = [x, mod, *arrays, state_pool_t]
    n_in = len(args)
    aliases = {}
    if pool_acc is not None:
        in_specs.append(pl.BlockSpec(memory_space=pl.ANY))
        args.append(pool_acc)
        aliases = {n_in: 1}
        kern_fn = kern
    else:
        kern_fn = lambda *refs: kern(*refs[:n_in], None, *refs[n_in:])
    return pl.pallas_call(
        kern_fn,
        grid=(b // nseq,),
        in_specs=in_specs,
        out_specs=[pl.BlockSpec((seq, nseq, D_MODEL), lambda s: (0, s, 0)), state_spec],
        out_shape=[jax.ShapeDtypeStruct((seq, b, D_MODEL), _F32),
                   jax.ShapeDtypeStruct((n_layers, POOL_STATE, b, POOL_WIDTH), _F32)],
        scratch_shapes=[pltpu.VMEM((POOL_STATE + seq, nseq, POOL_WIDTH), _F32)],
        input_output_aliases=aliases,
        compiler_params=_params(1),
        name="pool_sample",
    )(*args)


def _prepare_weights(norm_g, ssd_w_in, ssd_conv_w, ssd_conv_b, ssd_dt_bias, ssd_a_log, ssd_d, ssd_norm_g,
                     ssd_w_out, pool_w_in, pool_w_group, pool_scale, pool_w_out, final_norm_g):
    n_ssd = ssd_w_in.shape[0]
    n_pool = pool_w_in.shape[0]
    pad = LANES - N_HEADS
    pool_w_in_bf = pool_w_in.astype(_BF16)
    pool_w_group_bf = pool_w_group.astype(_BF16)
    return {
        "pool_w_v": _fold_group_mix(pool_w_in_bf, pool_w_group_bf),
        "norm_g": norm_g.reshape(DEPTH, 1, D_MODEL),
        "final_g": final_norm_g.reshape(1, D_MODEL),
        "ssd_w_in": ssd_w_in.astype(_BF16),
        "ssd_w_dt": jnp.pad(ssd_w_in[:, :, D_INNER + CONV_DIM:], ((0, 0), (0, 0), (0, pad))).astype(_BF16),
        "ssd_conv_w": ssd_conv_w,
        "ssd_conv_b": ssd_conv_b.reshape(n_ssd, 1, CONV_DIM),
        "ssd_dt_bias": jnp.pad(ssd_dt_bias, ((0, 0), (0, pad))).reshape(n_ssd, 1, LANES),
        "ssd_a_log": jnp.pad(ssd_a_log, ((0, 0), (0, pad))).reshape(n_ssd, 1, LANES),
        "ssd_d": jnp.repeat(ssd_d, HEAD_DIM, axis=1).reshape(n_ssd, 1, D_INNER),
        "ssd_norm_g": ssd_norm_g.reshape(n_ssd, 1, D_INNER),
        "ssd_w_out": ssd_w_out.astype(_BF16),
        "pool_w_in": pool_w_in_bf,
        "pool_w_group": pool_w_group_bf,
        "pool_scale": pool_scale.reshape(n_pool, 1, POOL_WIDTH),
        "pool_w_out": pool_w_out.astype(_BF16),
    }


def kernel(x_prompt, x_sample, state_ssm, state_conv, state_pool, c_prompt, c_sample, ada_w, ada_b, norm_g,
           ssd_w_in, ssd_conv_w, ssd_conv_b, ssd_dt_bias, ssd_a_log, ssd_d, ssd_norm_g, ssd_w_out, pool_w_in,
           pool_w_group, pool_scale, pool_w_out, final_norm_g):
    b_p = x_prompt.shape[0]
    b_s = x_sample.shape[0]
    n_ssd = state_ssm.shape[0]
    w = _prepare_weights(norm_g, ssd_w_in, ssd_conv_w, ssd_conv_b, ssd_dt_bias, ssd_a_log, ssd_d, ssd_norm_g,
                         ssd_w_out, pool_w_in, pool_w_group, pool_scale, pool_w_out, final_norm_g)
    mod = _modulation(jnp.concatenate([c_prompt, c_sample], axis=0), ada_w, ada_b)
    mod_p = mod[:, :b_p].reshape(DEPTH, b_p, 3, D_MODEL)
    mod_s = mod[:, b_p:].reshape(DEPTH, b_s, 3, D_MODEL).transpose(0, 2, 1, 3)
    ssm_in = state_ssm.reshape(n_ssd, b_s, D_INNER, D_STATE)
    conv_in = state_conv.transpose(0, 2, 1, 3)
    pool_in = state_pool.transpose(0, 2, 1, 3)

    xp, xs = x_prompt, x_sample.transpose(1, 0, 2)
    ssm_p, conv_p, pool_p = [], [], []
    ssm_s = conv_s = pool_s = None
    for i in range(DEPTH):
        last = i == DEPTH - 1
        if i % N_MIXERS == 0:
            xp, cv, st = _ssd_prompt_layer(xp, mod_p[i], w, i, last)
            conv_p.append(cv)
            ssm_p.append(st)
            xs, conv_s, ssm_s = _tm_ssd_layer(xs, mod_s[i], w, i, conv_in, ssm_in, conv_s, ssm_s, last)
        else:
            xp, ps = _pool_prompt_layer(xp, mod_p[i], w, i, last)
            pool_p.append(ps)
            xs, pool_s = _tm_pool_layer(xs, mod_s[i], w, i, pool_in, pool_s, PAST_LEN, last)

    head_shape = (N_HEADS, HEAD_DIM, D_STATE)
    return (xp, xs.transpose(1, 0, 2),
            jnp.stack(ssm_p).reshape(n_ssd, b_p, *head_shape),
            jnp.stack(conv_p), jnp.stack(pool_p),
            ssm_s.reshape(n_ssd, b_s, *head_shape),
            conv_s.transpose(0, 2, 1, 3), pool_s.transpose(0, 2, 1, 3))
```

```python
import functools

import numpy as np
import jax
import jax.numpy as jnp
from jax import lax
from jax.experimental import pallas as pl
from jax.experimental.pallas import tpu as pltpu

D_MODEL = 1024
DEPTH = 4
PAST_LEN = 16384
N_MIXERS = 2
D_INNER = 2048
HEAD_DIM = 64
N_HEADS = 32
N_GROUPS = 8
HEADS_PER_GROUP = 4
D_STATE = 128
CONV_WIDTH = 4
CONV_DIM = D_INNER + 2 * N_GROUPS * D_STATE
GROUP_WIDTH = HEADS_PER_GROUP * HEAD_DIM
POOL_WIDTH = 2048
POOL_WINDOWS = (2, 4, 8, 16)
POOL_GROUP_DIM = 512
POOL_STATE = 15
EPS = 1e-6

LANES = 128
SUBLANES = 8
CHUNK = 128
N_PHASES = CHUNK // SUBLANES
PROMPT_TILE = 512
SSD_PROMPT_TILE = 512
SSD_SEQ_BLOCK = 8
SSD_SEQ_UNROLL = 8
POOL_SEQ_BLOCK = 16
POOL_HIST = 2 * SUBLANES
WEIGHT_COLS = 2048
PIECE_COLS = 256
PROJ_COLS = 1024
VMEM_LIMIT = 60 * 1024 * 1024
NEG_BIG = -1e30
LOG2E = 1.4426950408889634

_F32 = jnp.float32
_BF16 = jnp.bfloat16
_HI = lax.Precision.HIGHEST


def _silu(v):
    h = 0.5 * v
    return h + h * jnp.tanh(h)


def _softplus(v):
    return jnp.maximum(v, 0.0) + jnp.log(1.0 + jnp.exp(-jnp.abs(v)))


def _dot(a, b):
    return jnp.dot(a, b, preferred_element_type=_F32)


def _dot_exact(a, b):
    return jnp.dot(a, b, precision=_HI, preferred_element_type=_F32)


def _dot_nt(a, b):
    return lax.dot_general(a, b, (((1,), (1,)), ((), ())), preferred_element_type=_F32)


def _dot_tn(a, b):
    return lax.dot_general(a, b, (((0,), (0,)), ((), ())), preferred_element_type=_F32)


def _select_lanes(v, onehot_bf, terms):
    acc = None
    rest = v
    for _ in range(terms):
        piece = rest.astype(_BF16)
        part = _dot(piece, onehot_bf)
        acc = part if acc is None else acc + part
        rest = rest - piece.astype(_F32)
    return acc


def _prenorm(x, g, shift, scale):
    ms = jnp.mean(x * x, axis=-1, keepdims=True)
    return x * lax.rsqrt(ms + EPS) * g * (1.0 + scale) + shift


def _residual(x, out, gate, fin_ref, apply_final):
    y = x + (1.0 + gate) * out
    if apply_final:
        ms = jnp.mean(y * y, axis=-1, keepdims=True)
        y = y * lax.rsqrt(ms + EPS) * fin_ref[...]
    return y


def _group_rmsnorm_gate(y_ref, z_ref, ng_ref):
    parts = []
    for g in range(N_GROUPS):
        sl = slice(g * GROUP_WIDTH, (g + 1) * GROUP_WIDTH)
        gated = y_ref[:, sl] * _silu(z_ref[:, sl])
        ms = jnp.mean(gated * gated, axis=-1, keepdims=True)
        parts.append((gated * lax.rsqrt(ms + EPS) * ng_ref[:, sl]).astype(_BF16))
    return jnp.concatenate(parts, axis=1)


def _wspec(block_shape, index):
    return pl.BlockSpec(block_shape, lambda *_: index, pipeline_mode=pl.Buffered(1))


def _params(n_axes):
    return pltpu.CompilerParams(dimension_semantics=("arbitrary",) * n_axes, vmem_limit_bytes=VMEM_LIMIT)


def _operands(w, i):
    j = i // N_MIXERS
    wcols = (None, D_MODEL, WEIGHT_COLS)
    ops = {"norm_g": (w["norm_g"], _wspec((None, 1, D_MODEL), (i, 0, 0))),
           "final_g": (w["final_g"], _wspec((1, D_MODEL), (0, 0)))}
    if i % N_MIXERS == 0:
        ops.update(
            w_z=(w["ssd_w_in"], _wspec(wcols, (j, 0, 0))),
            w_x=(w["ssd_w_in"], _wspec(wcols, (j, 0, 1))),
            w_bc=(w["ssd_w_in"], _wspec(wcols, (j, 0, 2))),
            w_dt=(w["ssd_w_dt"], _wspec((None, D_MODEL, LANES), (j, 0, 0))),
            conv_w=(w["ssd_conv_w"], _wspec((None, CONV_WIDTH, CONV_DIM), (j, 0, 0))),
            conv_b=(w["ssd_conv_b"], _wspec((None, 1, CONV_DIM), (j, 0, 0))),
            dt_bias=(w["ssd_dt_bias"], _wspec((None, 1, LANES), (j, 0, 0))),
            a_log=(w["ssd_a_log"], _wspec((None, 1, LANES), (j, 0, 0))),
            d_skip=(w["ssd_d"], _wspec((None, 1, D_INNER), (j, 0, 0))),
            ssd_norm_g=(w["ssd_norm_g"], _wspec((None, 1, D_INNER), (j, 0, 0))),
            w_out=(w["ssd_w_out"], _wspec((None, D_INNER, D_MODEL), (j, 0, 0))),
        )
    else:
        ops.update(
            w_v=(w["pool_w_v"], _wspec(wcols, (j, 0, 0))),
            w_u=(w["pool_w_in"], _wspec(wcols, (j, 0, 0))),
            w_zp=(w["pool_w_in"], _wspec(wcols, (j, 0, 1))),
            w_group=(w["pool_w_group"],
                     _wspec((None, len(POOL_WINDOWS), POOL_GROUP_DIM, POOL_GROUP_DIM), (j, 0, 0, 0))),
            pool_scale=(w["pool_scale"], _wspec((None, 1, POOL_WIDTH), (j, 0, 0))),
            w_out=(w["pool_w_out"], _wspec((None, POOL_WIDTH, D_MODEL), (j, 0, 0))),
        )
    return ops


def _pick(ops, names):
    return [ops[n][0] for n in names], [ops[n][1] for n in names]


def _mod_kernel(c_ref, w_ref, b_ref, o_ref):
    sc = _silu(c_ref[...]).astype(_BF16)
    o_ref[...] = _dot(sc, w_ref[...].astype(_BF16)) + b_ref[...]


def _modulation(c_all, ada_w, ada_b):
    n = c_all.shape[0]
    tn = 1024
    return pl.pallas_call(
        _mod_kernel,
        grid=(DEPTH, 3 * D_MODEL // tn),
        in_specs=[
            pl.BlockSpec((n, D_MODEL), lambda i, j: (0, 0)),
            pl.BlockSpec((None, D_MODEL, tn), lambda i, j: (i, 0, j)),
            pl.BlockSpec((None, 1, tn), lambda i, j: (i, 0, j)),
        ],
        out_specs=pl.BlockSpec((None, n, tn), lambda i, j: (i, 0, j)),
        out_shape=jax.ShapeDtypeStruct((DEPTH, n, 3 * D_MODEL), _F32),
        compiler_params=_params(2),
        name="adaln_mod",
    )(c_all, ada_w, ada_b.reshape(DEPTH, 1, 3 * D_MODEL))


_SSD_PROMPT_OPS = ("norm_g", "w_z", "w_x", "w_bc", "w_dt", "conv_w", "conv_b", "dt_bias", "a_log", "d_skip",
                   "ssd_norm_g", "w_out", "final_g")


def _ssd_prompt_kernel(*refs, tile, n_tiles, apply_final):
    n_xcols = D_MODEL // LANES
    x_refs = refs[:n_xcols]
    (mod_ref, g_ref, wz_ref, wx_ref, wbc_ref, wdt_ref, cw_ref, cb_ref, dtb_ref, alog_ref, dsk_ref, ng_ref,
     wout_ref, fin_ref, xo_ref, nconv_ref, nssm_ref,
     ht_ref, carry_ref, act_ref, z_ref, y_ref, stage_ref) = refs[n_xcols:]
    l = pl.program_id(1)
    n_chunks = tile // CHUNK
    n_carry = CONV_WIDTH - 1

    @pl.when(l == 0)
    def _():
        ht_ref[...] = jnp.zeros_like(ht_ref)
        carry_ref[...] = jnp.zeros_like(carry_ref)

    def load_x():
        return jnp.concatenate(
            [jnp.concatenate([xr[pl.ds(c * CHUNK + r, SUBLANES, stride=N_PHASES), :] for xr in x_refs], axis=1)
             for c in range(n_chunks) for r in range(N_PHASES)], axis=0)

    x = load_x()
    hn = _prenorm(x, g_ref[...], mod_ref[0:1, :], mod_ref[1:2, :]).astype(_BF16)
    dt = _softplus(_dot(hn, wdt_ref[...]) + dtb_ref[...])

    last = (N_PHASES - n_carry) * SUBLANES
    for k in range(CONV_DIM // PIECE_COLS):
        cols = slice(k * PIECE_COLS, (k + 1) * PIECE_COLS)
        half = D_INNER // PIECE_COLS
        w_ref, kw = (wx_ref, k) if k < half else (wbc_ref, k - half)
        xb = _dot(hn, w_ref[:, kw * PIECE_COLS:(kw + 1) * PIECE_COLS])
        prev = carry_ref[:, cols]
        for c in range(n_chunks):
            cur = xb[c * CHUNK:(c + 1) * CHUNK, :]
            tail = cur[last:CHUNK, :]
            wrapped = [jnp.concatenate([prev[j * SUBLANES + SUBLANES - 1:(j + 1) * SUBLANES, :],
                                        tail[j * SUBLANES:(j + 1) * SUBLANES - 1, :]], axis=0)
                       for j in range(n_carry)]
            ext = jnp.concatenate(wrapped + [cur], axis=0)
            conv = cb_ref[:, cols]
            for kk in range(CONV_WIDTH):
                conv = conv + ext[kk * SUBLANES:kk * SUBLANES + CHUNK, :] * cw_ref[kk:kk + 1, cols]
            act_ref[c * CHUNK:(c + 1) * CHUNK, cols] = _silu(conv)
            prev = tail
        carry_ref[:, cols] = prev
        for j in range(n_carry):
            nconv_ref[j:j + 1, cols] = prev[(j + 1) * SUBLANES - 1:(j + 1) * SUBLANES, :]

    row = lax.broadcasted_iota(jnp.int32, (CHUNK, CHUNK), 0)
    col = lax.broadcasted_iota(jnp.int32, (CHUNK, CHUNK), 1)
    token = lambda p: (p % SUBLANES) * N_PHASES + p // SUBLANES
    causal = token(col) <= token(row)
    tril = causal.astype(_F32)
    head_of_lane = lax.broadcasted_iota(jnp.int32, (CHUNK, GROUP_WIDTH), 1) // HEAD_DIM
    low_half = lax.broadcasted_iota(jnp.int32, (CHUNK, LANES), 1) < HEAD_DIM
    neg_a = -jnp.exp(alog_ref[...])

    def decay_stage(c):
        dt_c = dt[c * CHUNK:(c + 1) * CHUNK, :]
        acum = _dot_exact(tril, dt_c * neg_a)
        acum2 = acum * LOG2E
        w_c = dt_c * jnp.exp(acum[CHUNK - 1:CHUNK, :] - acum)
        srow_t = (jnp.log(dt_c) * LOG2E - acum2).T
        return acum2, srow_t, w_c.T

    def cb_stage(c, g):
        rows = slice(c * CHUNK, (c + 1) * CHUNK)
        b_g = act_ref[rows, D_INNER + g * D_STATE:D_INNER + (g + 1) * D_STATE]
        c_lo = D_INNER + (N_GROUPS + g) * D_STATE
        c_bf = act_ref[rows, c_lo:c_lo + D_STATE].astype(_BF16)
        bt_g = b_g.T
        return c_bf, bt_g, _dot(c_bf, bt_g.astype(_BF16))

    def operand_stage(c, g, decay, cbs):
        acum2, srow_t, w_t = decay
        _, bt_g, cb = cbs
        xs_g = act_ref[c * CHUNK:(c + 1) * CHUNK, g * GROUP_WIDTH:(g + 1) * GROUP_WIDTH]
        rhs = jnp.concatenate(
            [jnp.where(head_of_lane == hh, xs_g, 0.0).astype(_BF16) for hh in range(HEADS_PER_GROUP)],
            axis=0)
        m_parts, w_parts, e_cols = [], [], []
        for hh in range(HEADS_PER_GROUP):
            h = g * HEADS_PER_GROUP + hh
            acum_bc = jnp.broadcast_to(acum2[:, h:h + 1], (CHUNK, CHUNK))
            seg = acum_bc + srow_t[h:h + 1, :]
            m_parts.append((cb * jnp.exp2(jnp.where(causal, seg, NEG_BIG))).astype(_BF16))
            w_parts.append((bt_g * w_t[h:h + 1, :]).astype(_BF16))
            e_cols.append(jnp.exp2(acum_bc))
        lhs = jnp.concatenate(
            [jnp.concatenate(m_parts, axis=1), jnp.concatenate(w_parts, axis=1)], axis=0)
        e_g = jnp.concatenate(
            [jnp.where(low_half, e_cols[0], e_cols[1]), jnp.where(low_half, e_cols[2], e_cols[3])],
            axis=1)
        return lhs, rhs, e_g

    def output_stage(c, g, cbs, e_g, out):
        rows = slice(c * CHUNK, (c + 1) * CHUNK)
        sl = slice(g * GROUP_WIDTH, (g + 1) * GROUP_WIDTH)
        ht_g = ht_ref[g]
        y_inter = _dot(cbs[0], ht_g.astype(_BF16))
        y_ref[rows, sl] = out[0:CHUNK, :] + y_inter * e_g + dsk_ref[:, sl] * act_ref[rows, sl]
        ht_ref[g] = ht_g * e_g[CHUNK - 1:CHUNK, :] + out[CHUNK:2 * CHUNK, :]

    def gate_proj_piece(p):
        if p < D_INNER // PIECE_COLS:
            zc = slice(p * PIECE_COLS, (p + 1) * PIECE_COLS)
            z_ref[:, zc] = _dot(hn, wz_ref[:, zc])

    groups = range(N_GROUPS)
    decays = [decay_stage(c) for c in range(n_chunks)]
    cbs = [[cb_stage(c, g) for g in groups] for c in range(n_chunks)]
    operands = [operand_stage(0, g, decays[0], cbs[0][g]) for g in groups]
    for c in range(n_chunks):
        outs, nxt = [], []
        for g in groups:
            lhs, rhs, _ = operands[g]
            outs.append(_dot(lhs, rhs))
            if c + 1 < n_chunks:
                nxt.append(operand_stage(c + 1, g, decays[c + 1], cbs[c + 1][g]))
        for g in groups:
            output_stage(c, g, cbs[c][g], operands[g][2], outs[g])
            gate_proj_piece(c * N_GROUPS + g)
        operands = nxt

    gn = _group_rmsnorm_gate(y_ref, z_ref, ng_ref)
    xo = _residual(x, _dot(gn, wout_ref[...]), mod_ref[2:3, :], fin_ref, apply_final)
    for c in range(n_chunks):
        for r in range(N_PHASES):
            p0 = c * CHUNK + r * SUBLANES
            for j in range(n_xcols):
                stage_ref[j, pl.ds(c * CHUNK + r, SUBLANES, stride=N_PHASES), :] = (
                    xo[p0:p0 + SUBLANES, j * LANES:(j + 1) * LANES])
    for j in range(n_xcols):
        xo_ref[:, j * LANES:(j + 1) * LANES] = stage_ref[j]

    @pl.when(l == n_tiles - 1)
    def _():
        for g in range(N_GROUPS):
            nssm_ref[g * GROUP_WIDTH:(g + 1) * GROUP_WIDTH, :] = ht_ref[g].T


def _ssd_prompt_layer(x, mod, w, i, apply_final):
    b, L, _ = x.shape
    tile = SSD_PROMPT_TILE
    n_tiles = L // tile
    assert tile // CHUNK * N_GROUPS >= D_INNER // PIECE_COLS
    kern = functools.partial(_ssd_prompt_kernel, tile=tile, n_tiles=n_tiles, apply_final=apply_final)
    arrays, specs = _pick(_operands(w, i), _SSD_PROMPT_OPS)
    n_xcols = D_MODEL // LANES
    x_specs = [pl.BlockSpec((None, tile, LANES), lambda s, l, j=j: (s, l, j)) for j in range(n_xcols)]
    return pl.pallas_call(
        kern,
        grid=(b, n_tiles),
        in_specs=x_specs + [pl.BlockSpec((None, 3, D_MODEL), lambda s, l: (s, 0, 0))] + specs,
        out_specs=[pl.BlockSpec((None, tile, D_MODEL), lambda s, l: (s, l, 0)),
                   pl.BlockSpec((None, CONV_WIDTH - 1, CONV_DIM), lambda s, l: (s, 0, 0)),
                   pl.BlockSpec((None, D_INNER, D_STATE), lambda s, l: (s, 0, 0))],
        out_shape=[jax.ShapeDtypeStruct((b, L, D_MODEL), _F32),
                   jax.ShapeDtypeStruct((b, CONV_WIDTH - 1, CONV_DIM), _F32),
                   jax.ShapeDtypeStruct((b, D_INNER, D_STATE), _F32)],
        scratch_shapes=[pltpu.VMEM((N_GROUPS, D_STATE, GROUP_WIDTH), _F32),
                        pltpu.VMEM(((CONV_WIDTH - 1) * SUBLANES, CONV_DIM), _F32),
                        pltpu.VMEM((tile, CONV_DIM), _F32),
                        pltpu.VMEM((tile, D_INNER), _F32),
                        pltpu.VMEM((tile, D_INNER), _F32),
                        pltpu.VMEM((n_xcols, tile, LANES), _F32)],
        compiler_params=_params(2),
        name="ssd_prompt",
    )(*([x] * n_xcols), mod, *arrays)


_SSD_SAMPLE_OPS = ("conv_w", "conv_b", "a_log", "d_skip", "ssd_norm_g", "w_out", "final_g")


def _expansion_constants(seq):
    exps = np.zeros((LANES, N_GROUPS * LANES), np.float32)
    for h in range(N_HEADS):
        g, hh = divmod(h, HEADS_PER_GROUP)
        exps[h, g * LANES + hh * seq:g * LANES + (hh + 1) * seq] = 1.0
    exp64 = np.zeros((LANES, D_INNER), np.float32)
    for h in range(N_HEADS):
        exp64[h, h * HEAD_DIM:(h + 1) * HEAD_DIM] = 1.0
    return jnp.asarray(exps, _BF16), jnp.asarray(exp64, _BF16)


_POOL_OPS = ("norm_g", "w_u", "w_zp", "w_group", "pool_scale", "w_out", "final_g")


def _pool_mix(load_rows, z, wg_ref, ps_ref, pos):
    parts = []
    for g, w in enumerate(POOL_WINDOWS):
        sl = slice(g * POOL_GROUP_DIM, (g + 1) * POOL_GROUP_DIM)
        u_g = load_rows(0, sl)
        win = u_g
        for k in range(1, w):
            win = win + load_rows(k, sl)
        cnt = jnp.minimum(float(w), pos + 1.0)
        pooled = win / cnt - u_g
        mixed = pooled if wg_ref is None else _dot(pooled.astype(_BF16), wg_ref[g])
        parts.append((mixed * ps_ref[:, sl] * _silu(z[:, sl])).astype(_BF16))
    return jnp.concatenate(parts, axis=1)


def _fold_kernel(wu_ref, wg_ref, o_ref):
    o_ref[...] = _dot(wu_ref[...], wg_ref[...]).astype(_BF16)


def _fold_group_mix(w_in, w_group):
    n_layers = w_in.shape[0]
    n_groups = len(POOL_WINDOWS)
    return pl.pallas_call(
        _fold_kernel,
        grid=(n_layers, n_groups),
        in_specs=[pl.BlockSpec((None, D_MODEL, POOL_GROUP_DIM), lambda j, g: (j, 0, g)),
                  pl.BlockSpec((None, None, POOL_GROUP_DIM, POOL_GROUP_DIM), lambda j, g: (j, g, 0, 0))],
        out_specs=pl.BlockSpec((None, D_MODEL, POOL_GROUP_DIM), lambda j, g: (j, 0, g)),
        out_shape=jax.ShapeDtypeStruct((n_layers, D_MODEL, POOL_WIDTH), _BF16),
        compiler_params=_params(2),
        name="fold_group_mix",
    )(w_in, w_group)


def _pool_prompt_kernel(x_ref, mod_ref, g_ref, wv_ref, wu_ref, wz_ref, ps_ref, wout_ref, fin_ref,
                        xo_ref, npool_ref, vp_ref, *, tile, n_tiles, apply_final):
    l = pl.program_id(1)
    hist = POOL_HIST

    @pl.when(l == 0)
    def _():
        vp_ref[0:hist, :] = jnp.zeros((hist, POOL_WIDTH), _F32)

    x = x_ref[...]
    hn = _prenorm(x, g_ref[...], mod_ref[0:1, :], mod_ref[1:2, :]).astype(_BF16)
    vp_ref[hist:hist + tile, :] = _dot(hn, wv_ref[...])
    z = _dot(hn, wz_ref[...])
    pos = (l * tile + lax.broadcasted_iota(jnp.int32, (tile, POOL_GROUP_DIM), 0)).astype(_F32)
    mixed = _pool_mix(lambda k, sl: vp_ref[hist - k:hist - k + tile, sl], z, None, ps_ref, pos)
    xo_ref[...] = _residual(x, _dot(mixed, wout_ref[...]), mod_ref[2:3, :], fin_ref, apply_final)
    vp_ref[hist - POOL_STATE:hist, :] = vp_ref[hist + tile - POOL_STATE:hist + tile, :]

    @pl.when(l == n_tiles - 1)
    def _():
        u_tail = _dot(hn[tile - POOL_HIST:tile, :], wu_ref[...])
        npool_ref[...] = u_tail[POOL_HIST - POOL_STATE:POOL_HIST, :]


_POOL_PROMPT_OPS = ("norm_g", "w_v", "w_u", "w_zp", "pool_scale", "w_out", "final_g")


def _pool_prompt_layer(x, mod, w, i, apply_final):
    b, L, _ = x.shape
    tile = PROMPT_TILE
    n_tiles = L // tile
    kern = functools.partial(_pool_prompt_kernel, tile=tile, n_tiles=n_tiles, apply_final=apply_final)
    arrays, specs = _pick(_operands(w, i), _POOL_PROMPT_OPS)
    return pl.pallas_call(
        kern,
        grid=(b, n_tiles),
        in_specs=[pl.BlockSpec((None, tile, D_MODEL), lambda s, l: (s, l, 0)),
                  pl.BlockSpec((None, 3, D_MODEL), lambda s, l: (s, 0, 0))] + specs,
        out_specs=[pl.BlockSpec((None, tile, D_MODEL), lambda s, l: (s, l, 0)),
                   pl.BlockSpec((None, POOL_STATE, POOL_WIDTH), lambda s, l: (s, 0, 0))],
        out_shape=[jax.ShapeDtypeStruct((b, L, D_MODEL), _F32),
                   jax.ShapeDtypeStruct((b, POOL_STATE, POOL_WIDTH), _F32)],
        scratch_shapes=[pltpu.VMEM((POOL_HIST + tile, POOL_WIDTH), _F32)],
        compiler_params=_params(2),
        name="pool_prompt",
    )(x, mod, *arrays)


def _tm_proj_kernel(*refs, has_dt):
    if has_dt:
        x_ref, mod_ref, g_ref, w_ref, wdt_ref, dtb_ref, alog_ref, o_ref, dt_ref, dec_ref, hn_ref = refs
    else:
        x_ref, mod_ref, g_ref, w_ref, o_ref, hn_ref = refs

    @pl.when(pl.program_id(0) == 0)
    def _():
        hn = _prenorm(x_ref[...], g_ref[...], mod_ref[0:1], mod_ref[1:2])
        hn_ref[...] = hn.reshape(hn_ref.shape).astype(_BF16)
        if has_dt:
            dt = _softplus(_dot(hn_ref[...], wdt_ref[...]) + dtb_ref[...])
            dt_ref[...] = dt
            total = jnp.sum(dt.reshape(x_ref.shape[0], x_ref.shape[1], LANES), axis=0)
            dec_ref[...] = jnp.exp(total * (-jnp.exp(alog_ref[...])))

    o_ref[...] = _dot(hn_ref[...], w_ref[...])


def _tm_proj(x, mod, w, i, weight, n_cols, has_dt):
    seq, b, _ = x.shape
    rows = seq * b
    j = i // N_MIXERS
    ops = _operands(w, i)
    arrays = [x, mod, ops["norm_g"][0], weight]
    specs = [pl.BlockSpec((seq, b, D_MODEL), lambda n: (0, 0, 0)),
             pl.BlockSpec((3, b, D_MODEL), lambda n: (0, 0, 0)),
             ops["norm_g"][1],
             pl.BlockSpec((None, D_MODEL, PROJ_COLS), lambda n: (j, 0, n))]
    out_shape = [jax.ShapeDtypeStruct((rows, n_cols), _F32)]
    out_specs = [pl.BlockSpec((rows, PROJ_COLS), lambda n: (0, n))]
    if has_dt:
        extra, extra_specs = _pick(ops, ("w_dt", "dt_bias", "a_log"))
        arrays += extra
        specs += extra_specs
        out_shape += [jax.ShapeDtypeStruct((rows, LANES), _F32), jax.ShapeDtypeStruct((b, LANES), _F32)]
        out_specs += [pl.BlockSpec((rows, LANES), lambda n: (0, 0)), pl.BlockSpec((b, LANES), lambda n: (0, 0))]
    return pl.pallas_call(
        functools.partial(_tm_proj_kernel, has_dt=has_dt),
        grid=(n_cols // PROJ_COLS,),
        in_specs=specs,
        out_specs=out_specs,
        out_shape=out_shape,
        scratch_shapes=[pltpu.VMEM((rows, D_MODEL), _BF16)],
        compiler_params=_params(1),
        name="sample_proj",
    )(*arrays)


def _tm_ssd_kernel(dec_ref, x_ref, mod_ref, z_ref, xin_ref, bcin_ref, dt_ref,
                   cw_ref, cb_ref, alog_ref, dsk_ref, ng_ref, wout_ref, fin_ref,
                   cprev_ref, hprev_ref, exps_ref, exp64_ref, _conv_alias_ref, _ssm_alias_ref,
                   xo_ref, nconv_ref, nssm_ref,
                   xp_ref, act_ref, mfac_ref, e_ref, xw_ref, y_ref, *, nseq, seq, apply_final):
    rows = nseq * seq
    n_carry = CONV_WIDTH - 1
    xp_ref[0:n_carry] = cprev_ref[...]
    xp_ref[n_carry:n_carry + seq, :, 0:D_INNER] = xin_ref[...]
    xp_ref[n_carry:n_carry + seq, :, D_INNER:CONV_DIM] = bcin_ref[...]
    conv = cb_ref[...]
    for k in range(CONV_WIDTH):
        conv = conv + xp_ref[k:k + seq] * cw_ref[k:k + 1, :]
    act = _silu(conv).reshape(rows, CONV_DIM)
    for jb in range(CONV_DIM // LANES):
        act_ref[jb] = act[:, jb * LANES:(jb + 1) * LANES]
    nconv_ref[...] = xp_ref[seq:seq + n_carry]

    dt3 = dt_ref[...]
    a3 = dt3 * (-jnp.exp(alog_ref[...]))
    slabs = [a3[0]]
    for t in range(1, seq):
        slabs.append(slabs[-1] + a3[t])
    acum3 = jnp.stack(slabs, axis=0)
    w3 = dt3 * jnp.exp(acum3[seq - 1:seq] - acum3)
    acum = acum3.reshape(rows, LANES)

    shape3 = (seq, nseq, N_GROUPS * LANES)
    z3 = _select_lanes(acum, exps_ref[...], 3).reshape(shape3)
    dtz3 = _select_lanes(dt3.reshape(rows, LANES), exps_ref[...], 2).reshape(shape3)
    t3 = lax.broadcasted_iota(jnp.int32, shape3, 0)
    s3 = lax.broadcasted_iota(jnp.int32, shape3, 2) % seq
    diag = t3 == s3
    acum_s = jnp.sum(jnp.where(diag, z3, 0.0), axis=0, keepdims=True)
    dt_s = jnp.sum(jnp.where(diag, dtz3, 0.0), axis=0, keepdims=True)
    mfac = (jnp.exp(jnp.where(s3 <= t3, z3 - acum_s, NEG_BIG)) * dt_s).reshape(rows, N_GROUPS * LANES)
    for g in range(N_GROUPS):
        mfac_ref[g] = mfac[:, g * LANES:(g + 1) * LANES]
    e = _select_lanes(jnp.exp(acum), exp64_ref[...], 2)
    wx = _select_lanes(w3.reshape(rows, LANES), exp64_ref[...], 2)
    for jb in range(D_INNER // LANES):
        lanes = slice(jb * LANES, (jb + 1) * LANES)
        e_ref[jb] = e[:, lanes]
        xw_ref[jb] = act[:, lanes] * wx[:, lanes]

    width = HEADS_PER_GROUP * seq
    bd_r = lax.broadcasted_iota(jnp.int32, (width, GROUP_WIDTH), 0)
    bd_c = lax.broadcasted_iota(jnp.int32, (width, GROUP_WIDTH), 1)
    blockdiag = (bd_r // seq) == (bd_c // HEAD_DIM)
    zeros_b = jnp.zeros((seq, D_STATE), _F32)
    b_blk = D_INNER // LANES
    c_blk = b_blk + N_GROUPS
    zeros_x = jnp.zeros((seq, GROUP_WIDTH), _F32)
    first_seq = pl.program_id(0) * nseq

    def per_seq(s, carry):
        tokens = pl.ds(s, seq, stride=nseq)

        def pair(ref, g):
            return jnp.concatenate([ref[2 * g, tokens, :], ref[2 * g + 1, tokens, :]], axis=1)

        for g in range(N_GROUPS):
            b_g = act_ref[b_blk + g, tokens, :]
            c_bf = act_ref[c_blk + g, tokens, :].astype(_BF16)
            sl = slice(g * GROUP_WIDTH, (g + 1) * GROUP_WIDTH)
            xs_g = pair(act_ref, g)
            cbx = _dot_nt(c_bf, jnp.concatenate([b_g] * HEADS_PER_GROUP, axis=0).astype(_BF16))
            mp = (cbx * mfac_ref[g, tokens, :][:, 0:width]).astype(_BF16)
            rhs = jnp.where(blockdiag, jnp.concatenate([xs_g] * HEADS_PER_GROUP, axis=0), 0.0).astype(_BF16)
            h0 = hprev_ref[s, sl, :]
            e_g = pair(e_ref, g)
            y_g = _dot(mp, rhs) + _dot_nt(c_bf, h0.astype(_BF16)) * e_g + dsk_ref[:, sl] * xs_g
            y_ref[2 * g, tokens, :] = y_g[:, 0:LANES]
            y_ref[2 * g + 1, tokens, :] = y_g[:, LANES:GROUP_WIDTH]
            lhs_t = jnp.concatenate([pair(xw_ref, g), zeros_x], axis=0).astype(_BF16)
            rhs_s = jnp.concatenate([b_g, zeros_b], axis=0).astype(_BF16)
            ds = _dot_tn(lhs_t, rhs_s)
            for hh in range(HEADS_PER_GROUP):
                decay = dec_ref[first_seq + s, g * HEADS_PER_GROUP + hh]
                hr = slice(hh * HEAD_DIM, (hh + 1) * HEAD_DIM)
                nssm_ref[s, g * GROUP_WIDTH + hh * HEAD_DIM:g * GROUP_WIDTH + (hh + 1) * HEAD_DIM, :] = (
                    h0[hr, :] * decay + ds[hr, :])
        return carry

    lax.fori_loop(0, nseq, per_seq, 0, unroll=SSD_SEQ_UNROLL)

    z = z_ref[...].reshape(rows, D_INNER)
    parts = []
    for g in range(N_GROUPS):
        sl = slice(g * GROUP_WIDTH, (g + 1) * GROUP_WIDTH)
        gated = jnp.concatenate([y_ref[2 * g], y_ref[2 * g + 1]], axis=1) * _silu(z[:, sl])
        ms = jnp.mean(gated * gated, axis=-1, keepdims=True)
        parts.append((gated * lax.rsqrt(ms + EPS) * ng_ref[:, sl]).astype(_BF16))
    out = _dot(jnp.concatenate(parts, axis=1), wout_ref[...]).reshape(seq, nseq, D_MODEL)
    xo_ref[...] = _residual(x_ref[...], out, mod_ref[2:3], fin_ref, apply_final)


def _tm_ssd_layer(x, mod, w, i, state_conv_t, state_ssm, conv_acc, ssm_acc, apply_final):
    seq, b, _ = x.shape
    j = i // N_MIXERS
    nseq = SSD_SEQ_BLOCK
    rows = nseq * seq
    n_layers = state_ssm.shape[0]
    proj, dt, decay = _tm_proj(x, mod, w, i, w["ssd_w_in"], D_INNER + CONV_DIM, True)
    proj = proj.reshape(seq, b, D_INNER + CONV_DIM)
    dt = dt.reshape(seq, b, LANES)
    kern = functools.partial(_tm_ssd_kernel, nseq=nseq, seq=seq, apply_final=apply_final)
    arrays, specs = _pick(_operands(w, i), _SSD_SAMPLE_OPS)
    exps, exp64 = _expansion_constants(seq)
    col_block = lambda n: pl.BlockSpec((seq, nseq, WEIGHT_COLS), lambda s: (0, s, n))
    in_specs = ([pl.BlockSpec(memory_space=pltpu.SMEM),
                 pl.BlockSpec((seq, nseq, D_MODEL), lambda s: (0, s, 0)),
                 pl.BlockSpec((3, nseq, D_MODEL), lambda s: (0, s, 0)),
                 col_block(0), col_block(1), col_block(2),
                 pl.BlockSpec((seq, nseq, LANES), lambda s: (0, s, 0))]
                + specs
                + [pl.BlockSpec((None, CONV_WIDTH - 1, nseq, CONV_DIM), lambda s: (j, 0, s, 0)),
                   pl.BlockSpec((None, nseq, D_INNER, D_STATE), lambda s: (j, s, 0, 0)),
                   _wspec(exps.shape, (0, 0)), _wspec(exp64.shape, (0, 0))])
    args = [decay, x, mod, proj, proj, proj, dt, *arrays, state_conv_t, state_ssm, exps, exp64]
    n_in = len(args)
    aliases = {}
    if conv_acc is not None:
        in_specs += [pl.BlockSpec(memory_space=pl.ANY), pl.BlockSpec(memory_space=pl.ANY)]
        args += [conv_acc, ssm_acc]
        aliases = {n_in: 1, n_in + 1: 2}
        kern_fn = kern
    else:
        kern_fn = lambda *refs: kern(*refs[:n_in], None, None, *refs[n_in:])
    slab = lambda width: pltpu.VMEM((width // LANES, rows, LANES), _F32)
    return pl.pallas_call(
        kern_fn,
        grid=(b // nseq,),
        in_specs=in_specs,
        out_specs=[pl.BlockSpec((seq, nseq, D_MODEL), lambda s: (0, s, 0)),
                   pl.BlockSpec((None, CONV_WIDTH - 1, nseq, CONV_DIM), lambda s: (j, 0, s, 0)),
                   pl.BlockSpec((None, nseq, D_INNER, D_STATE), lambda s: (j, s, 0, 0))],
        out_shape=[jax.ShapeDtypeStruct((seq, b, D_MODEL), _F32),
                   jax.ShapeDtypeStruct((n_layers, CONV_WIDTH - 1, b, CONV_DIM), _F32),
                   jax.ShapeDtypeStruct((n_layers, b, D_INNER, D_STATE), _F32)],
        scratch_shapes=[pltpu.VMEM((seq + CONV_WIDTH - 1, nseq, CONV_DIM), _F32),
                        slab(CONV_DIM),
                        slab(N_GROUPS * LANES),
                        slab(D_INNER),
                        slab(D_INNER),
                        slab(D_INNER)],
        input_output_aliases=aliases,
        compiler_params=_params(1),
        name="ssd_sample",
    )(*args)


def _tm_pool_kernel(x_ref, mod_ref, g_ref, wu_ref, wz_ref, wg_ref, ps_ref, wout_ref, fin_ref, prev_ref,
                    _alias_ref, xo_ref, npool_ref, up_ref, *, nseq, seq, start_pos, apply_final):
    rows = nseq * seq
    x3 = x_ref[...]
    hn = _prenorm(x3, g_ref[...], mod_ref[0:1], mod_ref[1:2]).reshape(rows, D_MODEL).astype(_BF16)
    up_ref[0:POOL_STATE] = prev_ref[...]
    up_ref[POOL_STATE:POOL_STATE + seq] = _dot(hn, wu_ref[...]).reshape(seq, nseq, POOL_WIDTH)
    z = _dot(hn, wz_ref[...])
    pos3 = start_pos + lax.broadcasted_iota(jnp.int32, (seq, nseq, POOL_GROUP_DIM), 0)
    pos = pos3.astype(_F32).reshape(rows, POOL_GROUP_DIM)
    mixed = _pool_mix(
        lambda k, sl: up_ref[POOL_STATE - k:POOL_STATE - k + seq, :, sl].reshape(rows, POOL_GROUP_DIM),
        z, wg_ref, ps_ref, pos)
    out = _dot(mixed, wout_ref[...]).reshape(seq, nseq, D_MODEL)
    xo_ref[...] = _residual(x3, out, mod_ref[2:3], fin_ref, apply_final)
    npool_ref[...] = up_ref[seq:seq + POOL_STATE]


def _tm_pool_layer(x, mod, w, i, state_pool_t, pool_acc, start_pos, apply_final):
    seq, b, _ = x.shape
    j = i // N_MIXERS
    nseq = POOL_SEQ_BLOCK
    n_layers = state_pool_t.shape[0]
    kern = functools.partial(_tm_pool_kernel, nseq=nseq, seq=seq, start_pos=start_pos,
                             apply_final=apply_final)
    arrays, specs = _pick(_operands(w, i), _POOL_OPS)
    state_spec = pl.BlockSpec((None, POOL_STATE, nseq, POOL_WIDTH), lambda s: (j, 0, s, 0))
    in_specs = ([pl.BlockSpec((seq, nseq, D_MODEL), lambda s: (0, s, 0)),
                 pl.BlockSpec((3, nseq, D_MODEL), lambda s: (0, s, 0))] + specs + [state_spec])
    args = [x, mod, *arrays, state_pool_t]
    n_in = len(args)
    aliases = {}
    if pool_acc is not None:
        in_specs.append(pl.BlockSpec(memory_space=pl.ANY))
        args.append(pool_acc)
        aliases = {n_in: 1}
        kern_fn = kern
    else:
        kern_fn = lambda *refs: kern(*refs[:n_in], None, *refs[n_in:])
    return pl.pallas_call(
        kern_fn,
        grid=(b // nseq,),
        in_specs=in_specs,
        out_specs=[pl.BlockSpec((seq, nseq, D_MODEL), lambda s: (0, s, 0)), state_spec],
        out_shape=[jax.ShapeDtypeStruct((seq, b, D_MODEL), _F32),
                   jax.ShapeDtypeStruct((n_layers, POOL_STATE, b, POOL_WIDTH), _F32)],
        scratch_shapes=[pltpu.VMEM((POOL_STATE + seq, nseq, POOL_WIDTH), _F32)],
        input_output_aliases=aliases,
        compiler_params=_params(1),
        name="pool_sample",
    )(*args)


def _prepare_weights(norm_g, ssd_w_in, ssd_conv_w, ssd_conv_b, ssd_dt_bias, ssd_a_log, ssd_d, ssd_norm_g,
                     ssd_w_out, pool_w_in, pool_w_group, pool_scale, pool_w_out, final_norm_g):
    n_ssd = ssd_w_in.shape[0]
    n_pool = pool_w_in.shape[0]
    pad = LANES - N_HEADS
    pool_w_in_bf = pool_w_in.astype(_BF16)
    pool_w_group_bf = pool_w_group.astype(_BF16)
    return {
        "pool_w_v": _fold_group_mix(pool_w_in_bf, pool_w_group_bf),
        "norm_g": norm_g.reshape(DEPTH, 1, D_MODEL),
        "final_g": final_norm_g.reshape(1, D_MODEL),
        "ssd_w_in": ssd_w_in.astype(_BF16),
        "ssd_w_dt": jnp.pad(ssd_w_in[:, :, D_INNER + CONV_DIM:], ((0, 0), (0, 0), (0, pad))).astype(_BF16),
        "ssd_conv_w": ssd_conv_w,
        "ssd_conv_b": ssd_conv_b.reshape(n_ssd, 1, CONV_DIM),
        "ssd_dt_bias": jnp.pad(ssd_dt_bias, ((0, 0), (0, pad))).reshape(n_ssd, 1, LANES),
        "ssd_a_log": jnp.pad(ssd_a_log, ((0, 0), (0, pad))).reshape(n_ssd, 1, LANES),
        "ssd_d": jnp.repeat(ssd_d, HEAD_DIM, axis=1).reshape(n_ssd, 1, D_INNER),
        "ssd_norm_g": ssd_norm_g.reshape(n_ssd, 1, D_INNER),
        "ssd_w_out": ssd_w_out.astype(_BF16),
        "pool_w_in": pool_w_in_bf,
        "pool_w_group": pool_w_group_bf,
        "pool_scale": pool_scale.reshape(n_pool, 1, POOL_WIDTH),
        "pool_w_out": pool_w_out.astype(_BF16),
    }


def kernel(x_prompt, x_sample, state_ssm, state_conv, state_pool, c_prompt, c_sample, ada_w, ada_b, norm_g,
           ssd_w_in, ssd_conv_w, ssd_conv_b, ssd_dt_bias, ssd_a_log, ssd_d, ssd_norm_g, ssd_w_out, pool_w_in,
           pool_w_group, pool_scale, pool_w_out, final_norm_g):
    b_p = x_prompt.shape[0]
    b_s = x_sample.shape[0]
    n_ssd = state_ssm.shape[0]
    w = _prepare_weights(norm_g, ssd_w_in, ssd_conv_w, ssd_conv_b, ssd_dt_bias, ssd_a_log, ssd_d, ssd_norm_g,
                         ssd_w_out, pool_w_in, pool_w_group, pool_scale, pool_w_out, final_norm_g)
    mod = _modulation(jnp.concatenate([c_prompt, c_sample], axis=0), ada_w, ada_b)
    mod_p = mod[:, :b_p].reshape(DEPTH, b_p, 3, D_MODEL)
    mod_s = mod[:, b_p:].reshape(DEPTH, b_s, 3, D_MODEL).transpose(0, 2, 1, 3)
    ssm_in = state_ssm.reshape(n_ssd, b_s, D_INNER, D_STATE)
    conv_in = state_conv.transpose(0, 2, 1, 3)
    pool_in = state_pool.transpose(0, 2, 1, 3)

    xp, xs = x_prompt, x_sample.transpose(1, 0, 2)
    ssm_p, conv_p, pool_p = [], [], []
    ssm_s = conv_s = pool_s = None
    for i in range(DEPTH):
        last = i == DEPTH - 1
        if i % N_MIXERS == 0:
            xp, cv, st = _ssd_prompt_layer(xp, mod_p[i], w, i, last)
            conv_p.append(cv)
            ssm_p.append(st)
            xs, conv_s, ssm_s = _tm_ssd_layer(xs, mod_s[i], w, i, conv_in, ssm_in, conv_s, ssm_s, last)
        else:
            xp, ps = _pool_prompt_layer(xp, mod_p[i], w, i, last)
            pool_p.append(ps)
            xs, pool_s = _tm_pool_layer(xs, mod_s[i], w, i, pool_in, pool_s, PAST_LEN, last)

    head_shape = (N_HEADS, HEAD_DIM, D_STATE)
    return (xp, xs.transpose(1, 0, 2),
            jnp.stack(ssm_p).reshape(n_ssd, b_p, *head_shape),
            jnp.stack(conv_p), jnp.stack(pool_p),
            ssm_s.reshape(n_ssd, b_s, *head_shape),
            conv_s.transpose(0, 2, 1, 3), pool_s.transpose(0, 2, 1, 3))
```

```python
import functools

import numpy as np
import jax
import jax.numpy as jnp
from jax import lax
from jax.experimental import pallas as pl
from jax.experimental.pallas import tpu as pltpu

D_MODEL = 1024
DEPTH = 4
PAST_LEN = 16384
N_MIXERS = 2
D_INNER = 2048
HEAD_DIM = 64
N_HEADS = 32
N_GROUPS = 8
HEADS_PER_GROUP = 4
D_STATE = 128
CONV_WIDTH = 4
CONV_DIM = D_INNER + 2 * N_GROUPS * D_STATE
GROUP_WIDTH = HEADS_PER_GROUP * HEAD_DIM
POOL_WIDTH = 2048
POOL_WINDOWS = (2, 4, 8, 16)
POOL_GROUP_DIM = 512
POOL_STATE = 15
EPS = 1e-6

LANES = 128
SUBLANES = 8
CHUNK = 128
N_PHASES = CHUNK // SUBLANES
PROMPT_TILE = 1024
SSD_PROMPT_TILE = 512
SSD_SEQ_BLOCK = 8
SSD_SEQ_UNROLL = 8
POOL_SEQ_BLOCK = 16
POOL_HIST = 2 * SUBLANES
WEIGHT_COLS = 2048
PIECE_COLS = 256
PROJ_COLS = 1024
VMEM_LIMIT = 60 * 1024 * 1024
NEG_BIG = -1e30
LOG2E = 1.4426950408889634

_F32 = jnp.float32
_BF16 = jnp.bfloat16
_HI = lax.Precision.HIGHEST


def _silu(v):
    h = 0.5 * v
    return h + h * jnp.tanh(h)


def _softplus(v):
    return jnp.maximum(v, 0.0) + jnp.log(1.0 + jnp.exp(-jnp.abs(v)))


def _dot(a, b):
    return jnp.dot(a, b, preferred_element_type=_F32)


def _dot_exact(a, b):
    return jnp.dot(a, b, precision=_HI, preferred_element_type=_F32)


def _dot_nt(a, b):
    return lax.dot_general(a, b, (((1,), (1,)), ((), ())), preferred_element_type=_F32)


def _dot_tn(a, b):
    return lax.dot_general(a, b, (((0,), (0,)), ((), ())), preferred_element_type=_F32)


def _select_lanes(v, onehot_bf, terms):
    acc = None
    rest = v
    for _ in range(terms):
        piece = rest.astype(_BF16)
        part = _dot(piece, onehot_bf)
        acc = part if acc is None else acc + part
        rest = rest - piece.astype(_F32)
    return acc


def _prenorm(x, g, shift, scale):
    ms = jnp.mean(x * x, axis=-1, keepdims=True)
    return x * lax.rsqrt(ms + EPS) * g * (1.0 + scale) + shift


def _residual(x, out, gate, fin_ref, apply_final):
    y = x + (1.0 + gate) * out
    if apply_final:
        ms = jnp.mean(y * y, axis=-1, keepdims=True)
        y = y * lax.rsqrt(ms + EPS) * fin_ref[...]
    return y


def _group_rmsnorm_gate(y_ref, z_ref, ng_ref):
    parts = []
    for g in range(N_GROUPS):
        sl = slice(g * GROUP_WIDTH, (g + 1) * GROUP_WIDTH)
        gated = y_ref[:, sl] * _silu(z_ref[:, sl])
        ms = jnp.mean(gated * gated, axis=-1, keepdims=True)
        parts.append((gated * lax.rsqrt(ms + EPS) * ng_ref[:, sl]).astype(_BF16))
    return jnp.concatenate(parts, axis=1)


def _wspec(block_shape, index):
    return pl.BlockSpec(block_shape, lambda *_: index, pipeline_mode=pl.Buffered(1))


def _params(n_axes):
    return pltpu.CompilerParams(dimension_semantics=("arbitrary",) * n_axes, vmem_limit_bytes=VMEM_LIMIT)


def _operands(w, i):
    j = i // N_MIXERS
    wcols = (None, D_MODEL, WEIGHT_COLS)
    ops = {"norm_g": (w["norm_g"], _wspec((None, 1, D_MODEL), (i, 0, 0))),
           "final_g": (w["final_g"], _wspec((1, D_MODEL), (0, 0)))}
    if i % N_MIXERS == 0:
        ops.update(
            w_z=(w["ssd_w_in"], _wspec(wcols, (j, 0, 0))),
            w_x=(w["ssd_w_in"], _wspec(wcols, (j, 0, 1))),
            w_bc=(w["ssd_w_in"], _wspec(wcols, (j, 0, 2))),
            w_dt=(w["ssd_w_dt"], _wspec((None, D_MODEL, LANES), (j, 0, 0))),
            conv_w=(w["ssd_conv_w"], _wspec((None, CONV_WIDTH, CONV_DIM), (j, 0, 0))),
            conv_b=(w["ssd_conv_b"], _wspec((None, 1, CONV_DIM), (j, 0, 0))),
            dt_bias=(w["ssd_dt_bias"], _wspec((None, 1, LANES), (j, 0, 0))),
            a_log=(w["ssd_a_log"], _wspec((None, 1, LANES), (j, 0, 0))),
            d_skip=(w["ssd_d"], _wspec((None, 1, D_INNER), (j, 0, 0))),
            ssd_norm_g=(w["ssd_norm_g"], _wspec((None, 1, D_INNER), (j, 0, 0))),
            w_out=(w["ssd_w_out"], _wspec((None, D_INNER, D_MODEL), (j, 0, 0))),
        )
    else:
        ops.update(
            w_v=(w["pool_w_v"], _wspec(wcols, (j, 0, 0))),
            w_u=(w["pool_w_in"], _wspec(wcols, (j, 0, 0))),
            w_zp=(w["pool_w_in"], _wspec(wcols, (j, 0, 1))),
            w_group=(w["pool_w_group"],
                     _wspec((None, len(POOL_WINDOWS), POOL_GROUP_DIM, POOL_GROUP_DIM), (j, 0, 0, 0))),
            pool_scale=(w["pool_scale"], _wspec((None, 1, POOL_WIDTH), (j, 0, 0))),
            w_out=(w["pool_w_out"], _wspec((None, POOL_WIDTH, D_MODEL), (j, 0, 0))),
        )
    return ops


def _pick(ops, names):
    return [ops[n][0] for n in names], [ops[n][1] for n in names]


def _mod_kernel(c_ref, w_ref, b_ref, o_ref):
    sc = _silu(c_ref[...]).astype(_BF16)
    o_ref[...] = _dot(sc, w_ref[...].astype(_BF16)) + b_ref[...]


def _modulation(c_all, ada_w, ada_b):
    n = c_all.shape[0]
    tn = 1024
    return pl.pallas_call(
        _mod_kernel,
        grid=(DEPTH, 3 * D_MODEL // tn),
        in_specs=[
            pl.BlockSpec((n, D_MODEL), lambda i, j: (0, 0)),
            pl.BlockSpec((None, D_MODEL, tn), lambda i, j: (i, 0, j)),
            pl.BlockSpec((None, 1, tn), lambda i, j: (i, 0, j)),
        ],
        out_specs=pl.BlockSpec((None, n, tn), lambda i, j: (i, 0, j)),
        out_shape=jax.ShapeDtypeStruct((DEPTH, n, 3 * D_MODEL), _F32),
        compiler_params=_params(2),
        name="adaln_mod",
    )(c_all, ada_w, ada_b.reshape(DEPTH, 1, 3 * D_MODEL))


_SSD_PROMPT_OPS = ("norm_g", "w_z", "w_x", "w_bc", "w_dt", "conv_w", "conv_b", "dt_bias", "a_log", "d_skip",
                   "ssd_norm_g", "w_out", "final_g")


def _ssd_prompt_kernel(*refs, tile, n_tiles, apply_final):
    n_xcols = D_MODEL // LANES
    x_refs = refs[:n_xcols]
    (mod_ref, g_ref, wz_ref, wx_ref, wbc_ref, wdt_ref, cw_ref, cb_ref, dtb_ref, alog_ref, dsk_ref, ng_ref,
     wout_ref, fin_ref, xo_ref, nconv_ref, nssm_ref,
     ht_ref, carry_ref, act_ref, z_ref, y_ref, stage_ref) = refs[n_xcols:]
    l = pl.program_id(1)
    n_chunks = tile // CHUNK
    n_carry = CONV_WIDTH - 1

    @pl.when(l == 0)
    def _():
        ht_ref[...] = jnp.zeros_like(ht_ref)
        carry_ref[...] = jnp.zeros_like(carry_ref)

    def load_x():
        return jnp.concatenate(
            [jnp.concatenate([xr[pl.ds(c * CHUNK + r, SUBLANES, stride=N_PHASES), :] for xr in x_refs], axis=1)
             for c in range(n_chunks) for r in range(N_PHASES)], axis=0)

    x = load_x()
    hn = _prenorm(x, g_ref[...], mod_ref[0:1, :], mod_ref[1:2, :]).astype(_BF16)
    dt = _softplus(_dot(hn, wdt_ref[...]) + dtb_ref[...])

    last = (N_PHASES - n_carry) * SUBLANES
    for k in range(CONV_DIM // PIECE_COLS):
        cols = slice(k * PIECE_COLS, (k + 1) * PIECE_COLS)
        half = D_INNER // PIECE_COLS
        w_ref, kw = (wx_ref, k) if k < half else (wbc_ref, k - half)
        xb = _dot(hn, w_ref[:, kw * PIECE_COLS:(kw + 1) * PIECE_COLS])
        prev = carry_ref[:, cols]
        for c in range(n_chunks):
            cur = xb[c * CHUNK:(c + 1) * CHUNK, :]
            tail = cur[last:CHUNK, :]
            wrapped = [jnp.concatenate([prev[j * SUBLANES + SUBLANES - 1:(j + 1) * SUBLANES, :],
                                        tail[j * SUBLANES:(j + 1) * SUBLANES - 1, :]], axis=0)
                       for j in range(n_carry)]
            ext = jnp.concatenate(wrapped + [cur], axis=0)
            conv = cb_ref[:, cols]
            for kk in range(CONV_WIDTH):
                conv = conv + ext[kk * SUBLANES:kk * SUBLANES + CHUNK, :] * cw_ref[kk:kk + 1, cols]
            act_ref[c * CHUNK:(c + 1) * CHUNK, cols] = _silu(conv)
            prev = tail
        carry_ref[:, cols] = prev
        for j in range(n_carry):
            nconv_ref[j:j + 1, cols] = prev[(j + 1) * SUBLANES - 1:(j + 1) * SUBLANES, :]

    row = lax.broadcasted_iota(jnp.int32, (CHUNK, CHUNK), 0)
    col = lax.broadcasted_iota(jnp.int32, (CHUNK, CHUNK), 1)
    token = lambda p: (p % SUBLANES) * N_PHASES + p // SUBLANES
    causal = token(col) <= token(row)
    tril = causal.astype(_F32)
    head_of_lane = lax.broadcasted_iota(jnp.int32, (CHUNK, GROUP_WIDTH), 1) // HEAD_DIM
    low_half = lax.broadcasted_iota(jnp.int32, (CHUNK, LANES), 1) < HEAD_DIM
    neg_a = -jnp.exp(alog_ref[...])

    def decay_stage(c):
        dt_c = dt[c * CHUNK:(c + 1) * CHUNK, :]
        acum = _dot_exact(tril, dt_c * neg_a)
        acum2 = acum * LOG2E
        w_c = dt_c * jnp.exp(acum[CHUNK - 1:CHUNK, :] - acum)
        srow_t = (jnp.log(dt_c) * LOG2E - acum2).T
        return acum2, srow_t, w_c.T

    def cb_stage(c, g):
        rows = slice(c * CHUNK, (c + 1) * CHUNK)
        b_g = act_ref[rows, D_INNER + g * D_STATE:D_INNER + (g + 1) * D_STATE]
        c_lo = D_INNER + (N_GROUPS + g) * D_STATE
        c_bf = act_ref[rows, c_lo:c_lo + D_STATE].astype(_BF16)
        bt_g = b_g.T
        return c_bf, bt_g, _dot(c_bf, bt_g.astype(_BF16))

    def operand_stage(c, g, decay, cbs):
        acum2, srow_t, w_t = decay
        _, bt_g, cb = cbs
        xs_g = act_ref[c * CHUNK:(c + 1) * CHUNK, g * GROUP_WIDTH:(g + 1) * GROUP_WIDTH]
        rhs = jnp.concatenate(
            [jnp.where(head_of_lane == hh, xs_g, 0.0).astype(_BF16) for hh in range(HEADS_PER_GROUP)],
            axis=0)
        m_parts, w_parts, e_cols = [], [], []
        for hh in range(HEADS_PER_GROUP):
            h = g * HEADS_PER_GROUP + hh
            acum_bc = jnp.broadcast_to(acum2[:, h:h + 1], (CHUNK, CHUNK))
            seg = acum_bc + srow_t[h:h + 1, :]
            m_parts.append((cb * jnp.exp2(jnp.where(causal, seg, NEG_BIG))).astype(_BF16))
            w_parts.append((bt_g * w_t[h:h + 1, :]).astype(_BF16))
            e_cols.append(jnp.exp2(acum_bc))
        lhs = jnp.concatenate(
            [jnp.concatenate(m_parts, axis=1), jnp.concatenate(w_parts, axis=1)], axis=0)
        e_g = jnp.concatenate(
            [jnp.where(low_half, e_cols[0], e_cols[1]), jnp.where(low_half, e_cols[2], e_cols[3])],
            axis=1)
        return lhs, rhs, e_g

    def output_stage(c, g, cbs, e_g, out):
        rows = slice(c * CHUNK, (c + 1) * CHUNK)
        sl = slice(g * GROUP_WIDTH, (g + 1) * GROUP_WIDTH)
        ht_g = ht_ref[g]
        y_inter = _dot(cbs[0], ht_g.astype(_BF16))
        y_ref[rows, sl] = out[0:CHUNK, :] + y_inter * e_g + dsk_ref[:, sl] * act_ref[rows, sl]
        ht_ref[g] = ht_g * e_g[CHUNK - 1:CHUNK, :] + out[CHUNK:2 * CHUNK, :]

    def gate_proj_piece(p):
        if p < D_INNER // PIECE_COLS:
            zc = slice(p * PIECE_COLS, (p + 1) * PIECE_COLS)
            z_ref[:, zc] = _dot(hn, wz_ref[:, zc])

    groups = range(N_GROUPS)
    decays = [decay_stage(c) for c in range(n_chunks)]
    cbs = [[cb_stage(c, g) for g in groups] for c in range(n_chunks)]
    operands = [operand_stage(0, g, decays[0], cbs[0][g]) for g in groups]
    for c in range(n_chunks):
        outs, nxt = [], []
        for g in groups:
            lhs, rhs, _ = operands[g]
            outs.append(_dot(lhs, rhs))
            if c + 1 < n_chunks:
                nxt.append(operand_stage(c + 1, g, decays[c + 1], cbs[c + 1][g]))
        for g in groups:
            output_stage(c, g, cbs[c][g], operands[g][2], outs[g])
            gate_proj_piece(c * N_GROUPS + g)
        operands = nxt

    gn = _group_rmsnorm_gate(y_ref, z_ref, ng_ref)
    xo = _residual(x, _dot(gn, wout_ref[...]), mod_ref[2:3, :], fin_ref, apply_final)
    for c in range(n_chunks):
        for r in range(N_PHASES):
            p0 = c * CHUNK + r * SUBLANES
            for j in range(n_xcols):
                stage_ref[j, pl.ds(c * CHUNK + r, SUBLANES, stride=N_PHASES), :] = (
                    xo[p0:p0 + SUBLANES, j * LANES:(j + 1) * LANES])
    for j in range(n_xcols):
        xo_ref[:, j * LANES:(j + 1) * LANES] = stage_ref[j]

    @pl.when(l == n_tiles - 1)
    def _():
        for g in range(N_GROUPS):
            nssm_ref[g * GROUP_WIDTH:(g + 1) * GROUP_WIDTH, :] = ht_ref[g].T


def _ssd_prompt_layer(x, mod, w, i, apply_final):
    b, L, _ = x.shape
    tile = SSD_PROMPT_TILE
    n_tiles = L // tile
    assert tile // CHUNK * N_GROUPS >= D_INNER // PIECE_COLS
    kern = functools.partial(_ssd_prompt_kernel, tile=tile, n_tiles=n_tiles, apply_final=apply_final)
    arrays, specs = _pick(_operands(w, i), _SSD_PROMPT_OPS)
    n_xcols = D_MODEL // LANES
    x_specs = [pl.BlockSpec((None, tile, LANES), lambda s, l, j=j: (s, l, j)) for j in range(n_xcols)]
    return pl.pallas_call(
        kern,
        grid=(b, n_tiles),
        in_specs=x_specs + [pl.BlockSpec((None, 3, D_MODEL), lambda s, l: (s, 0, 0))] + specs,
        out_specs=[pl.BlockSpec((None, tile, D_MODEL), lambda s, l: (s, l, 0)),
                   pl.BlockSpec((None, CONV_WIDTH - 1, CONV_DIM), lambda s, l: (s, 0, 0)),
                   pl.BlockSpec((None, D_INNER, D_STATE), lambda s, l: (s, 0, 0))],
        out_shape=[jax.ShapeDtypeStruct((b, L, D_MODEL), _F32),
                   jax.ShapeDtypeStruct((b, CONV_WIDTH - 1, CONV_DIM), _F32),
                   jax.ShapeDtypeStruct((b, D_INNER, D_STATE), _F32)],
        scratch_shapes=[pltpu.VMEM((N_GROUPS, D_STATE, GROUP_WIDTH), _F32),
                        pltpu.VMEM(((CONV_WIDTH - 1) * SUBLANES, CONV_DIM), _F32),
                        pltpu.VMEM((tile, CONV_DIM), _F32),
                        pltpu.VMEM((tile, D_INNER), _F32),
                        pltpu.VMEM((tile, D_INNER), _F32),
                        pltpu.VMEM((n_xcols, tile, LANES), _F32)],
        compiler_params=_params(2),
        name="ssd_prompt",
    )(*([x] * n_xcols), mod, *arrays)


_SSD_SAMPLE_OPS = ("conv_w", "conv_b", "a_log", "d_skip", "ssd_norm_g", "w_out", "final_g")


def _expansion_constants(seq):
    exps = np.zeros((LANES, N_GROUPS * LANES), np.float32)
    for h in range(N_HEADS):
        g, hh = divmod(h, HEADS_PER_GROUP)
        exps[h, g * LANES + hh * seq:g * LANES + (hh + 1) * seq] = 1.0
    exp64 = np.zeros((LANES, D_INNER), np.float32)
    for h in range(N_HEADS):
        exp64[h, h * HEAD_DIM:(h + 1) * HEAD_DIM] = 1.0
    return jnp.asarray(exps, _BF16), jnp.asarray(exp64, _BF16)


_POOL_OPS = ("norm_g", "w_u", "w_zp", "w_group", "pool_scale", "w_out", "final_g")


def _pool_mix(load_rows, z, wg_ref, ps_ref, pos):
    parts = []
    for g, w in enumerate(POOL_WINDOWS):
        sl = slice(g * POOL_GROUP_DIM, (g + 1) * POOL_GROUP_DIM)
        u_g = load_rows(0, sl)
        win = u_g
        for k in range(1, w):
            win = win + load_rows(k, sl)
        cnt = jnp.minimum(float(w), pos + 1.0)
        pooled = win / cnt - u_g
        mixed = pooled if wg_ref is None else _dot(pooled.astype(_BF16), wg_ref[g])
        parts.append((mixed * ps_ref[:, sl] * _silu(z[:, sl])).astype(_BF16))
    return jnp.concatenate(parts, axis=1)


def _fold_kernel(wu_ref, wg_ref, o_ref):
    o_ref[...] = _dot(wu_ref[...], wg_ref[...]).astype(_BF16)


def _fold_group_mix(w_in, w_group):
    n_layers = w_in.shape[0]
    n_groups = len(POOL_WINDOWS)
    return pl.pallas_call(
        _fold_kernel,
        grid=(n_layers, n_groups),
        in_specs=[pl.BlockSpec((None, D_MODEL, POOL_GROUP_DIM), lambda j, g: (j, 0, g)),
                  pl.BlockSpec((None, None, POOL_GROUP_DIM, POOL_GROUP_DIM), lambda j, g: (j, g, 0, 0))],
        out_specs=pl.BlockSpec((None, D_MODEL, POOL_GROUP_DIM), lambda j, g: (j, 0, g)),
        out_shape=jax.ShapeDtypeStruct((n_layers, D_MODEL, POOL_WIDTH), _BF16),
        compiler_params=_params(2),
        name="fold_group_mix",
    )(w_in, w_group)


def _pool_prompt_kernel(x_ref, mod_ref, g_ref, wv_ref, wu_ref, wz_ref, ps_ref, wout_ref, fin_ref,
                        xo_ref, npool_ref, vp_ref, *, tile, n_tiles, apply_final):
    l = pl.program_id(1)
    hist = POOL_HIST

    @pl.when(l == 0)
    def _():
        vp_ref[0:hist, :] = jnp.zeros((hist, POOL_WIDTH), _F32)

    x = x_ref[...]
    hn = _prenorm(x, g_ref[...], mod_ref[0:1, :], mod_ref[1:2, :]).astype(_BF16)
    vp_ref[hist:hist + tile, :] = _dot(hn, wv_ref[...])
    z = _dot(hn, wz_ref[...])
    pos = (l * tile + lax.broadcasted_iota(jnp.int32, (tile, POOL_GROUP_DIM), 0)).astype(_F32)
    mixed = _pool_mix(lambda k, sl: vp_ref[hist - k:hist - k + tile, sl], z, None, ps_ref, pos)
    xo_ref[...] = _residual(x, _dot(mixed, wout_ref[...]), mod_ref[2:3, :], fin_ref, apply_final)
    vp_ref[hist - POOL_STATE:hist, :] = vp_ref[hist + tile - POOL_STATE:hist + tile, :]

    @pl.when(l == n_tiles - 1)
    def _():
        u_tail = _dot(hn[tile - POOL_HIST:tile, :], wu_ref[...])
        npool_ref[...] = u_tail[POOL_HIST - POOL_STATE:POOL_HIST, :]


_POOL_PROMPT_OPS = ("norm_g", "w_v", "w_u", "w_zp", "pool_scale", "w_out", "final_g")


def _pool_prompt_layer(x, mod, w, i, apply_final):
    b, L, _ = x.shape
    tile = PROMPT_TILE
    n_tiles = L // tile
    kern = functools.partial(_pool_prompt_kernel, tile=tile, n_tiles=n_tiles, apply_final=apply_final)
    arrays, specs = _pick(_operands(w, i), _POOL_PROMPT_OPS)
    return pl.pallas_call(
        kern,
        grid=(b, n_tiles),
        in_specs=[pl.BlockSpec((None, tile, D_MODEL), lambda s, l: (s, l, 0)),
                  pl.BlockSpec((None, 3, D_MODEL), lambda s, l: (s, 0, 0))] + specs,
        out_specs=[pl.BlockSpec((None, tile, D_MODEL), lambda s, l: (s, l, 0)),
                   pl.BlockSpec((None, POOL_STATE, POOL_WIDTH), lambda s, l: (s, 0, 0))],
        out_shape=[jax.ShapeDtypeStruct((b, L, D_MODEL), _F32),
                   jax.ShapeDtypeStruct((b, POOL_STATE, POOL_WIDTH), _F32)],
        scratch_shapes=[pltpu.VMEM((POOL_HIST + tile, POOL_WIDTH), _F32)],
        compiler_params=_params(2),
        name="pool_prompt",
    )(x, mod, *arrays)


def _tm_proj_kernel(*refs, has_dt):
    if has_dt:
        x_ref, mod_ref, g_ref, w_ref, wdt_ref, dtb_ref, alog_ref, o_ref, dt_ref, dec_ref, hn_ref = refs
    else:
        x_ref, mod_ref, g_ref, w_ref, o_ref, hn_ref = refs

    @pl.when(pl.program_id(0) == 0)
    def _():
        hn = _prenorm(x_ref[...], g_ref[...], mod_ref[0:1], mod_ref[1:2])
        hn_ref[...] = hn.reshape(hn_ref.shape).astype(_BF16)
        if has_dt:
            dt = _softplus(_dot(hn_ref[...], wdt_ref[...]) + dtb_ref[...])
            dt_ref[...] = dt
            total = jnp.sum(dt.reshape(x_ref.shape[0], x_ref.shape[1], LANES), axis=0)
            dec_ref[...] = jnp.exp(total * (-jnp.exp(alog_ref[...])))

    o_ref[...] = _dot(hn_ref[...], w_ref[...])


def _tm_proj(x, mod, w, i, weight, n_cols, has_dt):
    seq, b, _ = x.shape
    rows = seq * b
    j = i // N_MIXERS
    ops = _operands(w, i)
    arrays = [x, mod, ops["norm_g"][0], weight]
    specs = [pl.BlockSpec((seq, b, D_MODEL), lambda n: (0, 0, 0)),
             pl.BlockSpec((3, b, D_MODEL), lambda n: (0, 0, 0)),
             ops["norm_g"][1],
             pl.BlockSpec((None, D_MODEL, PROJ_COLS), lambda n: (j, 0, n))]
    out_shape = [jax.ShapeDtypeStruct((rows, n_cols), _F32)]
    out_specs = [pl.BlockSpec((rows, PROJ_COLS), lambda n: (0, n))]
    if has_dt:
        extra, extra_specs = _pick(ops, ("w_dt", "dt_bias", "a_log"))
        arrays += extra
        specs += extra_specs
        out_shape += [jax.ShapeDtypeStruct((rows, LANES), _F32), jax.ShapeDtypeStruct((b, LANES), _F32)]
        out_specs += [pl.BlockSpec((rows, LANES), lambda n: (0, 0)), pl.BlockSpec((b, LANES), lambda n: (0, 0))]
    return pl.pallas_call(
        functools.partial(_tm_proj_kernel, has_dt=has_dt),
        grid=(n_cols // PROJ_COLS,),
        in_specs=specs,
        out_specs=out_specs,
        out_shape=out_shape,
        scratch_shapes=[pltpu.VMEM((rows, D_MODEL), _BF16)],
        compiler_params=_params(1),
        name="sample_proj",
    )(*arrays)


def _tm_ssd_kernel(dec_ref, x_ref, mod_ref, z_ref, xin_ref, bcin_ref, dt_ref,
                   cw_ref, cb_ref, alog_ref, dsk_ref, ng_ref, wout_ref, fin_ref,
                   cprev_ref, hprev_ref, exps_ref, exp64_ref, _conv_alias_ref, _ssm_alias_ref,
                   xo_ref, nconv_ref, nssm_ref,
                   xp_ref, act_ref, mfac_ref, e_ref, xw_ref, y_ref, *, nseq, seq, apply_final):
    rows = nseq * seq
    n_carry = CONV_WIDTH - 1
    xp_ref[0:n_carry] = cprev_ref[...]
    xp_ref[n_carry:n_carry + seq, :, 0:D_INNER] = xin_ref[...]
    xp_ref[n_carry:n_carry + seq, :, D_INNER:CONV_DIM] = bcin_ref[...]
    conv = cb_ref[...]
    for k in range(CONV_WIDTH):
        conv = conv + xp_ref[k:k + seq] * cw_ref[k:k + 1, :]
    act = _silu(conv).reshape(rows, CONV_DIM)
    for jb in range(CONV_DIM // LANES):
        act_ref[jb] = act[:, jb * LANES:(jb + 1) * LANES]
    nconv_ref[...] = xp_ref[seq:seq + n_carry]

    dt3 = dt_ref[...]
    a3 = dt3 * (-jnp.exp(alog_ref[...]))
    slabs = [a3[0]]
    for t in range(1, seq):
        slabs.append(slabs[-1] + a3[t])
    acum3 = jnp.stack(slabs, axis=0)
    w3 = dt3 * jnp.exp(acum3[seq - 1:seq] - acum3)
    acum = acum3.reshape(rows, LANES)

    shape3 = (seq, nseq, N_GROUPS * LANES)
    z3 = _select_lanes(acum, exps_ref[...], 3).reshape(shape3)
    dtz3 = _select_lanes(dt3.reshape(rows, LANES), exps_ref[...], 2).reshape(shape3)
    t3 = lax.broadcasted_iota(jnp.int32, shape3, 0)
    s3 = lax.broadcasted_iota(jnp.int32, shape3, 2) % seq
    diag = t3 == s3
    acum_s = jnp.sum(jnp.where(diag, z3, 0.0), axis=0, keepdims=True)
    dt_s = jnp.sum(jnp.where(diag, dtz3, 0.0), axis=0, keepdims=True)
    mfac = (jnp.exp(jnp.where(s3 <= t3, z3 - acum_s, NEG_BIG)) * dt_s).reshape(rows, N_GROUPS * LANES)
    for g in range(N_GROUPS):
        mfac_ref[g] = mfac[:, g * LANES:(g + 1) * LANES]
    e = _select_lanes(jnp.exp(acum), exp64_ref[...], 2)
    wx = _select_lanes(w3.reshape(rows, LANES), exp64_ref[...], 2)
    for jb in range(D_INNER // LANES):
        lanes = slice(jb * LANES, (jb + 1) * LANES)
        e_ref[jb] = e[:, lanes]
        xw_ref[jb] = act[:, lanes] * wx[:, lanes]

    width = HEADS_PER_GROUP * seq
    bd_r = lax.broadcasted_iota(jnp.int32, (width, GROUP_WIDTH), 0)
    bd_c = lax.broadcasted_iota(jnp.int32, (width, GROUP_WIDTH), 1)
    blockdiag = (bd_r // seq) == (bd_c // HEAD_DIM)
    zeros_b = jnp.zeros((seq, D_STATE), _F32)
    b_blk = D_INNER // LANES
    c_blk = b_blk + N_GROUPS
    zeros_x = jnp.zeros((seq, GROUP_WIDTH), _F32)
    first_seq = pl.program_id(0) * nseq

    def per_seq(s, carry):
        tokens = pl.ds(s, seq, stride=nseq)

        def pair(ref, g):
            return jnp.concatenate([ref[2 * g, tokens, :], ref[2 * g + 1, tokens, :]], axis=1)

        for g in range(N_GROUPS):
            b_g = act_ref[b_blk + g, tokens, :]
            c_bf = act_ref[c_blk + g, tokens, :].astype(_BF16)
            sl = slice(g * GROUP_WIDTH, (g + 1) * GROUP_WIDTH)
            xs_g = pair(act_ref, g)
            cbx = _dot_nt(c_bf, jnp.concatenate([b_g] * HEADS_PER_GROUP, axis=0).astype(_BF16))
            mp = (cbx * mfac_ref[g, tokens, :][:, 0:width]).astype(_BF16)
            rhs = jnp.where(blockdiag, jnp.concatenate([xs_g] * HEADS_PER_GROUP, axis=0), 0.0).astype(_BF16)
            h0 = hprev_ref[s, sl, :]
            e_g = pair(e_ref, g)
            y_g = _dot(mp, rhs) + _dot_nt(c_bf, h0.astype(_BF16)) * e_g + dsk_ref[:, sl] * xs_g
            y_ref[2 * g, tokens, :] = y_g[:, 0:LANES]
            y_ref[2 * g + 1, tokens, :] = y_g[:, LANES:GROUP_WIDTH]
            lhs_t = jnp.concatenate([pair(xw_ref, g), zeros_x], axis=0).astype(_BF16)
            rhs_s = jnp.concatenate([b_g, zeros_b], axis=0).astype(_BF16)
            ds = _dot_tn(lhs_t, rhs_s)
            for hh in range(HEADS_PER_GROUP):
                decay = dec_ref[first_seq + s, g * HEADS_PER_GROUP + hh]
                hr = slice(hh * HEAD_DIM, (hh + 1) * HEAD_DIM)
                nssm_ref[s, g * GROUP_WIDTH + hh * HEAD_DIM:g * GROUP_WIDTH + (hh + 1) * HEAD_DIM, :] = (
                    h0[hr, :] * decay + ds[hr, :])
        return carry

    lax.fori_loop(0, nseq, per_seq, 0, unroll=SSD_SEQ_UNROLL)

    z = z_ref[...].reshape(rows, D_INNER)
    parts = []
    for g in range(N_GROUPS):
        sl = slice(g * GROUP_WIDTH, (g + 1) * GROUP_WIDTH)
        gated = jnp.concatenate([y_ref[2 * g], y_ref[2 * g + 1]], axis=1) * _silu(z[:, sl])
        ms = jnp.mean(gated * gated, axis=-1, keepdims=True)
        parts.append((gated * lax.rsqrt(ms + EPS) * ng_ref[:, sl]).astype(_BF16))
    out = _dot(jnp.concatenate(parts, axis=1), wout_ref[...]).reshape(seq, nseq, D_MODEL)
    xo_ref[...] = _residual(x_ref[...], out, mod_ref[2:3], fin_ref, apply_final)


def _tm_ssd_layer(x, mod, w, i, state_conv_t, state_ssm, conv_acc, ssm_acc, apply_final):
    seq, b, _ = x.shape
    j = i // N_MIXERS
    nseq = SSD_SEQ_BLOCK
    rows = nseq * seq
    n_layers = state_ssm.shape[0]
    proj, dt, decay = _tm_proj(x, mod, w, i, w["ssd_w_in"], D_INNER + CONV_DIM, True)
    proj = proj.reshape(seq, b, D_INNER + CONV_DIM)
    dt = dt.reshape(seq, b, LANES)
    kern = functools.partial(_tm_ssd_kernel, nseq=nseq, seq=seq, apply_final=apply_final)
    arrays, specs = _pick(_operands(w, i), _SSD_SAMPLE_OPS)
    exps, exp64 = _expansion_constants(seq)
    col_block = lambda n: pl.BlockSpec((seq, nseq, WEIGHT_COLS), lambda s: (0, s, n))
    in_specs = ([pl.BlockSpec(memory_space=pltpu.SMEM),
                 pl.BlockSpec((seq, nseq, D_MODEL), lambda s: (0, s, 0)),
                 pl.BlockSpec((3, nseq, D_MODEL), lambda s: (0, s, 0)),
                 col_block(0), col_block(1), col_block(2),
                 pl.BlockSpec((seq, nseq, LANES), lambda s: (0, s, 0))]
                + specs
                + [pl.BlockSpec((None, CONV_WIDTH - 1, nseq, CONV_DIM), lambda s: (j, 0, s, 0)),
                   pl.BlockSpec((None, nseq, D_INNER, D_STATE), lambda s: (j, s, 0, 0)),
                   _wspec(exps.shape, (0, 0)), _wspec(exp64.shape, (0, 0))])
    args = [decay, x, mod, proj, proj, proj, dt, *arrays, state_conv_t, state_ssm, exps, exp64]
    n_in = len(args)
    aliases = {}
    if conv_acc is not None:
        in_specs += [pl.BlockSpec(memory_space=pl.ANY), pl.BlockSpec(memory_space=pl.ANY)]
        args += [conv_acc, ssm_acc]
        aliases = {n_in: 1, n_in + 1: 2}
        kern_fn = kern
    else:
        kern_fn = lambda *refs: kern(*refs[:n_in], None, None, *refs[n_in:])
    slab = lambda width: pltpu.VMEM((width // LANES, rows, LANES), _F32)
    return pl.pallas_call(
        kern_fn,
        grid=(b // nseq,),
        in_specs=in_specs,
        out_specs=[pl.BlockSpec((seq, nseq, D_MODEL), lambda s: (0, s, 0)),
                   pl.BlockSpec((None, CONV_WIDTH - 1, nseq, CONV_DIM), lambda s: (j, 0, s, 0)),
                   pl.BlockSpec((None, nseq, D_INNER, D_STATE), lambda s: (j, s, 0, 0))],
        out_shape=[jax.ShapeDtypeStruct((seq, b, D_MODEL), _F32),
                   jax.ShapeDtypeStruct((n_layers, CONV_WIDTH - 1, b, CONV_DIM), _F32),
                   jax.ShapeDtypeStruct((n_layers, b, D_INNER, D_STATE), _F32)],
        scratch_shapes=[pltpu.VMEM((seq + CONV_WIDTH - 1, nseq, CONV_DIM), _F32),
                        slab(CONV_DIM),
                        slab(N_GROUPS * LANES),
                        slab(D_INNER),
                        slab(D_INNER),
                        slab(D_INNER)],
        input_output_aliases=aliases,
        compiler_params=_params(1),
        name="ssd_sample",
    )(*args)


def _tm_pool_kernel(x_ref, mod_ref, g_ref, wu_ref, wz_ref, wg_ref, ps_ref, wout_ref, fin_ref, prev_ref,
                    _alias_ref, xo_ref, npool_ref, up_ref, *, nseq, seq, start_pos, apply_final):
    rows = nseq * seq
    x3 = x_ref[...]
    hn = _prenorm(x3, g_ref[...], mod_ref[0:1], mod_ref[1:2]).reshape(rows, D_MODEL).astype(_BF16)
    up_ref[0:POOL_STATE] = prev_ref[...]
    up_ref[POOL_STATE:POOL_STATE + seq] = _dot(hn, wu_ref[...]).reshape(seq, nseq, POOL_WIDTH)
    z = _dot(hn, wz_ref[...])
    pos3 = start_pos + lax.broadcasted_iota(jnp.int32, (seq, nseq, POOL_GROUP_DIM), 0)
    pos = pos3.astype(_F32).reshape(rows, POOL_GROUP_DIM)
    mixed = _pool_mix(
        lambda k, sl: up_ref[POOL_STATE - k:POOL_STATE - k + seq, :, sl].reshape(rows, POOL_GROUP_DIM),
        z, wg_ref, ps_ref, pos)
    out = _dot(mixed, wout_ref[...]).reshape(seq, nseq, D_MODEL)
    xo_ref[...] = _residual(x3, out, mod_ref[2:3], fin_ref, apply_final)
    npool_ref[...] = up_ref[seq:seq + POOL_STATE]


def _tm_pool_layer(x, mod, w, i, state_pool_t, pool_acc, start_pos, apply_final):
    seq, b, _ = x.shape
    j = i // N_MIXERS
    nseq = POOL_SEQ_BLOCK
    n_layers = state_pool_t.shape[0]
    kern = functools.partial(_tm_pool_kernel, nseq=nseq, seq=seq, start_pos=start_pos,
                             apply_final=apply_final)
    arrays, specs = _pick(_operands(w, i), _POOL_OPS)
    state_spec = pl.BlockSpec((None, POOL_STATE, nseq, POOL_WIDTH), lambda s: (j, 0, s, 0))
    in_specs = ([pl.BlockSpec((seq, nseq, D_MODEL), lambda s: (0, s, 0)),
                 pl.BlockSpec((3, nseq, D_MODEL), lambda s: (0, s, 0))] + specs + [state_spec])
    args = [x, mod, *arrays, state_pool_t]
    n_in = len(args)
    aliases = {}
    if pool_acc is not None:
        in_specs.append(pl.BlockSpec(memory_space=pl.ANY))
        args.append(pool_acc)
        aliases = {n_in: 1}
        kern_fn = kern
    else:
        kern_fn = lambda *refs: kern(*refs[:n_in], None, *refs[n_in:])
    return pl.pallas_call(
        kern_fn,
        grid=(b // nseq,),
        in_specs=in_specs,
        out_specs=[pl.BlockSpec((seq, nseq, D_MODEL), lambda s: (0, s, 0)), state_spec],
        out_shape=[jax.ShapeDtypeStruct((seq, b, D_MODEL), _F32),
                   jax.ShapeDtypeStruct((n_layers, POOL_STATE, b, POOL_WIDTH), _F32)],
        scratch_shapes=[pltpu.VMEM((POOL_STATE + seq, nseq, POOL_WIDTH), _F32)],
        input_output_aliases=aliases,
        compiler_params=_params(1),
        name="pool_sample",
    )(*args)


def _prepare_weights(norm_g, ssd_w_in, ssd_conv_w, ssd_conv_b, ssd_dt_bias, ssd_a_log, ssd_d, ssd_norm_g,
                     ssd_w_out, pool_w_in, pool_w_group, pool_scale, pool_w_out, final_norm_g):
    n_ssd = ssd_w_in.shape[0]
    n_pool = pool_w_in.shape[0]
    pad = LANES - N_HEADS
    pool_w_in_bf = pool_w_in.astype(_BF16)
    pool_w_group_bf = pool_w_group.astype(_BF16)
    return {
        "pool_w_v": _fold_group_mix(pool_w_in_bf, pool_w_group_bf),
        "norm_g": norm_g.reshape(DEPTH, 1, D_MODEL),
        "final_g": final_norm_g.reshape(1, D_MODEL),
        "ssd_w_in": ssd_w_in.astype(_BF16),
        "ssd_w_dt": jnp.pad(ssd_w_in[:, :, D_INNER + CONV_DIM:], ((0, 0), (0, 0), (0, pad))).astype(_BF16),
        "ssd_conv_w": ssd_conv_w,
        "ssd_conv_b": ssd_conv_b.reshape(n_ssd, 1, CONV_DIM),
        "ssd_dt_bias": jnp.pad(ssd_dt_bias, ((0, 0), (0, pad))).reshape(n_ssd, 1, LANES),
        "ssd_a_log": jnp.pad(ssd_a_log, ((0, 0), (0, pad))).reshape(n_ssd, 1, LANES),
        "ssd_d": jnp.repeat(ssd_d, HEAD_DIM, axis=1).reshape(n_ssd, 1, D_INNER),
        "ssd_norm_g": ssd_norm_g.reshape(n_ssd, 1, D_INNER),
        "ssd_w_out": ssd_w_out.astype(_BF16),
        "pool_w_in": pool_w_in_bf,
        "pool_w_group": pool_w_group_bf,
        "pool_scale": pool_scale.reshape(n_pool, 1, POOL_WIDTH),
        "pool_w_out": pool_w_out.astype(_BF16),
    }


def kernel(x_prompt, x_sample, state_ssm, state_conv, state_pool, c_prompt, c_sample, ada_w, ada_b, norm_g,
           ssd_w_in, ssd_conv_w, ssd_conv_b, ssd_dt_bias, ssd_a_log, ssd_d, ssd_norm_g, ssd_w_out, pool_w_in,
           pool_w_group, pool_scale, pool_w_out, final_norm_g):
    b_p = x_prompt.shape[0]
    b_s = x_sample.shape[0]
    n_ssd = state_ssm.shape[0]
    w = _prepare_weights(norm_g, ssd_w_in, ssd_conv_w, ssd_conv_b, ssd_dt_bias, ssd_a_log, ssd_d, ssd_norm_g,
                         ssd_w_out, pool_w_in, pool_w_group, pool_scale, pool_w_out, final_norm_g)
    mod = _modulation(jnp.concatenate([c_prompt, c_sample], axis=0), ada_w, ada_b)
    mod_p = mod[:, :b_p].reshape(DEPTH, b_p, 3, D_MODEL)
    mod_s = mod[:, b_p:].reshape(DEPTH, b_s, 3, D_MODEL).transpose(0, 2, 1, 3)
    ssm_in = state_ssm.reshape(n_ssd, b_s, D_INNER, D_STATE)
    conv_in = state_conv.transpose(0, 2, 1, 3)
    pool_in = state_pool.transpose(0, 2, 1, 3)

    xp, xs = x_prompt, x_sample.transpose(1, 0, 2)
    ssm_p, conv_p, pool_p = [], [], []
    ssm_s = conv_s = pool_s = None
    for i in range(DEPTH):
        last = i == DEPTH - 1
        if i % N_MIXERS == 0:
            xp, cv, st = _ssd_prompt_layer(xp, mod_p[i], w, i, last)
            conv_p.append(cv)
            ssm_p.append(st)
            xs, conv_s, ssm_s = _tm_ssd_layer(xs, mod_s[i], w, i, conv_in, ssm_in, conv_s, ssm_s, last)
        else:
            xp, ps = _pool_prompt_layer(xp, mod_p[i], w, i, last)
            pool_p.append(ps)
            xs, pool_s = _tm_pool_layer(xs, mod_s[i], w, i, pool_in, pool_s, PAST_LEN, last)

    head_shape = (N_HEADS, HEAD_DIM, D_STATE)
    return (xp, xs.transpose(1, 0, 2),
            jnp.stack(ssm_p).reshape(n_ssd, b_p, *head_shape),
            jnp.stack(conv_p), jnp.stack(pool_p),
            ssm_s.reshape(n_ssd, b_s, *head_shape),
            conv_s.transpose(0, 2, 1, 3), pool_s.transpose(0, 2, 1, 3))
```
